```python
import math
import jax, jax.numpy as jnp
from jax import lax
import numpy as np

D_MODEL = 1024
BATCH = 8
SEQ = 2048
DEPTH = 2
DEC_BATCH = 128
DEC_SEQ = 4
PAST_LEN = 16384
PAGE_SIZE = 128

RW_HEADS = 4
RW_DIM = 64
RW_WIDTH = RW_HEADS * RW_DIM
RW_DECAY_LORA = 64
RW_AAA_LORA = 64
RW_GATE_LORA = 128
RW_PROJ = 3 * RW_WIDTH + RW_DECAY_LORA + RW_AAA_LORA + RW_GATE_LORA
RW_SPLITS = [RW_WIDTH, 2 * RW_WIDTH, 3 * RW_WIDTH, 3 * RW_WIDTH + RW_DECAY_LORA,
             3 * RW_WIDTH + RW_DECAY_LORA + RW_AAA_LORA]
RW_GN_EPS = 64e-5
HG_HEADS = 4
HG_DIM = 128
HG_WIDTH = HG_HEADS * HG_DIM
HG_PROJ = 4 * HG_WIDTH
HG_CHUNK = 16
RMS_EPS = 1e-6
F_MIN = 1e-30
CM_HEADS = 4
CM_DIM = 64
CM_WIDTH = CM_HEADS * CM_DIM
CM_CHUNK = 128
CM_PROJ = 2 * CM_WIDTH
MIX_WIDTH = RW_WIDTH + HG_WIDTH + CM_WIDTH
IN_PROJ = RW_PROJ + HG_PROJ + CM_PROJ
D_FF = 2816
LN_EPS = 1e-5
ALPHA = (2.0 * DEPTH) ** 0.25
BETA = (8.0 * DEPTH) ** -0.25

kernel_name = "hybrid_rwkv7_hgrn2_chunkgmlp_decoder_step"


def layer_norm(x, g, b, eps=LN_EPS):
    xf = x.astype(jnp.float32)
    mu = xf.mean(-1, keepdims=True)
    var = jnp.square(xf - mu).mean(-1, keepdims=True)
    return ((xf - mu) * lax.rsqrt(var + eps) * g + b).astype(x.dtype)


def swiglu(x, w_in, w_out):
    gate, up = jnp.split(x @ w_in, 2, axis=-1)
    return (jax.nn.silu(gate) * up) @ w_out


def rwkv7_mix(z, shift0, S0, mu, w0, w_w2, a0, a_w2, g_w2, k_k, k_a, r_k, gn_g, gn_b):
    B, T, _ = z.shape
    f32 = jnp.float32
    z_prev = jnp.concatenate([shift0[:, None, :].astype(z.dtype), z[:, :-1]], axis=1)
    zs = (z + (z_prev - z) * mu).astype(f32)
    r, k, v, xw, xa, xg = jnp.split(zs, RW_SPLITS, axis=-1)
    w_log = -jax.nn.softplus(-(w0 + jnp.tanh(xw) @ w_w2)) - 0.5
    decay = jnp.exp(-jnp.exp(w_log))
    a = jax.nn.sigmoid(a0 + xa @ a_w2)
    g = jax.nn.sigmoid(xg) @ g_w2
    kk = k * k_k
    k = k * (1.0 + (a - 1.0) * k_a)
    hs = lambda t: t.reshape(B, T, RW_HEADS, RW_DIM)
    r, decay, k, v, kk, a = map(hs, (r, decay, k, v, kk, a))
    kk = kk / jnp.maximum(jnp.sqrt(jnp.sum(kk * kk, axis=-1, keepdims=True)), 1e-12)

    def step(S, inp):
        r_t, w_t, k_t, v_t, kk_t, a_t = inp
        sa = jnp.einsum('bhvk,bhk->bhv', S, -kk_t)
        S = (S * w_t[:, :, None, :] + sa[..., None] * (kk_t * a_t)[:, :, None, :]
             + v_t[..., None] * k_t[:, :, None, :])
        return S, jnp.einsum('bhvk,bhk->bhv', S, r_t)

    tm = lambda t: jnp.moveaxis(t, 1, 0)
    S_T, o = lax.scan(step, S0.astype(f32), tuple(map(tm, (r, decay, k, v, kk, a))))
    o = jnp.moveaxis(o, 0, 1)
    o_mu = o.mean(-1, keepdims=True)
    o_var = jnp.square(o - o_mu).mean(-1, keepdims=True)
    o = ((o - o_mu) * lax.rsqrt(o_var + RW_GN_EPS)).reshape(B, T, RW_WIDTH) * gn_g + gn_b
    o = o + (jnp.sum(r * k * r_k, axis=-1, keepdims=True) * v).reshape(B, T, RW_WIDTH)
    o = o * g
    return o.astype(z.dtype), S_T.astype(S0.dtype), z[:, -1]


def gated_linear_recurrence(q, k, v, log_f, S0):
    B, T, H, K = q.shape
    C = math.gcd(T, HG_CHUNK)
    N = T // C
    mask = jnp.tril(jnp.ones((C, C), dtype=bool))[None, :, :, None, None]
    to_chunks = lambda t: jnp.moveaxis(t.reshape(B, N, C, H, t.shape[-1]), 1, 0)

    def step(S, inp):
        qc, kc, vc, gc = inp
        cum = jnp.cumsum(gc, axis=1)
        diff = cum[:, :, None] - cum[:, None, :]
        dec = jnp.where(mask, jnp.exp(jnp.minimum(diff, 0.0)), 0.0)
        A = jnp.einsum('bthk,bshk,btshk->bhts', qc, kc, dec)
        o = (jnp.einsum('bhts,bshv->bthv', A, vc)
             + jnp.einsum('bthk,bhkv->bthv', qc * jnp.exp(cum), S))
        total = cum[:, -1]
        S = (jnp.exp(total)[..., None] * S
             + jnp.einsum('bshk,bshv->bhkv', kc * jnp.exp(total[:, None] - cum), vc))
        return S, o

    S_T, o = lax.scan(step, S0, (to_chunks(q), to_chunks(k), to_chunks(v), to_chunks(log_f)))
    return jnp.moveaxis(o, 0, 1).reshape(B, T, H, v.shape[-1]), S_T


def hgrn2_mix(z, S0, lb, norm_g):
    B, T, _ = z.shape
    f32 = jnp.float32
    q, fz, i, og = jnp.split(z.astype(f32), 4, axis=-1)
    lbf = lb.astype(f32)
    f = lbf + (1.0 - lbf) * jax.nn.sigmoid(fz)
    log_f = jnp.log(jnp.maximum(f, F_MIN))
    k = 1.0 - f
    hs = lambda t: t.reshape(B, T, HG_HEADS, HG_DIM)
    o, S_T = gated_linear_recurrence(hs(jax.nn.silu(q)), hs(k), hs(i), hs(log_f), S0.astype(f32))
    o = o * lax.rsqrt(jnp.mean(o * o, axis=-1, keepdims=True) + RMS_EPS)
    o = o.reshape(B, T, HG_WIDTH) * norm_g * jax.nn.silu(og)
    return o.astype(z.dtype), S_T.astype(S0.dtype)


def chunk_mlp_mix(z, ws, bs, ln_g, ln_b):
    B, T, _ = z.shape
    u, v = jnp.split(z, 2, axis=-1)
    u = jax.nn.gelu(u, approximate=False)
    v = jax.nn.gelu(v, approximate=False).reshape(B, T, CM_HEADS, CM_DIM)
    v = layer_norm(v, ln_g.reshape(CM_HEADS, CM_DIM), ln_b.reshape(CM_HEADS, CM_DIM))
    Tp = -(-T // CM_CHUNK) * CM_CHUNK
    vp = jnp.pad(v, ((0, 0), (0, Tp - T), (0, 0), (0, 0))).reshape(B, Tp // CM_CHUNK, CM_CHUNK, CM_HEADS, CM_DIM)
    w_causal = ws * jnp.tril(jnp.ones((CM_CHUNK, CM_CHUNK), ws.dtype))
    mixed = jnp.einsum('hts,bnshd->bnthd', w_causal, vp) + bs.T[None, None, :, :, None]
    mixed = mixed.reshape(B, Tp, CM_HEADS, CM_DIM)[:, :T].reshape(B, T, CM_WIDTH)
    return u * mixed, v.reshape(B, T, CM_WIDTH)


def hgrn_lower_bounds(logits):
    s = jax.nn.softmax(logits.astype(jnp.float32), axis=0)
    return jnp.cumsum(s, axis=0) - s[0]


def run_trunk(x, rw_S0, rw_shift0, hg_S0, p):
    lb = hgrn_lower_bounds(p['hg_lb_logits'])
    rw_S, rw_sh, hg_S, cm_v = [], [], [], []
    for l in range(DEPTH):
        x = layer_norm(ALPHA * x + 0.5 * swiglu(x, p['ffn1_w_in'][l], p['ffn1_w_out'][l]),
                       p['ln1_g'][l], p['ln1_b'][l])
        z = x @ p['mix_w_in'][l]
        z_rw, z_hg, z_cm = jnp.split(z, [RW_PROJ, RW_PROJ + HG_PROJ], axis=-1)
        o_rw, s_rw, sh_rw = rwkv7_mix(z_rw, rw_shift0[l], rw_S0[l], p['rw_mu'][l], p['rw_w0'][l],
                                      p['rw_w_w2'][l], p['rw_a0'][l], p['rw_a_w2'][l], p['rw_g_w2'][l],
                                      p['rw_k_k'][l], p['rw_k_a'][l], p['rw_r_k'][l],
                                      p['rw_gn_g'][l], p['rw_gn_b'][l])
        o_hg, s_hg = hgrn2_mix(z_hg, hg_S0[l], lb[l], p['hg_norm_g'][l])
        o_cm, v_cm = chunk_mlp_mix(z_cm, p['cm_ws'][l], p['cm_bs'][l], p['cm_ln_g'][l], p['cm_ln_b'][l])
        mix = jnp.concatenate([o_rw, o_hg, o_cm], axis=-1) @ p['mix_w_out'][l]
        x = layer_norm(ALPHA * x + mix, p['ln2_g'][l], p['ln2_b'][l])
        x = layer_norm(ALPHA * x + 0.5 * swiglu(x, p['ffn2_w_in'][l], p['ffn2_w_out'][l]),
                       p['ln3_g'][l], p['ln3_b'][l])
        rw_S.append(s_rw)
        rw_sh.append(sh_rw)
        hg_S.append(s_hg)
        cm_v.append(v_cm)
    return x, jnp.stack(rw_S), jnp.stack(rw_sh), jnp.stack(hg_S), jnp.stack(cm_v)


def setup_inputs(seed: int = 0) -> dict:
    key = jax.random.key(seed)
    ks = iter(jax.random.split(key, 48))
    nrm = lambda shape, s: jax.random.normal(next(ks), shape, jnp.float32) * s
    L = DEPTH
    d = {}
    d['x_prompt'] = nrm((BATCH, SEQ, D_MODEL), 1.0)
    d['x_sample'] = nrm((DEC_BATCH, DEC_SEQ, D_MODEL), 1.0)
    d['state_rwkv'] = nrm((L, DEC_BATCH, RW_HEADS, RW_DIM, RW_DIM), 0.3)
    d['state_rwkv_shift'] = nrm((L, DEC_BATCH, RW_PROJ), 1.0)
    d['state_hgrn'] = nrm((L, DEC_BATCH, HG_HEADS, HG_DIM, HG_DIM), 0.5)
    d['ffn1_w_in'] = nrm((L, D_MODEL, 2 * D_FF), D_MODEL ** -0.5)
    d['ffn1_w_out'] = nrm((L, D_FF, D_MODEL), BETA * D_FF ** -0.5)
    d['ln1_g'] = 1.0 + nrm((L, D_MODEL), 0.02)
    d['ln1_b'] = nrm((L, D_MODEL), 0.02)
    d['mix_w_in'] = nrm((L, D_MODEL, IN_PROJ), D_MODEL ** -0.5)
    d['mix_w_out'] = nrm((L, MIX_WIDTH, D_MODEL), BETA * MIX_WIDTH ** -0.5)
    d['ln2_g'] = 1.0 + nrm((L, D_MODEL), 0.02)
    d['ln2_b'] = nrm((L, D_MODEL), 0.02)
    d['rw_mu'] = jax.random.uniform(next(ks), (L, RW_PROJ), jnp.float32)
    d['rw_w0'] = -1.0 + nrm((L, RW_WIDTH), 0.5)
    d['rw_w_w2'] = nrm((L, RW_DECAY_LORA, RW_WIDTH), 0.1 * RW_DECAY_LORA ** -0.5)
    d['rw_a0'] = nrm((L, RW_WIDTH), 0.1)
    d['rw_a_w2'] = nrm((L, RW_AAA_LORA, RW_WIDTH), 0.1 * RW_AAA_LORA ** -0.5)
    d['rw_g_w2'] = nrm((L, RW_GATE_LORA, RW_WIDTH), RW_GATE_LORA ** -0.5)
    d['rw_k_k'] = 0.85 + nrm((L, RW_WIDTH), 0.02)
    d['rw_k_a'] = 1.0 + nrm((L, RW_WIDTH), 0.02)
    d['rw_r_k'] = nrm((L, RW_HEADS, RW_DIM), 0.1)
    d['rw_gn_g'] = 1.0 + nrm((L, RW_WIDTH), 0.02)
    d['rw_gn_b'] = nrm((L, RW_WIDTH), 0.02)
    d['hg_lb_logits'] = nrm((L, HG_WIDTH), 0.5)
    d['hg_norm_g'] = 1.0 + nrm((L, HG_WIDTH), 0.02)
    d['cm_ws'] = nrm((L, CM_HEADS, CM_CHUNK, CM_CHUNK), CM_CHUNK ** -0.5)
    d['cm_bs'] = 1.0 + nrm((L, CM_HEADS, CM_CHUNK), 0.1)
    d['cm_ln_g'] = 1.0 + nrm((L, CM_WIDTH), 0.02)
    d['cm_ln_b'] = nrm((L, CM_WIDTH), 0.02)
    d['ffn2_w_in'] = nrm((L, D_MODEL, 2 * D_FF), D_MODEL ** -0.5)
    d['ffn2_w_out'] = nrm((L, D_FF, D_MODEL), BETA * D_FF ** -0.5)
    d['ln3_g'] = 1.0 + nrm((L, D_MODEL), 0.02)
    d['ln3_b'] = nrm((L, D_MODEL), 0.02)
    return d


def reference(x_prompt, x_sample, state_rwkv, state_rwkv_shift, state_hgrn,
              ffn1_w_in, ffn1_w_out, ln1_g, ln1_b, mix_w_in, mix_w_out, ln2_g, ln2_b,
              rw_mu, rw_w0, rw_w_w2, rw_a0, rw_a_w2, rw_g_w2, rw_k_k, rw_k_a, rw_r_k,
              rw_gn_g, rw_gn_b, hg_lb_logits, hg_norm_g, cm_ws, cm_bs, cm_ln_g, cm_ln_b,
              ffn2_w_in, ffn2_w_out, ln3_g, ln3_b):
    p = dict(ffn1_w_in=ffn1_w_in, ffn1_w_out=ffn1_w_out, ln1_g=ln1_g, ln1_b=ln1_b,
             mix_w_in=mix_w_in, mix_w_out=mix_w_out, ln2_g=ln2_g, ln2_b=ln2_b,
             rw_mu=rw_mu, rw_w0=rw_w0, rw_w_w2=rw_w_w2, rw_a0=rw_a0, rw_a_w2=rw_a_w2,
             rw_g_w2=rw_g_w2, rw_k_k=rw_k_k, rw_k_a=rw_k_a, rw_r_k=rw_r_k,
             rw_gn_g=rw_gn_g, rw_gn_b=rw_gn_b, hg_lb_logits=hg_lb_logits, hg_norm_g=hg_norm_g,
             cm_ws=cm_ws, cm_bs=cm_bs, cm_ln_g=cm_ln_g, cm_ln_b=cm_ln_b,
             ffn2_w_in=ffn2_w_in, ffn2_w_out=ffn2_w_out, ln3_g=ln3_g, ln3_b=ln3_b)
    B = x_prompt.shape[0]
    rw_S0 = jnp.zeros((DEPTH, B, RW_HEADS, RW_DIM, RW_DIM), state_rwkv.dtype)
    rw_sh0 = jnp.zeros((DEPTH, B, RW_PROJ), state_rwkv_shift.dtype)
    hg_S0 = jnp.zeros((DEPTH, B, HG_HEADS, HG_DIM, HG_DIM), state_hgrn.dtype)
    y_prompt, rw_S_p, rw_sh_p, hg_S_p, _ = run_trunk(x_prompt, rw_S0, rw_sh0, hg_S0, p)
    y_sample, rw_S_s, rw_sh_s, hg_S_s, cm_v_s = run_trunk(x_sample, state_rwkv, state_rwkv_shift,
                                                         state_hgrn, p)
    return (y_prompt, y_sample, rw_S_p, rw_sh_p, hg_S_p, rw_S_s, rw_sh_s, hg_S_s, cm_v_s)
```

```python
import functools

import jax
import jax.numpy as jnp
from jax import lax
from jax.experimental import pallas as pl
from jax.experimental.pallas import tpu as pltpu

F32 = jnp.float32
BF16 = jnp.bfloat16

D_MODEL = 1024
DEPTH = 2
RW_HEADS, RW_DIM = 4, 64
RW_WIDTH = RW_HEADS * RW_DIM
RW_DECAY_LORA, RW_AAA_LORA, RW_GATE_LORA = 64, 64, 128
RW_PROJ = 3 * RW_WIDTH + RW_DECAY_LORA + RW_AAA_LORA + RW_GATE_LORA
RW_GN_EPS = 64e-5
HG_HEADS, HG_DIM = 4, 128
HG_WIDTH = HG_HEADS * HG_DIM
HG_PROJ = 4 * HG_WIDTH
RMS_EPS = 1e-6
F_MIN = 1e-30
CM_HEADS, CM_DIM = 4, 64
CM_WIDTH = CM_HEADS * CM_DIM
CM_CHUNK = 128
CM_PROJ = 2 * CM_WIDTH
MIX_WIDTH = RW_WIDTH + HG_WIDTH + CM_WIDTH
IN_PROJ = RW_PROJ + HG_PROJ + CM_PROJ
D_FF = 2816
LN_EPS = 1e-5
ALPHA = (2.0 * DEPTH) ** 0.25

VMEM_LIMIT_BYTES = 56 * 1024 * 1024
DENSE_ROWS = 256
FF_CHUNK = 1408
RW_CHUNK = 64
HG_CHUNK = 64
HG_SUB = 16
EXP_CLAMP = 80.0


def _params(semantics):
    return pltpu.CompilerParams(dimension_semantics=semantics,
                                vmem_limit_bytes=VMEM_LIMIT_BYTES)


def _resident(shape):
    nd = len(shape)
    return pl.BlockSpec(shape, lambda *_: (0,) * nd, pipeline_mode=pl.Buffered(1))


def _dot(a, b, dims=((1,), (0,))):
    return lax.dot_general(a, b, (dims, ((), ())), precision=lax.Precision.HIGHEST,
                           preferred_element_type=F32)


def _bdot(a, b):
    return jnp.dot(a.astype(BF16), b.astype(BF16), preferred_element_type=F32)


_NT = ((1,), (1,))
_TN = ((0,), (0,))


def _layer_norm(x, g, b):
    mu = jnp.mean(x, axis=-1, keepdims=True)
    xc = x - mu
    var = jnp.mean(xc * xc, axis=-1, keepdims=True)
    return xc * lax.rsqrt(var + LN_EPS) * g + b


def _swiglu(xb, w_in_ref, w_out_ref):
    acc = None
    for lo in range(0, D_FF, FF_CHUNK):
        gate = jnp.dot(xb, w_in_ref[:, lo:lo + FF_CHUNK], preferred_element_type=F32)
        up = jnp.dot(xb, w_in_ref[:, D_FF + lo:D_FF + lo + FF_CHUNK], preferred_element_type=F32)
        h = (gate * jax.nn.sigmoid(gate) * up).astype(BF16)
        part = jnp.dot(h, w_out_ref[lo:lo + FF_CHUNK, :], preferred_element_type=F32)
        acc = part if acc is None else acc + part
    return acc


def _dense_in_kernel(x_ref, w_in_ref, w_out_ref, g_ref, b_ref, w_mix_ref, x1_ref, z_ref):
    x = x_ref[...]
    y = _layer_norm(ALPHA * x + 0.5 * _swiglu(x.astype(BF16), w_in_ref, w_out_ref),
                    g_ref[...], b_ref[...])
    x1_ref[...] = y
    z_ref[...] = jnp.dot(y.astype(BF16), w_mix_ref[...], preferred_element_type=F32)


def _dense_in(x, w_in, w_out, g, b, w_mix):
    n = x.shape[0]
    rows = min(DENSE_ROWS, n)
    row_block = lambda width: pl.BlockSpec((rows, width), lambda i: (i, 0))
    return pl.pallas_call(
        _dense_in_kernel,
        grid=(n // rows,),
        in_specs=[row_block(D_MODEL), _resident(w_in.shape), _resident(w_out.shape),
                  _resident(g.shape), _resident(b.shape), _resident(w_mix.shape)],
        out_specs=[row_block(D_MODEL), row_block(IN_PROJ)],
        out_shape=[jax.ShapeDtypeStruct((n, D_MODEL), F32),
                   jax.ShapeDtypeStruct((n, IN_PROJ), F32)],
        compiler_params=_params(("parallel",)),
        name="dense_in",
    )(x, w_in, w_out, g, b, w_mix)


def _dense_out_kernel(x_ref, orw_ref, ohg_ref, ocm_ref, wrw_ref, whg_ref, wcm_ref, g2_ref, b2_ref,
                      w_in_ref, w_out_ref, g3_ref, b3_ref, y_ref):
    mix = (jnp.dot(orw_ref[...], wrw_ref[...], preferred_element_type=F32)
           + jnp.dot(ohg_ref[...], whg_ref[...], preferred_element_type=F32)
           + jnp.dot(ocm_ref[...], wcm_ref[...], preferred_element_type=F32))
    x2 = _layer_norm(ALPHA * x_ref[...] + mix, g2_ref[...], b2_ref[...])
    y_ref[...] = _layer_norm(ALPHA * x2 + 0.5 * _swiglu(x2.astype(BF16), w_in_ref, w_out_ref),
                             g3_ref[...], b3_ref[...])


def _dense_out(x, o_rw, o_hg, o_cm, w_rw, w_hg, w_cm, g2, b2, w_in, w_out, g3, b3):
    n = x.shape[0]
    rows = min(DENSE_ROWS, n)
    row_block = lambda width: pl.BlockSpec((rows, width), lambda i: (i, 0))
    weights = (w_rw, w_hg, w_cm, g2, b2, w_in, w_out, g3, b3)
    return pl.pallas_call(
        _dense_out_kernel,
        grid=(n // rows,),
        in_specs=[row_block(D_MODEL), row_block(RW_WIDTH), row_block(HG_WIDTH), row_block(CM_WIDTH)]
                 + [_resident(w.shape) for w in weights],
        out_specs=row_block(D_MODEL),
        out_shape=jax.ShapeDtypeStruct((n, D_MODEL), F32),
        compiler_params=_params(("parallel",)),
        name="dense_out",
    )(x, o_rw, o_hg, o_cm, *weights)


def _head_stack(x, head_masks):
    return jnp.concatenate([x * m for m in head_masks], axis=0)


def _rwkv_kernel(*refs, chunk, t_valid, has_state):
    if has_state:
        (zr_ref, zk_ref, zv_ref, zx_ref, shift_ref, s0_ref, mu_ref, w0_ref, ww2_ref, a0_ref, aw2_ref,
         gw2_ref, kk_ref, ka_ref, rk_ref, gng_ref, gnb_ref, o_ref, s_out_ref, s_scr, prev_scr) = refs
    else:
        (zr_ref, zk_ref, zv_ref, zx_ref, mu_ref, w0_ref, ww2_ref, a0_ref, aw2_ref,
         gw2_ref, kk_ref, ka_ref, rk_ref, gng_ref, gnb_ref, o_ref, s_out_ref, s_scr, prev_scr) = refs
    C, N, H, W = chunk, RW_DIM, RW_HEADS, RW_WIDTH
    step = pl.program_id(1)

    @pl.when(step == 0)
    def _():
        if has_state:
            s_scr[...] = jnp.concatenate([s0_ref[h] for h in range(H)], axis=1)
            prev_scr[...] = shift_ref[...]
        else:
            s_scr[...] = jnp.zeros_like(s_scr)
            prev_scr[...] = jnp.zeros_like(prev_scr)

    row = lax.broadcasted_iota(jnp.int32, (C, 1), 0)

    def load(ref, part):
        z = ref[...]
        if t_valid < C:
            z = jnp.where(row < t_valid, z, 0.0)
        prev = jnp.where(row == 0, prev_scr[:, part * W:(part + 1) * W], pltpu.roll(z, 1, axis=0))
        prev_scr[:, part * W:(part + 1) * W] = z[t_valid - 1:t_valid, :] if t_valid < C else z[C - 1:C, :]
        return z + (prev - z) * mu_ref[:, part * W:(part + 1) * W]

    r = load(zr_ref, 0)
    k = load(zk_ref, 1)
    v = load(zv_ref, 2)
    x4 = load(zx_ref, 3)

    w_pre = w0_ref[...] + _dot(jnp.tanh(x4), ww2_ref[...])
    nw = -w_pre
    softplus = jnp.maximum(nw, 0.0) + jnp.log(1.0 + jnp.exp(-jnp.abs(nw)))
    lw = -jnp.exp(-softplus - 0.5)
    a = jax.nn.sigmoid(a0_ref[...] + _dot(x4, aw2_ref[...]))
    gate = _dot(jax.nn.sigmoid(x4), gw2_ref[...])

    lane = lax.broadcasted_iota(jnp.int32, (1, W), 1)
    head_masks = [((lane >= h * N) & (lane < (h + 1) * N)).astype(F32) for h in range(H)]
    hr = lax.broadcasted_iota(jnp.int32, (W, W), 0)
    hc = lax.broadcasted_iota(jnp.int32, (W, W), 1)
    head_ones = ((hr // N) == (hc // N)).astype(F32)

    kk = k * kk_ref[...]
    k = k * (1.0 + (a - 1.0) * ka_ref[...])
    kk = kk / jnp.maximum(jnp.sqrt(_dot(kk * kk, head_ones)), 1e-12)
    bonus = _dot(r * k * rk_ref[...], head_ones) * v

    if t_valid < C:
        valid = row < t_valid
        lw = jnp.where(valid, lw, 0.0)
        kk = jnp.where(valid, kk, 0.0)
        k = jnp.where(valid, k, 0.0)
        v = jnp.where(valid, v, 0.0)

    tr = lax.broadcasted_iota(jnp.int32, (C, C), 0)
    tc = lax.broadcasted_iota(jnp.int32, (C, C), 1)
    cum = _dot((tc <= tr).astype(F32), lw)
    g_in = jnp.exp(cum)
    g_ex = jnp.exp(cum - lw)
    g_inv = jnp.exp(-cum)
    g_last = g_in[C - 1:C, :]

    a_s = _head_stack(-kk * g_ex, head_masks)
    b_s = _head_stack(kk * a * g_inv, head_masks)
    k_s = _head_stack(k * g_inv, head_masks)
    r_s = _head_stack(r * g_in, head_masks)
    v_s = jnp.concatenate([v[:, h * N:(h + 1) * N] for h in range(H)], axis=0)

    HC = H * C
    sr = lax.broadcasted_iota(jnp.int32, (HC, HC), 0)
    sc = lax.broadcasted_iota(jnp.int32, (HC, HC), 1)
    same_head = (sr // C) == (sc // C)
    strict = same_head & ((sc % C) < (sr % C))
    incl = same_head & ((sc % C) <= (sr % C))
    m_ab = jnp.where(strict, _dot(a_s, b_s, _NT), 0.0)
    m_ak = jnp.where(strict, _dot(a_s, k_s, _NT), 0.0)
    p_rb = jnp.where(incl, _dot(r_s, b_s, _NT), 0.0)
    p_rk = jnp.where(incl, _dot(r_s, k_s, _NT), 0.0)

    eye = (sr == sc).astype(F32)
    t_inv = eye + m_ab
    power = m_ab
    span = 2
    while span < C:
        power = _dot(power, power)
        t_inv = t_inv + _dot(t_inv, power)
        span *= 2
    w_m = _dot(t_inv, a_s)
    u_m = _dot(t_inv, _dot(m_ak, v_s))

    s = s_scr[...]
    c_m = _dot(w_m, s, _NT) + u_m
    o_s = _dot(r_s, s, _NT) + _dot(p_rb, c_m) + _dot(p_rk, v_s)
    s_new = (s + _dot(c_m, b_s, _TN) + _dot(v_s, k_s, _TN)) * g_last
    s_scr[...] = s_new

    mu_o = jnp.mean(o_s, axis=-1, keepdims=True)
    oc = o_s - mu_o
    var_o = jnp.mean(oc * oc, axis=-1, keepdims=True)
    on = oc * lax.rsqrt(var_o + RW_GN_EPS)
    o = jnp.concatenate([on[h * C:(h + 1) * C, :] for h in range(H)], axis=1)
    o = (o * gng_ref[...] + gnb_ref[...] + bonus) * gate
    o_ref[...] = o.astype(o_ref.dtype)

    @pl.when(step == pl.num_programs(1) - 1)
    def _():
        for h in range(H):
            s_out_ref[h] = s_new[:, h * N:(h + 1) * N]


def _rwkv(z, shift0, s0, p, t_valid, chunk):
    B, T, _ = z.shape
    W = RW_WIDTH
    has_state = s0 is not None
    zcol = lambda j: pl.BlockSpec((None, chunk, W), lambda b, t, j=j: (b, t, j))
    vec = lambda a: _resident(a.shape)
    weights = (p["mu"], p["w0"], p["w_w2"], p["a0"], p["a_w2"], p["g_w2"], p["k_k"], p["k_a"],
               p["r_k"], p["gn_g"], p["gn_b"])
    args, specs = [z, z, z, z], [zcol(0), zcol(1), zcol(2), zcol(3)]
    if has_state:
        args += [shift0, s0]
        specs += [pl.BlockSpec((None, 1, RW_PROJ), lambda b, t: (b, 0, 0)),
                  pl.BlockSpec((None, RW_HEADS, RW_DIM, RW_DIM), lambda b, t: (b, 0, 0, 0))]
    args += list(weights)
    specs += [vec(w) for w in weights]
    kern = functools.partial(_rwkv_kernel, chunk=chunk, t_valid=t_valid, has_state=has_state)
    return pl.pallas_call(
        kern,
        grid=(B, T // chunk),
        in_specs=specs,
        out_specs=[pl.BlockSpec((None, chunk, W), lambda b, t: (b, t, 0)),
                   pl.BlockSpec((None, RW_HEADS, RW_DIM, RW_DIM), lambda b, t: (b, 0, 0, 0))],
        out_shape=[jax.ShapeDtypeStruct((B, T, W), BF16),
                   jax.ShapeDtypeStruct((B, RW_HEADS, RW_DIM, RW_DIM), F32)],
        scratch_shapes=[pltpu.VMEM((RW_DIM, W), F32), pltpu.VMEM((1, RW_PROJ), F32)],
        compiler_params=_params(("parallel", "arbitrary")),
        name="rwkv7",
    )(*args)


def _hgrn_kernel(*refs, chunk, t_valid, has_state):
    if has_state:
        zq_ref, zf_ref, zi_ref, zg_ref, s0_ref, lb_ref, ng_ref, o_ref, s_out_ref, s_scr = refs
    else:
        zq_ref, zf_ref, zi_ref, zg_ref, lb_ref, ng_ref, o_ref, s_out_ref, s_scr = refs
    C, N, H = chunk, HG_DIM, HG_HEADS
    sub = min(HG_SUB, C)
    step = pl.program_id(1)

    @pl.when(step == 0)
    def _():
        if has_state:
            s_scr[...] = s0_ref[...]
        else:
            s_scr[...] = jnp.zeros_like(s_scr)

    zq = zq_ref[...]
    q = zq * jax.nn.sigmoid(zq)
    lb = lb_ref[...]
    f = lb + (1.0 - lb) * jax.nn.sigmoid(zf_ref[...])
    log_f = jnp.log(jnp.maximum(f, F_MIN))
    k = 1.0 - f
    v = zi_ref[...]
    zg = zg_ref[...]
    out_gate = ng_ref[...] * (zg * jax.nn.sigmoid(zg))
    if t_valid < C:
        valid = lax.broadcasted_iota(jnp.int32, (C, 1), 0) < t_valid
        log_f = jnp.where(valid, log_f, 0.0)
        k = jnp.where(valid, k, 0.0)

    tr = lax.broadcasted_iota(jnp.int32, (C, C), 0)
    tc = lax.broadcasted_iota(jnp.int32, (C, C), 1)
    causal = tc <= tr
    cum = _dot(causal.astype(F32), log_f)
    q_dec = q * jnp.exp(cum)
    total = cum[C - 1:C, :]
    k_dec = k * jnp.exp(total - cum)
    decay_all = jnp.exp(total)
    er = lax.broadcasted_iota(jnp.int32, (N, N), 0)
    ec = lax.broadcasted_iota(jnp.int32, (N, N), 1)
    eye = er == ec

    outs = []
    for h in range(H):
        sl = slice(h * N, (h + 1) * N)
        cum_h, q_h, k_h, v_h = cum[:, sl], q[:, sl], k[:, sl], v[:, sl]
        rows = []
        for i in range(C // sub):
            blk = slice(i * sub, (i + 1) * sub)
            ref_row = cum_h[i * sub:i * sub + 1, :]
            q_hat = q_h[blk] * jnp.exp(cum_h[blk] - ref_row)
            k_hat = k_h * jnp.exp(jnp.minimum(ref_row - cum_h, EXP_CLAMP))
            rows.append(_dot(q_hat, k_hat, _NT))
        attn = jnp.where(causal, jnp.concatenate(rows, axis=0) if len(rows) > 1 else rows[0], 0.0)
        s_h = s_scr[h]
        o_h = _dot(attn, v_h) + _dot(q_dec[:, sl], s_h)
        decay_diag = jnp.where(eye, decay_all[:, sl], 0.0)
        s_new = _dot(k_dec[:, sl], v_h, _TN) + _dot(decay_diag, s_h)
        s_scr[h] = s_new
        o_h = o_h * lax.rsqrt(jnp.mean(o_h * o_h, axis=-1, keepdims=True) + RMS_EPS)
        outs.append(o_h)
    o = jnp.concatenate(outs, axis=1) * out_gate
    o_ref[...] = o.astype(o_ref.dtype)

    @pl.when(step == pl.num_programs(1) - 1)
    def _():
        s_out_ref[...] = s_scr[...]


def _hgrn(z, s0, lb, norm_g, t_valid, chunk):
    B, T, _ = z.shape
    W = HG_WIDTH
    first = RW_PROJ // W
    has_state = s0 is not None
    zcol = lambda j: pl.BlockSpec((None, chunk, W), lambda b, t, j=j: (b, t, first + j))
    state_spec = pl.BlockSpec((None, HG_HEADS, HG_DIM, HG_DIM), lambda b, t: (b, 0, 0, 0))
    args, specs = [z, z, z, z], [zcol(0), zcol(1), zcol(2), zcol(3)]
    if has_state:
        args.append(s0)
        specs.append(state_spec)
    args += [lb, norm_g]
    specs += [_resident(lb.shape), _resident(norm_g.shape)]
    kern = functools.partial(_hgrn_kernel, chunk=chunk, t_valid=t_valid, has_state=has_state)
    return pl.pallas_call(
        kern,
        grid=(B, T // chunk),
        in_specs=specs,
        out_specs=[pl.BlockSpec((None, chunk, W), lambda b, t: (b, t, 0)), state_spec],
        out_shape=[jax.ShapeDtypeStruct((B, T, W), BF16),
                   jax.ShapeDtypeStruct((B, HG_HEADS, HG_DIM, HG_DIM), F32)],
        scratch_shapes=[pltpu.VMEM((HG_HEADS, HG_DIM, HG_DIM), F32)],
        compiler_params=_params(("parallel", "arbitrary")),
        name="hgrn2",
    )(*args)


def _gelu(x):
    return 0.5 * x * (1.0 + lax.erf(x * (2.0 ** -0.5)))


def _cm_kernel(zu_ref, zv_ref, ws_ref, bs_ref, g_ref, b_ref, o_ref, v_ref):
    H, N, W = CM_HEADS, CM_DIM, CM_WIDTH
    C = zu_ref.shape[0]
    u = _gelu(zu_ref[...])
    v = _gelu(zv_ref[...])
    hr = lax.broadcasted_iota(jnp.int32, (W, W), 0)
    hc = lax.broadcasted_iota(jnp.int32, (W, W), 1)
    head_mean = jnp.where((hr // N) == (hc // N), 1.0 / N, 0.0).astype(F32)
    mu = _dot(v, head_mean)
    vc = v - mu
    var = _dot(vc * vc, head_mean)
    vn = vc * lax.rsqrt(var + LN_EPS) * g_ref[...] + b_ref[...]
    v_ref[...] = vn
    tr = lax.broadcasted_iota(jnp.int32, (C, C), 0)
    tc = lax.broadcasted_iota(jnp.int32, (C, C), 1)
    lane = lax.broadcasted_iota(jnp.int32, (1, W), 1)
    mixed = bs_ref[...]
    for h in range(H):
        w_causal = jnp.where(tc <= tr, ws_ref[h], 0.0)
        v_head = jnp.where((lane >= h * N) & (lane < (h + 1) * N), vn, 0.0)
        mixed = mixed + _bdot(w_causal, v_head)
    o_ref[...] = (u * mixed).astype(o_ref.dtype)


def _chunk_mlp(z, ws, bs_wide, ln_g, ln_b, chunk):
    B, T, _ = z.shape
    W = CM_WIDTH
    first = (RW_PROJ + HG_PROJ) // W
    zcol = lambda j: pl.BlockSpec((None, chunk, W), lambda b, t, j=j: (b, t, first + j))
    out_block = pl.BlockSpec((None, chunk, W), lambda b, t: (b, t, 0))
    return pl.pallas_call(
        _cm_kernel,
        grid=(B, T // chunk),
        in_specs=[zcol(0), zcol(1), _resident(ws.shape), _resident(bs_wide.shape),
                  _resident(ln_g.shape), _resident(ln_b.shape)],
        out_specs=[out_block, out_block],
        out_shape=[jax.ShapeDtypeStruct((B, T, W), BF16), jax.ShapeDtypeStruct((B, T, W), F32)],
        compiler_params=_params(("parallel", "parallel")),
        name="chunk_gmlp",
    )(z, z, ws, bs_wide, ln_g, ln_b)


def _pad_rows(n, mult):
    return -(-n // mult) * mult


def _layer_weights(p, l, lb):
    row = lambda a: a.reshape(1, -1)
    lora_pad = lambda w, start: jnp.zeros((RW_WIDTH, RW_WIDTH), F32).at[start:start + w.shape[0]].set(w)
    mix_out = p["mix_w_out"][l].astype(BF16)
    return dict(
        ffn1_in=p["ffn1_w_in"][l].astype(BF16), ffn1_out=p["ffn1_w_out"][l].astype(BF16),
        ln1_g=row(p["ln1_g"][l]), ln1_b=row(p["ln1_b"][l]),
        mix_in=p["mix_w_in"][l].astype(BF16),
        mix_out_rw=mix_out[:RW_WIDTH], mix_out_hg=mix_out[RW_WIDTH:RW_WIDTH + HG_WIDTH],
        mix_out_cm=mix_out[RW_WIDTH + HG_WIDTH:],
        ln2_g=row(p["ln2_g"][l]), ln2_b=row(p["ln2_b"][l]),
        ffn2_in=p["ffn2_w_in"][l].astype(BF16), ffn2_out=p["ffn2_w_out"][l].astype(BF16),
        ln3_g=row(p["ln3_g"][l]), ln3_b=row(p["ln3_b"][l]),
        rw=dict(mu=row(p["rw_mu"][l]), w0=row(p["rw_w0"][l]),
                w_w2=lora_pad(p["rw_w_w2"][l], 0), a0=row(p["rw_a0"][l]),
                a_w2=lora_pad(p["rw_a_w2"][l], RW_DECAY_LORA),
                g_w2=lora_pad(p["rw_g_w2"][l], RW_DECAY_LORA + RW_AAA_LORA),
                k_k=row(p["rw_k_k"][l]), k_a=row(p["rw_k_a"][l]), r_k=row(p["rw_r_k"][l]),
                gn_g=row(p["rw_gn_g"][l]), gn_b=row(p["rw_gn_b"][l])),
        hg_lb=row(lb[l]), hg_norm_g=row(p["hg_norm_g"][l]),
        cm_ws=p["cm_ws"][l],
        cm_bs=jnp.repeat(p["cm_bs"][l].T, CM_DIM, axis=1),
        cm_ln_g=row(p["cm_ln_g"][l]), cm_ln_b=row(p["cm_ln_b"][l]),
    )


def _run_trunk(x, rw_s0, rw_shift0, hg_s0, weights, t_pad, rw_chunk, hg_chunk, cm_chunk):
    B, T, _ = x.shape
    n = B * T
    xf = x.reshape(n, D_MODEL)
    rw_states, rw_shifts, hg_states, cm_vs = [], [], [], []
    for l in range(DEPTH):
        w = weights[l]
        x1, z = _dense_in(xf, w["ffn1_in"], w["ffn1_out"], w["ln1_g"], w["ln1_b"], w["mix_in"])
        z = z.reshape(B, T, IN_PROJ)
        rw_shifts.append(z[:, T - 1, :RW_PROJ])
        zp = z if t_pad == T else jnp.pad(z, ((0, 0), (0, t_pad - T), (0, 0)))
        has_state = rw_s0 is not None
        o_rw, s_rw = _rwkv(zp, rw_shift0[l][:, None, :] if has_state else None,
                           rw_s0[l] if has_state else None, w["rw"], min(T, rw_chunk), rw_chunk)
        o_hg, s_hg = _hgrn(zp, hg_s0[l] if has_state else None, w["hg_lb"], w["hg_norm_g"],
                           min(T, hg_chunk), hg_chunk)
        o_cm, v_cm = _chunk_mlp(zp, w["cm_ws"][:, :cm_chunk, :cm_chunk], w["cm_bs"][:cm_chunk],
                                w["cm_ln_g"], w["cm_ln_b"], cm_chunk)
        trim = lambda o: o[:, :T].reshape(n, o.shape[-1])
        xf = _dense_out(x1, trim(o_rw), trim(o_hg), trim(o_cm), w["mix_out_rw"], w["mix_out_hg"],
                        w["mix_out_cm"], w["ln2_g"], w["ln2_b"], w["ffn2_in"], w["ffn2_out"],
                        w["ln3_g"], w["ln3_b"])
        rw_states.append(s_rw)
        hg_states.append(s_hg)
        cm_vs.append(v_cm[:, :T])
    return (xf.reshape(B, T, D_MODEL), jnp.stack(rw_states), jnp.stack(rw_shifts),
            jnp.stack(hg_states), jnp.stack(cm_vs))


def kernel(x_prompt, x_sample, state_rwkv, state_rwkv_shift, state_hgrn, ffn1_w_in, ffn1_w_out, ln1_g, ln1_b, mix_w_in, mix_w_out, ln2_g, ln2_b, rw_mu, rw_w0, rw_w_w2, rw_a0, rw_a_w2, rw_g_w2, rw_k_k, rw_k_a, rw_r_k, rw_gn_g, rw_gn_b, hg_lb_logits, hg_norm_g, cm_ws, cm_bs, cm_ln_g, cm_ln_b, ffn2_w_in, ffn2_w_out, ln3_g, ln3_b):
    p = dict(ffn1_w_in=ffn1_w_in, ffn1_w_out=ffn1_w_out, ln1_g=ln1_g, ln1_b=ln1_b,
             mix_w_in=mix_w_in, mix_w_out=mix_w_out, ln2_g=ln2_g, ln2_b=ln2_b,
             rw_mu=rw_mu, rw_w0=rw_w0, rw_w_w2=rw_w_w2, rw_a0=rw_a0, rw_a_w2=rw_a_w2,
             rw_g_w2=rw_g_w2, rw_k_k=rw_k_k, rw_k_a=rw_k_a, rw_r_k=rw_r_k,
             rw_gn_g=rw_gn_g, rw_gn_b=rw_gn_b, hg_norm_g=hg_norm_g,
             cm_ws=cm_ws, cm_bs=cm_bs, cm_ln_g=cm_ln_g, cm_ln_b=cm_ln_b,
             ffn2_w_in=ffn2_w_in, ffn2_w_out=ffn2_w_out, ln3_g=ln3_g, ln3_b=ln3_b)
    s = jax.nn.softmax(hg_lb_logits.astype(F32), axis=0)
    lb = jnp.cumsum(s, axis=0) - s[0]
    weights = [_layer_weights(p, l, lb) for l in range(DEPTH)]

    t_prompt = x_prompt.shape[1]
    y_p, rw_s_p, rw_sh_p, hg_s_p, _ = _run_trunk(
        x_prompt, None, None, None, weights, t_prompt, RW_CHUNK, HG_CHUNK, CM_CHUNK)
    t_sample = x_sample.shape[1]
    t_pad = _pad_rows(t_sample, 8)
    y_s, rw_s_s, rw_sh_s, hg_s_s, cm_v_s = _run_trunk(
        x_sample, state_rwkv, state_rwkv_shift, state_hgrn, weights, t_pad, t_pad, t_pad, t_pad)
    return (y_p, y_s, rw_s_p, rw_sh_p, hg_s_p, rw_s_s, rw_sh_s, hg_s_s, cm_v_s)
```

```python
import functools

import jax
import jax.numpy as jnp
from jax import lax
from jax.experimental import pallas as pl
from jax.experimental.pallas import tpu as pltpu

F32 = jnp.float32
BF16 = jnp.bfloat16

D_MODEL = 1024
DEPTH = 2
RW_HEADS, RW_DIM = 4, 64
RW_WIDTH = RW_HEADS * RW_DIM
RW_DECAY_LORA, RW_AAA_LORA, RW_GATE_LORA = 64, 64, 128
RW_PROJ = 3 * RW_WIDTH + RW_DECAY_LORA + RW_AAA_LORA + RW_GATE_LORA
RW_GN_EPS = 64e-5
HG_HEADS, HG_DIM = 4, 128
HG_WIDTH = HG_HEADS * HG_DIM
HG_PROJ = 4 * HG_WIDTH
RMS_EPS = 1e-6
F_MIN = 1e-30
CM_HEADS, CM_DIM = 4, 64
CM_WIDTH = CM_HEADS * CM_DIM
CM_CHUNK = 128
CM_PROJ = 2 * CM_WIDTH
MIX_WIDTH = RW_WIDTH + HG_WIDTH + CM_WIDTH
IN_PROJ = RW_PROJ + HG_PROJ + CM_PROJ
D_FF = 2816
LN_EPS = 1e-5
ALPHA = (2.0 * DEPTH) ** 0.25

VMEM_LIMIT_BYTES = 56 * 1024 * 1024
DENSE_ROWS = 256
FF_CHUNK = 1408
RW_CHUNK = 64
HG_CHUNK = 64
HG_SUB = 16
EXP_CLAMP = 80.0


def _params(semantics):
    return pltpu.CompilerParams(dimension_semantics=semantics,
                                vmem_limit_bytes=VMEM_LIMIT_BYTES)


def _resident(shape):
    nd = len(shape)
    return pl.BlockSpec(shape, lambda *_: (0,) * nd, pipeline_mode=pl.Buffered(1))


def _dot(a, b, dims=((1,), (0,))):
    return lax.dot_general(a.astype(BF16), b.astype(BF16), (dims, ((), ())),
                           preferred_element_type=F32)


def _mask_dot(mask, x):
    m = mask.astype(BF16)
    hi = x.astype(BF16)
    r1 = x - hi.astype(F32)
    mid = r1.astype(BF16)
    lo = (r1 - mid.astype(F32)).astype(BF16)
    dot = lambda t: jnp.dot(m, t, preferred_element_type=F32)
    return dot(hi) + (dot(mid) + dot(lo))


_NT = ((1,), (1,))
_TN = ((0,), (0,))


def _layer_norm(x, g, b):
    mu = jnp.mean(x, axis=-1, keepdims=True)
    xc = x - mu
    var = jnp.mean(xc * xc, axis=-1, keepdims=True)
    return xc * lax.rsqrt(var + LN_EPS) * g + b


def _swiglu(xb, w_in_ref, w_out_ref):
    acc = None
    for lo in range(0, D_FF, FF_CHUNK):
        gate = jnp.dot(xb, w_in_ref[:, lo:lo + FF_CHUNK], preferred_element_type=F32)
        up = jnp.dot(xb, w_in_ref[:, D_FF + lo:D_FF + lo + FF_CHUNK], preferred_element_type=F32)
        h = (gate * jax.nn.sigmoid(gate) * up).astype(BF16)
        part = jnp.dot(h, w_out_ref[lo:lo + FF_CHUNK, :], preferred_element_type=F32)
        acc = part if acc is None else acc + part
    return acc


def _dense_in_kernel(x_ref, w_in_ref, w_out_ref, g_ref, b_ref, w_mix_ref, x1_ref, z_ref):
    x = x_ref[...]
    y = _layer_norm(ALPHA * x + 0.5 * _swiglu(x.astype(BF16), w_in_ref, w_out_ref),
                    g_ref[...], b_ref[...])
    x1_ref[...] = y
    z_ref[...] = jnp.dot(y.astype(BF16), w_mix_ref[...], preferred_element_type=F32)


def _dense_in(x, w_in, w_out, g, b, w_mix):
    n = x.shape[0]
    rows = min(DENSE_ROWS, n)
    row_block = lambda width: pl.BlockSpec((rows, width), lambda i: (i, 0))
    return pl.pallas_call(
        _dense_in_kernel,
        grid=(n // rows,),
        in_specs=[row_block(D_MODEL), _resident(w_in.shape), _resident(w_out.shape),
                  _resident(g.shape), _resident(b.shape), _resident(w_mix.shape)],
        out_specs=[row_block(D_MODEL), row_block(IN_PROJ)],
        out_shape=[jax.ShapeDtypeStruct((n, D_MODEL), F32),
                   jax.ShapeDtypeStruct((n, IN_PROJ), F32)],
        compiler_params=_params(("parallel",)),
        name="dense_in",
    )(x, w_in, w_out, g, b, w_mix)


def _dense_out_kernel(x_ref, orw_ref, ohg_ref, ocm_ref, wrw_ref, whg_ref, wcm_ref, g2_ref, b2_ref,
                      w_in_ref, w_out_ref, g3_ref, b3_ref, y_ref):
    mix = (jnp.dot(orw_ref[...], wrw_ref[...], preferred_element_type=F32)
           + jnp.dot(ohg_ref[...], whg_ref[...], preferred_element_type=F32)
           + jnp.dot(ocm_ref[...], wcm_ref[...], preferred_element_type=F32))
    x2 = _layer_norm(ALPHA * x_ref[...] + mix, g2_ref[...], b2_ref[...])
    y_ref[...] = _layer_norm(ALPHA * x2 + 0.5 * _swiglu(x2.astype(BF16), w_in_ref, w_out_ref),
                             g3_ref[...], b3_ref[...])


def _dense_out(x, o_rw, o_hg, o_cm, w_rw, w_hg, w_cm, g2, b2, w_in, w_out, g3, b3):
    n = x.shape[0]
    rows = min(DENSE_ROWS, n)
    row_block = lambda width: pl.BlockSpec((rows, width), lambda i: (i, 0))
    weights = (w_rw, w_hg, w_cm, g2, b2, w_in, w_out, g3, b3)
    return pl.pallas_call(
        _dense_out_kernel,
        grid=(n // rows,),
        in_specs=[row_block(D_MODEL), row_block(RW_WIDTH), row_block(HG_WIDTH), row_block(CM_WIDTH)]
                 + [_resident(w.shape) for w in weights],
        out_specs=row_block(D_MODEL),
        out_shape=jax.ShapeDtypeStruct((n, D_MODEL), F32),
        compiler_params=_params(("parallel",)),
        name="dense_out",
    )(x, o_rw, o_hg, o_cm, *weights)


def _head_stack(x, head_masks):
    return jnp.concatenate([x * m for m in head_masks], axis=0)


def _rwkv_kernel(*refs, chunk, t_valid, has_state):
    if has_state:
        (zr_ref, zk_ref, zv_ref, zx_ref, shift_ref, s0_ref, mu_ref, w0_ref, ww2_ref, a0_ref, aw2_ref,
         gw2_ref, kk_ref, ka_ref, rk_ref, gng_ref, gnb_ref, o_ref, s_out_ref, s_scr, prev_scr) = refs
    else:
        (zr_ref, zk_ref, zv_ref, zx_ref, mu_ref, w0_ref, ww2_ref, a0_ref, aw2_ref,
         gw2_ref, kk_ref, ka_ref, rk_ref, gng_ref, gnb_ref, o_ref, s_out_ref, s_scr, prev_scr) = refs
    C, N, H, W = chunk, RW_DIM, RW_HEADS, RW_WIDTH
    step = pl.program_id(1)

    @pl.when(step == 0)
    def _():
        if has_state:
            s_scr[...] = jnp.concatenate([s0_ref[h] for h in range(H)], axis=1)
            prev_scr[...] = shift_ref[...]
        else:
            s_scr[...] = jnp.zeros_like(s_scr)
            prev_scr[...] = jnp.zeros_like(prev_scr)

    row = lax.broadcasted_iota(jnp.int32, (C, 1), 0)

    def load(ref, part):
        z = ref[...]
        if t_valid < C:
            z = jnp.where(row < t_valid, z, 0.0)
        prev = jnp.where(row == 0, prev_scr[:, part * W:(part + 1) * W], pltpu.roll(z, 1, axis=0))
        prev_scr[:, part * W:(part + 1) * W] = z[t_valid - 1:t_valid, :] if t_valid < C else z[C - 1:C, :]
        return z + (prev - z) * mu_ref[:, part * W:(part + 1) * W]

    r = load(zr_ref, 0)
    k = load(zk_ref, 1)
    v = load(zv_ref, 2)
    x4 = load(zx_ref, 3)

    w_pre = w0_ref[...] + _dot(jnp.tanh(x4), ww2_ref[...])
    nw = -w_pre
    softplus = jnp.maximum(nw, 0.0) + jnp.log(1.0 + jnp.exp(-jnp.abs(nw)))
    lw = -jnp.exp(-softplus - 0.5)
    a = jax.nn.sigmoid(a0_ref[...] + _dot(x4, aw2_ref[...]))
    gate = _dot(jax.nn.sigmoid(x4), gw2_ref[...])

    lane = lax.broadcasted_iota(jnp.int32, (1, W), 1)
    head_masks = [((lane >= h * N) & (lane < (h + 1) * N)).astype(F32) for h in range(H)]
    hr = lax.broadcasted_iota(jnp.int32, (W, W), 0)
    hc = lax.broadcasted_iota(jnp.int32, (W, W), 1)
    head_ones = ((hr // N) == (hc // N)).astype(F32)

    kk = k * kk_ref[...]
    k = k * (1.0 + (a - 1.0) * ka_ref[...])
    kk = kk / jnp.maximum(jnp.sqrt(_dot(kk * kk, head_ones)), 1e-12)
    bonus = _dot(r * k * rk_ref[...], head_ones) * v

    if t_valid < C:
        valid = row < t_valid
        lw = jnp.where(valid, lw, 0.0)
        kk = jnp.where(valid, kk, 0.0)
        k = jnp.where(valid, k, 0.0)
        v = jnp.where(valid, v, 0.0)

    tr = lax.broadcasted_iota(jnp.int32, (C, C), 0)
    tc = lax.broadcasted_iota(jnp.int32, (C, C), 1)
    cum = _mask_dot(tc <= tr, lw)
    g_in = jnp.exp(cum)
    g_ex = jnp.exp(cum - lw)
    g_inv = jnp.exp(-cum)
    g_last = g_in[C - 1:C, :]

    a_s = _head_stack(-kk * g_ex, head_masks)
    b_s = _head_stack(kk * a * g_inv, head_masks)
    k_s = _head_stack(k * g_inv, head_masks)
    r_s = _head_stack(r * g_in, head_masks)
    v_s = jnp.concatenate([v[:, h * N:(h + 1) * N] for h in range(H)], axis=0)

    HC = H * C
    sr = lax.broadcasted_iota(jnp.int32, (HC, HC), 0)
    sc = lax.broadcasted_iota(jnp.int32, (HC, HC), 1)
    same_head = (sr // C) == (sc // C)
    strict = same_head & ((sc % C) < (sr % C))
    incl = same_head & ((sc % C) <= (sr % C))
    m_ab = jnp.where(strict, _dot(a_s, b_s, _NT), 0.0)
    m_ak = jnp.where(strict, _dot(a_s, k_s, _NT), 0.0)
    p_rb = jnp.where(incl, _dot(r_s, b_s, _NT), 0.0)
    p_rk = jnp.where(incl, _dot(r_s, k_s, _NT), 0.0)

    eye = (sr == sc).astype(F32)
    t_inv = eye + m_ab
    power = m_ab
    span = 2
    while span < C:
        power = _dot(power, power)
        t_inv = t_inv + _dot(t_inv, power)
        span *= 2
    w_m = _dot(t_inv, a_s)
    u_m = _dot(t_inv, _dot(m_ak, v_s))

    s = s_scr[...]
    c_m = _dot(w_m, s, _NT) + u_m
    o_s = _dot(r_s, s, _NT) + _dot(p_rb, c_m) + _dot(p_rk, v_s)
    s_new = (s + _dot(c_m, b_s, _TN) + _dot(v_s, k_s, _TN)) * g_last
    s_scr[...] = s_new

    mu_o = jnp.mean(o_s, axis=-1, keepdims=True)
    oc = o_s - mu_o
    var_o = jnp.mean(oc * oc, axis=-1, keepdims=True)
    on = oc * lax.rsqrt(var_o + RW_GN_EPS)
    o = jnp.concatenate([on[h * C:(h + 1) * C, :] for h in range(H)], axis=1)
    o = (o * gng_ref[...] + gnb_ref[...] + bonus) * gate
    o_ref[...] = o.astype(o_ref.dtype)

    @pl.when(step == pl.num_programs(1) - 1)
    def _():
        for h in range(H):
            s_out_ref[h] = s_new[:, h * N:(h + 1) * N]


def _rwkv(z, shift0, s0, p, t_valid, chunk):
    B, T, _ = z.shape
    W = RW_WIDTH
    has_state = s0 is not None
    zcol = lambda j: pl.BlockSpec((None, chunk, W), lambda b, t, j=j: (b, t, j))
    vec = lambda a: _resident(a.shape)
    weights = (p["mu"], p["w0"], p["w_w2"], p["a0"], p["a_w2"], p["g_w2"], p["k_k"], p["k_a"],
               p["r_k"], p["gn_g"], p["gn_b"])
    args, specs = [z, z, z, z], [zcol(0), zcol(1), zcol(2), zcol(3)]
    if has_state:
        args += [shift0, s0]
        specs += [pl.BlockSpec((None, 1, RW_PROJ), lambda b, t: (b, 0, 0)),
                  pl.BlockSpec((None, RW_HEADS, RW_DIM, RW_DIM), lambda b, t: (b, 0, 0, 0))]
    args += list(weights)
    specs += [vec(w) for w in weights]
    kern = functools.partial(_rwkv_kernel, chunk=chunk, t_valid=t_valid, has_state=has_state)
    return pl.pallas_call(
        kern,
        grid=(B, T // chunk),
        in_specs=specs,
        out_specs=[pl.BlockSpec((None, chunk, W), lambda b, t: (b, t, 0)),
                   pl.BlockSpec((None, RW_HEADS, RW_DIM, RW_DIM), lambda b, t: (b, 0, 0, 0))],
        out_shape=[jax.ShapeDtypeStruct((B, T, W), BF16),
                   jax.ShapeDtypeStruct((B, RW_HEADS, RW_DIM, RW_DIM), F32)],
        scratch_shapes=[pltpu.VMEM((RW_DIM, W), F32), pltpu.VMEM((1, RW_PROJ), F32)],
        compiler_params=_params(("parallel", "arbitrary")),
        name="rwkv7",
    )(*args)


def _hgrn_kernel(*refs, chunk, t_valid, has_state):
    if has_state:
        zq_ref, zf_ref, zi_ref, zg_ref, s0_ref, lb_ref, ng_ref, o_ref, s_out_ref, s_scr = refs
    else:
        zq_ref, zf_ref, zi_ref, zg_ref, lb_ref, ng_ref, o_ref, s_out_ref, s_scr = refs
    C, N, H = chunk, HG_DIM, HG_HEADS
    sub = min(HG_SUB, C)
    step = pl.program_id(1)

    @pl.when(step == 0)
    def _():
        if has_state:
            s_scr[...] = s0_ref[...]
        else:
            s_scr[...] = jnp.zeros_like(s_scr)

    zq = zq_ref[...]
    q = zq * jax.nn.sigmoid(zq)
    lb = lb_ref[...]
    f = lb + (1.0 - lb) * jax.nn.sigmoid(zf_ref[...])
    log_f = jnp.log(jnp.maximum(f, F_MIN))
    k = 1.0 - f
    v = zi_ref[...]
    zg = zg_ref[...]
    out_gate = ng_ref[...] * (zg * jax.nn.sigmoid(zg))
    if t_valid < C:
        valid = lax.broadcasted_iota(jnp.int32, (C, 1), 0) < t_valid
        log_f = jnp.where(valid, log_f, 0.0)
        k = jnp.where(valid, k, 0.0)

    tr = lax.broadcasted_iota(jnp.int32, (C, C), 0)
    tc = lax.broadcasted_iota(jnp.int32, (C, C), 1)
    causal = tc <= tr
    cum = _mask_dot(causal, log_f)
    q_dec = q * jnp.exp(cum)
    total = cum[C - 1:C, :]
    k_dec = k * jnp.exp(total - cum)
    decay_all = jnp.exp(total)
    er = lax.broadcasted_iota(jnp.int32, (N, N), 0)
    ec = lax.broadcasted_iota(jnp.int32, (N, N), 1)
    eye = er == ec

    outs = []
    for h in range(H):
        sl = slice(h * N, (h + 1) * N)
        cum_h, q_h, k_h, v_h = cum[:, sl], q[:, sl], k[:, sl], v[:, sl]
        rows = []
        for i in range(C // sub):
            blk = slice(i * sub, (i + 1) * sub)
            ref_row = cum_h[i * sub:i * sub + 1, :]
            q_hat = q_h[blk] * jnp.exp(cum_h[blk] - ref_row)
            k_hat = k_h * jnp.exp(jnp.minimum(ref_row - cum_h, EXP_CLAMP))
            rows.append(_dot(q_hat, k_hat, _NT))
        attn = jnp.where(causal, jnp.concatenate(rows, axis=0) if len(rows) > 1 else rows[0], 0.0)
        s_h = s_scr[h]
        o_h = _dot(attn, v_h) + _dot(q_dec[:, sl], s_h)
        decay_diag = jnp.where(eye, decay_all[:, sl], 0.0)
        s_new = _dot(k_dec[:, sl], v_h, _TN) + _dot(decay_diag, s_h)
        s_scr[h] = s_new
        o_h = o_h * lax.rsqrt(jnp.mean(o_h * o_h, axis=-1, keepdims=True) + RMS_EPS)
        outs.append(o_h)
    o = jnp.concatenate(outs, axis=1) * out_gate
    o_ref[...] = o.astype(o_ref.dtype)

    @pl.when(step == pl.num_programs(1) - 1)
    def _():
        s_out_ref[...] = s_scr[...]


def _hgrn(z, s0, lb, norm_g, t_valid, chunk):
    B, T, _ = z.shape
    W = HG_WIDTH
    first = RW_PROJ // W
    has_state = s0 is not None
    zcol = lambda j: pl.BlockSpec((None, chunk, W), lambda b, t, j=j: (b, t, first + j))
    state_spec = pl.BlockSpec((None, HG_HEADS, HG_DIM, HG_DIM), lambda b, t: (b, 0, 0, 0))
    args, specs = [z, z, z, z], [zcol(0), zcol(1), zcol(2), zcol(3)]
    if has_state:
        args.append(s0)
        specs.append(state_spec)
    args += [lb, norm_g]
    specs += [_resident(lb.shape), _resident(norm_g.shape)]
    kern = functools.partial(_hgrn_kernel, chunk=chunk, t_valid=t_valid, has_state=has_state)
    return pl.pallas_call(
        kern,
        grid=(B, T // chunk),
        in_specs=specs,
        out_specs=[pl.BlockSpec((None, chunk, W), lambda b, t: (b, t, 0)), state_spec],
        out_shape=[jax.ShapeDtypeStruct((B, T, W), BF16),
                   jax.ShapeDtypeStruct((B, HG_HEADS, HG_DIM, HG_DIM), F32)],
        scratch_shapes=[pltpu.VMEM((HG_HEADS, HG_DIM, HG_DIM), F32)],
        compiler_params=_params(("parallel", "arbitrary")),
        name="hgrn2",
    )(*args)


def _gelu(x):
    return 0.5 * x * (1.0 + lax.erf(x * (2.0 ** -0.5)))


def _cm_kernel(zu_ref, zv_ref, ws_ref, bs_ref, g_ref, b_ref, o_ref, v_ref):
    H, N, W = CM_HEADS, CM_DIM, CM_WIDTH
    C = zu_ref.shape[0]
    u = _gelu(zu_ref[...])
    v = _gelu(zv_ref[...])
    hr = lax.broadcasted_iota(jnp.int32, (W, W), 0)
    hc = lax.broadcasted_iota(jnp.int32, (W, W), 1)
    head_mean = jnp.where((hr // N) == (hc // N), 1.0 / N, 0.0).astype(F32)
    mu = _dot(v, head_mean)
    vc = v - mu
    var = _dot(vc * vc, head_mean)
    vn = vc * lax.rsqrt(var + LN_EPS) * g_ref[...] + b_ref[...]
    v_ref[...] = vn
    tr = lax.broadcasted_iota(jnp.int32, (C, C), 0)
    tc = lax.broadcasted_iota(jnp.int32, (C, C), 1)
    lane = lax.broadcasted_iota(jnp.int32, (1, W), 1)
    mixed = bs_ref[...]
    for h in range(H):
        w_causal = jnp.where(tc <= tr, ws_ref[h], 0.0)
        v_head = jnp.where((lane >= h * N) & (lane < (h + 1) * N), vn, 0.0)
        mixed = mixed + _dot(w_causal, v_head)
    o_ref[...] = (u * mixed).astype(o_ref.dtype)


def _chunk_mlp(z, ws, bs_wide, ln_g, ln_b, chunk):
    B, T, _ = z.shape
    W = CM_WIDTH
    first = (RW_PROJ + HG_PROJ) // W
    zcol = lambda j: pl.BlockSpec((None, chunk, W), lambda b, t, j=j: (b, t, first + j))
    out_block = pl.BlockSpec((None, chunk, W), lambda b, t: (b, t, 0))
    return pl.pallas_call(
        _cm_kernel,
        grid=(B, T // chunk),
        in_specs=[zcol(0), zcol(1), _resident(ws.shape), _resident(bs_wide.shape),
                  _resident(ln_g.shape), _resident(ln_b.shape)],
        out_specs=[out_block, out_block],
        out_shape=[jax.ShapeDtypeStruct((B, T, W), BF16), jax.ShapeDtypeStruct((B, T, W), F32)],
        compiler_params=_params(("parallel", "parallel")),
        name="chunk_gmlp",
    )(z, z, ws, bs_wide, ln_g, ln_b)


def _pad_rows(n, mult):
    return -(-n // mult) * mult


def _layer_weights(p, l, lb):
    row = lambda a: a.reshape(1, -1)
    lora_pad = lambda w, start: jnp.zeros((RW_WIDTH, RW_WIDTH), F32).at[start:start + w.shape[0]].set(w)
    mix_out = p["mix_w_out"][l].astype(BF16)
    return dict(
        ffn1_in=p["ffn1_w_in"][l].astype(BF16), ffn1_out=p["ffn1_w_out"][l].astype(BF16),
        ln1_g=row(p["ln1_g"][l]), ln1_b=row(p["ln1_b"][l]),
        mix_in=p["mix_w_in"][l].astype(BF16),
        mix_out_rw=mix_out[:RW_WIDTH], mix_out_hg=mix_out[RW_WIDTH:RW_WIDTH + HG_WIDTH],
        mix_out_cm=mix_out[RW_WIDTH + HG_WIDTH:],
        ln2_g=row(p["ln2_g"][l]), ln2_b=row(p["ln2_b"][l]),
        ffn2_in=p["ffn2_w_in"][l].astype(BF16), ffn2_out=p["ffn2_w_out"][l].astype(BF16),
        ln3_g=row(p["ln3_g"][l]), ln3_b=row(p["ln3_b"][l]),
        rw=dict(mu=row(p["rw_mu"][l]), w0=row(p["rw_w0"][l]),
                w_w2=lora_pad(p["rw_w_w2"][l], 0), a0=row(p["rw_a0"][l]),
                a_w2=lora_pad(p["rw_a_w2"][l], RW_DECAY_LORA),
                g_w2=lora_pad(p["rw_g_w2"][l], RW_DECAY_LORA + RW_AAA_LORA),
                k_k=row(p["rw_k_k"][l]), k_a=row(p["rw_k_a"][l]), r_k=row(p["rw_r_k"][l]),
                gn_g=row(p["rw_gn_g"][l]), gn_b=row(p["rw_gn_b"][l])),
        hg_lb=row(lb[l]), hg_norm_g=row(p["hg_norm_g"][l]),
        cm_ws=p["cm_ws"][l],
        cm_bs=jnp.repeat(p["cm_bs"][l].T, CM_DIM, axis=1),
        cm_ln_g=row(p["cm_ln_g"][l]), cm_ln_b=row(p["cm_ln_b"][l]),
    )


def _run_trunk(x, rw_s0, rw_shift0, hg_s0, weights, t_pad, rw_chunk, hg_chunk, cm_chunk):
    B, T, _ = x.shape
    n = B * T
    xf = x.reshape(n, D_MODEL)
    rw_states, rw_shifts, hg_states, cm_vs = [], [], [], []
    for l in range(DEPTH):
        w = weights[l]
        x1, z = _dense_in(xf, w["ffn1_in"], w["ffn1_out"], w["ln1_g"], w["ln1_b"], w["mix_in"])
        z = z.reshape(B, T, IN_PROJ)
        rw_shifts.append(z[:, T - 1, :RW_PROJ])
        zp = z if t_pad == T else jnp.pad(z, ((0, 0), (0, t_pad - T), (0, 0)))
        has_state = rw_s0 is not None
        o_rw, s_rw = _rwkv(zp, rw_shift0[l][:, None, :] if has_state else None,
                           rw_s0[l] if has_state else None, w["rw"], min(T, rw_chunk), rw_chunk)
        o_hg, s_hg = _hgrn(zp, hg_s0[l] if has_state else None, w["hg_lb"], w["hg_norm_g"],
                           min(T, hg_chunk), hg_chunk)
        o_cm, v_cm = _chunk_mlp(zp, w["cm_ws"][:, :cm_chunk, :cm_chunk], w["cm_bs"][:cm_chunk],
                                w["cm_ln_g"], w["cm_ln_b"], cm_chunk)
        trim = lambda o: o[:, :T].reshape(n, o.shape[-1])
        xf = _dense_out(x1, trim(o_rw), trim(o_hg), trim(o_cm), w["mix_out_rw"], w["mix_out_hg"],
                        w["mix_out_cm"], w["ln2_g"], w["ln2_b"], w["ffn2_in"], w["ffn2_out"],
                        w["ln3_g"], w["ln3_b"])
        rw_states.append(s_rw)
        hg_states.append(s_hg)
        cm_vs.append(v_cm[:, :T])
    return (xf.reshape(B, T, D_MODEL), jnp.stack(rw_states), jnp.stack(rw_shifts),
            jnp.stack(hg_states), jnp.stack(cm_vs))


def kernel(x_prompt, x_sample, state_rwkv, state_rwkv_shift, state_hgrn, ffn1_w_in, ffn1_w_out, ln1_g, ln1_b, mix_w_in, mix_w_out, ln2_g, ln2_b, rw_mu, rw_w0, rw_w_w2, rw_a0, rw_a_w2, rw_g_w2, rw_k_k, rw_k_a, rw_r_k, rw_gn_g, rw_gn_b, hg_lb_logits, hg_norm_g, cm_ws, cm_bs, cm_ln_g, cm_ln_b, ffn2_w_in, ffn2_w_out, ln3_g, ln3_b):
    p = dict(ffn1_w_in=ffn1_w_in, ffn1_w_out=ffn1_w_out, ln1_g=ln1_g, ln1_b=ln1_b,
             mix_w_in=mix_w_in, mix_w_out=mix_w_out, ln2_g=ln2_g, ln2_b=ln2_b,
             rw_mu=rw_mu, rw_w0=rw_w0, rw_w_w2=rw_w_w2, rw_a0=rw_a0, rw_a_w2=rw_a_w2,
             rw_g_w2=rw_g_w2, rw_k_k=rw_k_k, rw_k_a=rw_k_a, rw_r_k=rw_r_k,
             rw_gn_g=rw_gn_g, rw_gn_b=rw_gn_b, hg_norm_g=hg_norm_g,
             cm_ws=cm_ws, cm_bs=cm_bs, cm_ln_g=cm_ln_g, cm_ln_b=cm_ln_b,
             ffn2_w_in=ffn2_w_in, ffn2_w_out=ffn2_w_out, ln3_g=ln3_g, ln3_b=ln3_b)
    s = jax.nn.softmax(hg_lb_logits.astype(F32), axis=0)
    lb = jnp.cumsum(s, axis=0) - s[0]
    weights = [_layer_weights(p, l, lb) for l in range(DEPTH)]

    t_prompt = x_prompt.shape[1]
    y_p, rw_s_p, rw_sh_p, hg_s_p, _ = _run_trunk(
        x_prompt, None, None, None, weights, t_prompt, RW_CHUNK, HG_CHUNK, CM_CHUNK)
    t_sample = x_sample.shape[1]
    t_pad = _pad_rows(t_sample, 8)
    y_s, rw_s_s, rw_sh_s, hg_s_s, cm_v_s = _run_trunk(
        x_sample, state_rwkv, state_rwkv_shift, state_hgrn, weights, t_pad, t_pad, t_pad, t_pad)
    return (y_p, y_s, rw_s_p, rw_sh_p, hg_s_p, rw_s_s, rw_sh_s, hg_s_s, cm_v_s)
```

```python
import functools

import jax
import jax.numpy as jnp
from jax import lax
from jax.experimental import pallas as pl
from jax.experimental.pallas import tpu as pltpu

F32 = jnp.float32
BF16 = jnp.bfloat16

D_MODEL = 1024
DEPTH = 2
RW_HEADS, RW_DIM = 4, 64
RW_WIDTH = RW_HEADS * RW_DIM
RW_DECAY_LORA, RW_AAA_LORA, RW_GATE_LORA = 64, 64, 128
RW_PROJ = 3 * RW_WIDTH + RW_DECAY_LORA + RW_AAA_LORA + RW_GATE_LORA
RW_GN_EPS = 64e-5
HG_HEADS, HG_DIM = 4, 128
HG_WIDTH = HG_HEADS * HG_DIM
HG_PROJ = 4 * HG_WIDTH
RMS_EPS = 1e-6
F_MIN = 1e-30
CM_HEADS, CM_DIM = 4, 64
CM_WIDTH = CM_HEADS * CM_DIM
CM_CHUNK = 128
CM_PROJ = 2 * CM_WIDTH
MIX_WIDTH = RW_WIDTH + HG_WIDTH + CM_WIDTH
IN_PROJ = RW_PROJ + HG_PROJ + CM_PROJ
D_FF = 2816
LN_EPS = 1e-5
ALPHA = (2.0 * DEPTH) ** 0.25

VMEM_LIMIT_BYTES = 56 * 1024 * 1024
SUBLANES = 8
DENSE_ROWS = 256
FF_CHUNK = 1408
RW_CHUNK = 64
RW_GROUP = 64
HG_CHUNK = 64
HG_SUB = 16
EXP_CLAMP = 80.0
PROMPT_STEP_ROWS = 256
SAMPLE_STEP_SEQS = 8
CM_STEP_ROWS = 512


def _params(semantics):
    return pltpu.CompilerParams(dimension_semantics=semantics,
                                vmem_limit_bytes=VMEM_LIMIT_BYTES)


def _resident(shape):
    nd = len(shape)
    return pl.BlockSpec(shape, lambda *_: (0,) * nd, pipeline_mode=pl.Buffered(1))


_NN = ((1,), (0,))
_NT = ((1,), (1,))
_TN = ((0,), (0,))


def _dot(a, b, dims=_NN):
    return lax.dot_general(a.astype(BF16), b.astype(BF16), (dims, ((), ())),
                           preferred_element_type=F32)


def _mask_dot(mask, x):
    m = mask.astype(BF16)
    hi = x.astype(BF16)
    r1 = x - hi.astype(F32)
    mid = r1.astype(BF16)
    lo = (r1 - mid.astype(F32)).astype(BF16)
    dot = lambda t: jnp.dot(m, t, preferred_element_type=F32)
    return dot(hi) + (dot(mid) + dot(lo))


def _chunk_cumsum(x, chunk):
    n = x.shape[0]
    r = lax.broadcasted_iota(jnp.int32, (n, n), 0)
    c = lax.broadcasted_iota(jnp.int32, (n, n), 1)
    return _mask_dot(((r // chunk) == (c // chunk)) & (c <= r), x)


def _layer_norm(x, g, b):
    mu = jnp.mean(x, axis=-1, keepdims=True)
    xc = x - mu
    var = jnp.mean(xc * xc, axis=-1, keepdims=True)
    return xc * lax.rsqrt(var + LN_EPS) * g + b


def _swiglu(xb, w_in_ref, w_out_ref):
    acc = None
    for lo in range(0, D_FF, FF_CHUNK):
        gate = jnp.dot(xb, w_in_ref[:, lo:lo + FF_CHUNK], preferred_element_type=F32)
        up = jnp.dot(xb, w_in_ref[:, D_FF + lo:D_FF + lo + FF_CHUNK], preferred_element_type=F32)
        h = (gate * jax.nn.sigmoid(gate) * up).astype(BF16)
        part = jnp.dot(h, w_out_ref[lo:lo + FF_CHUNK, :], preferred_element_type=F32)
        acc = part if acc is None else acc + part
    return acc


def _dense_in_kernel(x_ref, w_in_ref, w_out_ref, g_ref, b_ref, w_mix_ref, x1_ref, z_ref):
    x = x_ref[...]
    y = _layer_norm(ALPHA * x + 0.5 * _swiglu(x.astype(BF16), w_in_ref, w_out_ref),
                    g_ref[...], b_ref[...])
    x1_ref[...] = y
    z_ref[...] = jnp.dot(y.astype(BF16), w_mix_ref[...], preferred_element_type=F32)


def _dense_in(x, w_in, w_out, g, b, w_mix):
    n = x.shape[0]
    rows = min(DENSE_ROWS, n)
    row_block = lambda width: pl.BlockSpec((rows, width), lambda i: (i, 0))
    return pl.pallas_call(
        _dense_in_kernel,
        grid=(n // rows,),
        in_specs=[row_block(D_MODEL), _resident(w_in.shape), _resident(w_out.shape),
                  _resident(g.shape), _resident(b.shape), _resident(w_mix.shape)],
        out_specs=[row_block(D_MODEL), row_block(IN_PROJ)],
        out_shape=[jax.ShapeDtypeStruct((n, D_MODEL), F32),
                   jax.ShapeDtypeStruct((n, IN_PROJ), F32)],
        compiler_params=_params(("parallel",)),
        name="dense_in",
    )(x, w_in, w_out, g, b, w_mix)


def _dense_out_kernel(x_ref, orw_ref, ohg_ref, ocm_ref, wrw_ref, whg_ref, wcm_ref, g2_ref, b2_ref,
                      w_in_ref, w_out_ref, g3_ref, b3_ref, y_ref):
    mix = (jnp.dot(orw_ref[...], wrw_ref[...], preferred_element_type=F32)
           + jnp.dot(ohg_ref[...], whg_ref[...], preferred_element_type=F32)
           + jnp.dot(ocm_ref[...], wcm_ref[...], preferred_element_type=F32))
    x2 = _layer_norm(ALPHA * x_ref[...] + mix, g2_ref[...], b2_ref[...])
    y_ref[...] = _layer_norm(ALPHA * x2 + 0.5 * _swiglu(x2.astype(BF16), w_in_ref, w_out_ref),
                             g3_ref[...], b3_ref[...])


def _dense_out(x, o_rw, o_hg, o_cm, w_rw, w_hg, w_cm, g2, b2, w_in, w_out, g3, b3):
    n = x.shape[0]
    rows = min(DENSE_ROWS, n)
    row_block = lambda width: pl.BlockSpec((rows, width), lambda i: (i, 0))
    weights = (w_rw, w_hg, w_cm, g2, b2, w_in, w_out, g3, b3)
    return pl.pallas_call(
        _dense_out_kernel,
        grid=(n // rows,),
        in_specs=[row_block(D_MODEL), row_block(RW_WIDTH), row_block(HG_WIDTH), row_block(CM_WIDTH)]
                 + [_resident(w.shape) for w in weights],
        out_specs=row_block(D_MODEL),
        out_shape=jax.ShapeDtypeStruct((n, D_MODEL), F32),
        compiler_params=_params(("parallel",)),
        name="dense_out",
    )(x, o_rw, o_hg, o_cm, *weights)


def _head_stack(x, head_masks):
    return jnp.concatenate([x * m for m in head_masks], axis=0)


def _rwkv_kernel(*refs, chunk, t_valid, n_seq, has_state):
    if has_state:
        (zr_ref, zk_ref, zv_ref, zx_ref, shift_ref, s0_ref, mu_ref, w0_ref, ww2_ref, a0_ref, aw2_ref,
         gw2_ref, kk_ref, ka_ref, rk_ref, gng_ref, gnb_ref, o_ref, s_out_ref, s_scr, prev_scr) = refs
    else:
        (zr_ref, zk_ref, zv_ref, zx_ref, mu_ref, w0_ref, ww2_ref, a0_ref, aw2_ref,
         gw2_ref, kk_ref, ka_ref, rk_ref, gng_ref, gnb_ref, o_ref, s_out_ref, s_scr, prev_scr) = refs
    C, N, H, W, G = chunk, RW_DIM, RW_HEADS, RW_WIDTH, RW_GROUP
    R = zr_ref.shape[0]
    L = R // n_seq
    step = pl.program_id(1)

    @pl.when(step == 0)
    def _():
        for b in range(n_seq):
            if has_state:
                s_scr[b] = jnp.concatenate([s0_ref[b, h] for h in range(H)], axis=1)
                prev_scr[b:b + 1, :] = shift_ref[b]
            else:
                s_scr[b] = jnp.zeros((N, W), F32)
                prev_scr[b:b + 1, :] = jnp.zeros((1, RW_PROJ), F32)

    row = lax.broadcasted_iota(jnp.int32, (R, 1), 0)
    t_in = row % L
    valid = t_in < t_valid

    def load(ref, part):
        cols = slice(part * W, (part + 1) * W)
        z = ref[...]
        if t_valid < L:
            z = jnp.where(valid, z, 0.0)
        firsts = [jnp.broadcast_to(prev_scr[b:b + 1, cols], (L, W)) for b in range(n_seq)]
        first = firsts[0] if n_seq == 1 else jnp.concatenate(firsts, axis=0)
        prev = jnp.where(t_in == 0, first, pltpu.roll(z, 1, axis=0))
        for b in range(n_seq):
            last = b * L + t_valid - 1
            prev_scr[b:b + 1, cols] = z[last:last + 1, :]
        return z + (prev - z) * mu_ref[:, cols]

    r = load(zr_ref, 0)
    k = load(zk_ref, 1)
    v = load(zv_ref, 2)
    x4 = load(zx_ref, 3)

    w_pre = w0_ref[...] + _dot(jnp.tanh(x4), ww2_ref[...])
    nw = -w_pre
    softplus = jnp.maximum(nw, 0.0) + jnp.log(1.0 + jnp.exp(-jnp.abs(nw)))
    lw = -jnp.exp(-softplus - 0.5)
    a = jax.nn.sigmoid(a0_ref[...] + _dot(x4, aw2_ref[...]))
    gate = _dot(jax.nn.sigmoid(x4), gw2_ref[...])

    lane = lax.broadcasted_iota(jnp.int32, (1, W), 1)
    head_masks = [((lane >= h * N) & (lane < (h + 1) * N)).astype(F32) for h in range(H)]
    hr = lax.broadcasted_iota(jnp.int32, (W, W), 0)
    hc = lax.broadcasted_iota(jnp.int32, (W, W), 1)
    head_ones = ((hr // N) == (hc // N)).astype(F32)

    kk = k * kk_ref[...]
    k = k * (1.0 + (a - 1.0) * ka_ref[...])
    sums = _dot(jnp.concatenate([kk * kk, r * k * rk_ref[...]], axis=0), head_ones)
    kk = kk / jnp.maximum(jnp.sqrt(sums[:R]), 1e-12)
    bonus = sums[R:] * v

    if t_valid < L:
        lw = jnp.where(valid, lw, 0.0)
        kk = jnp.where(valid, kk, 0.0)
        k = jnp.where(valid, k, 0.0)
        v = jnp.where(valid, v, 0.0)

    cum = _chunk_cumsum(lw, C)
    g_in = jnp.exp(cum)
    g_inv = jnp.exp(-cum)
    a_hat = -kk * jnp.exp(cum - lw)
    b_chk = kk * a * g_inv
    k_chk = k * g_inv
    r_hat = r * g_in

    HG_ = H * G
    sr = lax.broadcasted_iota(jnp.int32, (HG_, HG_), 0)
    sc = lax.broadcasted_iota(jnp.int32, (HG_, HG_), 1)
    same_chunk = (sr // C) == (sc // C)
    strict = same_chunk & (sc < sr)
    incl = same_chunk & (sc <= sr)
    eye = (sr == sc).astype(F32)
    n_sub = G // C

    def gather(x, sub):
        if n_sub == 1:
            return x
        return jnp.concatenate([x[h * G + sub * C:h * G + (sub + 1) * C] for h in range(H)], axis=0)

    def scatter(pieces):
        if n_sub == 1:
            return pieces[0]
        return jnp.concatenate([pieces[sub][h * C:(h + 1) * C]
                                for h in range(H) for sub in range(n_sub)], axis=0)

    groups = range(R // G)
    a_st, b_st, k_st, r_st, v_st = [], [], [], [], []
    for gi in groups:
        rows = slice(gi * G, (gi + 1) * G)
        a_st.append(_head_stack(a_hat[rows], head_masks))
        b_st.append(_head_stack(b_chk[rows], head_masks))
        k_st.append(_head_stack(k_chk[rows], head_masks))
        r_st.append(_head_stack(r_hat[rows], head_masks))
        v_st.append(jnp.concatenate([v[rows, h * N:(h + 1) * N] for h in range(H)], axis=0))
    m_abs = [jnp.where(strict, _dot(a_st[gi], b_st[gi], _NT), 0.0) for gi in groups]
    m_aks = [jnp.where(strict, _dot(a_st[gi], k_st[gi], _NT), 0.0) for gi in groups]
    p_rbs = [jnp.where(incl, _dot(r_st[gi], b_st[gi], _NT), 0.0) for gi in groups]
    p_rks = [jnp.where(incl, _dot(r_st[gi], k_st[gi], _NT), 0.0) for gi in groups]
    t_invs = [eye + m for m in m_abs]
    powers = list(m_abs)
    span = 2
    while span < C:
        powers = [_dot(pw, pw) for pw in powers]
        t_invs = [t + _dot(t, pw) for t, pw in zip(t_invs, powers)]
        span *= 2
    w_ms = [_dot(t_invs[gi], a_st[gi]) for gi in groups]
    mv = [_dot(m_aks[gi], v_st[gi]) for gi in groups]
    u_ms = [_dot(t_invs[gi], mv[gi]) for gi in groups]

    states = [s_scr[b] for b in range(n_seq)]
    outs = []
    for gi in groups:
        b_s, k_s, r_s, v_s = b_st[gi], k_st[gi], r_st[gi], v_st[gi]
        p_rb, p_rk, w_m, u_m = p_rbs[gi], p_rks[gi], w_ms[gi], u_ms[gi]
        c_parts, rs_parts = [], []
        for sub in range(n_sub):
            b = (gi * G + sub * C) // L
            s = states[b]
            hc_rows = H * C
            wr = jnp.concatenate([gather(w_m, sub), gather(r_s, sub)], axis=0)
            cr = _dot(wr, s, _NT)
            c_sub = cr[:hc_rows] + gather(u_m, sub)
            rs_parts.append(cr[hc_rows:])
            c_parts.append(c_sub)
            upd = _dot(jnp.concatenate([c_sub, gather(v_s, sub)], axis=0),
                       jnp.concatenate([gather(b_s, sub), gather(k_s, sub)], axis=0), _TN)
            last = gi * G + sub * C + C - 1
            states[b] = (s + upd) * g_in[last:last + 1, :]
        c_m = scatter(c_parts)
        o_s = scatter(rs_parts) + _dot(jnp.concatenate([p_rb, p_rk], axis=1),
                                       jnp.concatenate([c_m, v_s], axis=0))

        mu_o = jnp.mean(o_s, axis=-1, keepdims=True)
        oc = o_s - mu_o
        var_o = jnp.mean(oc * oc, axis=-1, keepdims=True)
        on = oc * lax.rsqrt(var_o + RW_GN_EPS)
        outs.append(jnp.concatenate([on[h * G:(h + 1) * G, :] for h in range(H)], axis=1))

    o = outs[0] if len(outs) == 1 else jnp.concatenate(outs, axis=0)
    o_ref[...] = ((o * gng_ref[...] + gnb_ref[...] + bonus) * gate).astype(o_ref.dtype)
    for b in range(n_seq):
        s_scr[b] = states[b]

    @pl.when(step == pl.num_programs(1) - 1)
    def _():
        for b in range(n_seq):
            for h in range(H):
                s_out_ref[b, h] = states[b][:, h * N:(h + 1) * N]


def _seq_grid(n_rows, n_batch, n_seq, seq_rows):
    rows_per_seq = n_rows // n_batch
    time_steps = rows_per_seq // seq_rows
    assert n_seq == 1 or time_steps == 1
    return (n_batch // n_seq, time_steps), (lambda i, t: i * time_steps + t)


def _rwkv(z, shift0, s0, p, n_batch, t_valid, chunk, n_seq, seq_rows):
    W = RW_WIDTH
    has_state = s0 is not None
    grid, row_block = _seq_grid(z.shape[0], n_batch, n_seq, seq_rows)
    R = n_seq * seq_rows
    zcol = lambda j: pl.BlockSpec((R, W), lambda i, t, j=j: (row_block(i, t), j))
    state_spec = pl.BlockSpec((n_seq, RW_HEADS, RW_DIM, RW_DIM), lambda i, t: (i, 0, 0, 0))
    weights = (p["mu"], p["w0"], p["w_w2"], p["a0"], p["a_w2"], p["g_w2"], p["k_k"], p["k_a"],
               p["r_k"], p["gn_g"], p["gn_b"])
    args, specs = [z, z, z, z], [zcol(0), zcol(1), zcol(2), zcol(3)]
    if has_state:
        args += [shift0, s0]
        specs += [pl.BlockSpec((n_seq, 1, RW_PROJ), lambda i, t: (i, 0, 0)), state_spec]
    args += list(weights)
    specs += [_resident(w.shape) for w in weights]
    kern = functools.partial(_rwkv_kernel, chunk=chunk, t_valid=t_valid, n_seq=n_seq,
                             has_state=has_state)
    return pl.pallas_call(
        kern,
        grid=grid,
        in_specs=specs,
        out_specs=[pl.BlockSpec((R, W), lambda i, t: (row_block(i, t), 0)), state_spec],
        out_shape=[jax.ShapeDtypeStruct((z.shape[0], W), BF16),
                   jax.ShapeDtypeStruct((n_batch, RW_HEADS, RW_DIM, RW_DIM), F32)],
        scratch_shapes=[pltpu.VMEM((n_seq, RW_DIM, W), F32), pltpu.VMEM((n_seq, RW_PROJ), F32)],
        compiler_params=_params(("parallel", "arbitrary")),
        name="rwkv7",
    )(*args)


def _hgrn_kernel(*refs, chunk, t_valid, n_seq, has_state):
    if has_state:
        zq_ref, zf_ref, zi_ref, zg_ref, s0_ref, lb_ref, ng_ref, o_ref, s_out_ref, s_scr = refs
    else:
        zq_ref, zf_ref, zi_ref, zg_ref, lb_ref, ng_ref, o_ref, s_out_ref, s_scr = refs
    C, N, H = chunk, HG_DIM, HG_HEADS
    R = zq_ref.shape[0]
    L = R // n_seq
    sub = min(HG_SUB, C)
    step = pl.program_id(1)

    @pl.when(step == 0)
    def _():
        for b in range(n_seq):
            for h in range(H):
                s_scr[b, h] = s0_ref[b, h].T if has_state else jnp.zeros((N, N), F32)

    zq = zq_ref[...]
    q = zq * jax.nn.sigmoid(zq)
    lb = lb_ref[...]
    f = lb + (1.0 - lb) * jax.nn.sigmoid(zf_ref[...])
    log_f = jnp.log(jnp.maximum(f, F_MIN))
    k = 1.0 - f
    v = zi_ref[...]
    zg = zg_ref[...]
    out_gate = ng_ref[...] * (zg * jax.nn.sigmoid(zg))
    if t_valid < L:
        valid = (lax.broadcasted_iota(jnp.int32, (R, 1), 0) % L) < t_valid
        log_f = jnp.where(valid, log_f, 0.0)
        k = jnp.where(valid, k, 0.0)

    cum_all = _chunk_cumsum(log_f, C)
    tr = lax.broadcasted_iota(jnp.int32, (C, C), 0)
    tc = lax.broadcasted_iota(jnp.int32, (C, C), 1)
    causal = tc <= tr

    states = [[s_scr[b, h] for h in range(H)] for b in range(n_seq)]
    out_rows = []
    for c in range(R // C):
        b = (c * C) // L
        rows = slice(c * C, (c + 1) * C)
        outs = []
        for h in range(H):
            sl = slice(h * N, (h + 1) * N)
            cum, q_h, k_h, v_h = cum_all[rows, sl], q[rows, sl], k[rows, sl], v[rows, sl]
            total = cum[C - 1:C, :]
            score_rows = []
            for i in range(C // sub):
                blk = slice(i * sub, (i + 1) * sub)
                ref_row = cum[i * sub:i * sub + 1, :]
                q_hat = q_h[blk] * jnp.exp(cum[blk] - ref_row)
                k_hat = k_h * jnp.exp(jnp.minimum(ref_row - cum, EXP_CLAMP))
                score_rows.append(_dot(q_hat, k_hat, _NT))
            scores = score_rows[0] if len(score_rows) == 1 else jnp.concatenate(score_rows, axis=0)
            s_t = states[b][h]
            o_h = _dot(jnp.where(causal, scores, 0.0), v_h) + _dot(q_h * jnp.exp(cum), s_t, _NT)
            states[b][h] = s_t * jnp.exp(total) + _dot(v_h, k_h * jnp.exp(total - cum), _TN)
            o_h = o_h * lax.rsqrt(jnp.mean(o_h * o_h, axis=-1, keepdims=True) + RMS_EPS)
            outs.append(o_h)
        out_rows.append(jnp.concatenate(outs, axis=1))
    o = out_rows[0] if len(out_rows) == 1 else jnp.concatenate(out_rows, axis=0)
    o_ref[...] = (o * out_gate).astype(o_ref.dtype)
    for b in range(n_seq):
        for h in range(H):
            s_scr[b, h] = states[b][h]

    @pl.when(step == pl.num_programs(1) - 1)
    def _():
        for b in range(n_seq):
            for h in range(H):
                s_out_ref[b, h] = states[b][h].T


def _hgrn(z, s0, lb, norm_g, n_batch, t_valid, chunk, n_seq, seq_rows):
    W = HG_WIDTH
    first = RW_PROJ // W
    has_state = s0 is not None
    grid, row_block = _seq_grid(z.shape[0], n_batch, n_seq, seq_rows)
    R = n_seq * seq_rows
    zcol = lambda j: pl.BlockSpec((R, W), lambda i, t, j=j: (row_block(i, t), first + j))
    state_spec = pl.BlockSpec((n_seq, HG_HEADS, HG_DIM, HG_DIM), lambda i, t: (i, 0, 0, 0))
    args, specs = [z, z, z, z], [zcol(0), zcol(1), zcol(2), zcol(3)]
    if has_state:
        args.append(s0)
        specs.append(state_spec)
    args += [lb, norm_g]
    specs += [_resident(lb.shape), _resident(norm_g.shape)]
    kern = functools.partial(_hgrn_kernel, chunk=chunk, t_valid=t_valid, n_seq=n_seq,
                             has_state=has_state)
    return pl.pallas_call(
        kern,
        grid=grid,
        in_specs=specs,
        out_specs=[pl.BlockSpec((R, W), lambda i, t: (row_block(i, t), 0)), state_spec],
        out_shape=[jax.ShapeDtypeStruct((z.shape[0], W), BF16),
                   jax.ShapeDtypeStruct((n_batch, HG_HEADS, HG_DIM, HG_DIM), F32)],
        scratch_shapes=[pltpu.VMEM((n_seq, HG_HEADS, HG_DIM, HG_DIM), F32)],
        compiler_params=_params(("parallel", "arbitrary")),
        name="hgrn2",
    )(*args)


def _gelu(x):
    return 0.5 * x * (1.0 + lax.erf(x * (2.0 ** -0.5)))


def _cm_kernel(zu_ref, zv_ref, ws_ref, bs_ref, g_ref, b_ref, o_ref, *v_out):
    H, N, W, C = CM_HEADS, CM_DIM, CM_WIDTH, CM_CHUNK
    u = _gelu(zu_ref[...])
    v = _gelu(zv_ref[...])
    hr = lax.broadcasted_iota(jnp.int32, (W, W), 0)
    hc = lax.broadcasted_iota(jnp.int32, (W, W), 1)
    head_mean = jnp.where((hr // N) == (hc // N), 1.0 / N, 0.0).astype(F32)
    vc = v - _dot(v, head_mean)
    var = _dot(vc * vc, head_mean)
    vn = vc * lax.rsqrt(var + LN_EPS) * g_ref[...] + b_ref[...]
    if v_out:
        v_out[0][...] = vn
    tr = lax.broadcasted_iota(jnp.int32, (C, C), 0)
    tc = lax.broadcasted_iota(jnp.int32, (C, C), 1)
    lane = lax.broadcasted_iota(jnp.int32, (1, W), 1)
    w_causal = [jnp.where(tc <= tr, ws_ref[h], 0.0).astype(BF16) for h in range(H)]
    head_cols = [(lane >= h * N) & (lane < (h + 1) * N) for h in range(H)]
    vb = vn.astype(BF16)
    mixed = []
    for c in range(zu_ref.shape[0] // C):
        v_c = vb[c * C:(c + 1) * C]
        acc = bs_ref[...]
        for h in range(H):
            acc = acc + jnp.dot(w_causal[h], jnp.where(head_cols[h], v_c, jnp.zeros_like(v_c)),
                                preferred_element_type=F32)
        mixed.append(acc)
    mixed = mixed[0] if len(mixed) == 1 else jnp.concatenate(mixed, axis=0)
    o_ref[...] = (u * mixed).astype(o_ref.dtype)


def _chunk_mlp(z, ws, bs_wide, ln_g, ln_b, want_v):
    n, W = z.shape[0], CM_WIDTH
    rows = min(CM_STEP_ROWS, n)
    first = (RW_PROJ + HG_PROJ) // W
    zcol = lambda j: pl.BlockSpec((rows, W), lambda i, j=j: (i, first + j))
    out_block = pl.BlockSpec((rows, W), lambda i: (i, 0))
    out_specs, out_shape = [out_block], [jax.ShapeDtypeStruct((n, W), BF16)]
    if want_v:
        out_specs.append(out_block)
        out_shape.append(jax.ShapeDtypeStruct((n, W), F32))
    return pl.pallas_call(
        _cm_kernel,
        grid=(n // rows,),
        in_specs=[zcol(0), zcol(1), _resident(ws.shape), _resident(bs_wide.shape),
                  _resident(ln_g.shape), _resident(ln_b.shape)],
        out_specs=out_specs,
        out_shape=out_shape,
        compiler_params=_params(("parallel",)),
        name="chunk_gmlp",
    )(z, z, ws, bs_wide, ln_g, ln_b)


def _layer_weights(p, l, lb):
    row = lambda a: a.reshape(1, -1)
    lora_pad = lambda w, start: jnp.zeros((RW_WIDTH, RW_WIDTH), F32).at[start:start + w.shape[0]].set(w)
    mix_out = p["mix_w_out"][l].astype(BF16)
    return dict(
        ffn1_in=p["ffn1_w_in"][l].astype(BF16), ffn1_out=p["ffn1_w_out"][l].astype(BF16),
        ln1_g=row(p["ln1_g"][l]), ln1_b=row(p["ln1_b"][l]),
        mix_in=p["mix_w_in"][l].astype(BF16),
        mix_out_rw=mix_out[:RW_WIDTH], mix_out_hg=mix_out[RW_WIDTH:RW_WIDTH + HG_WIDTH],
        mix_out_cm=mix_out[RW_WIDTH + HG_WIDTH:],
        ln2_g=row(p["ln2_g"][l]), ln2_b=row(p["ln2_b"][l]),
        ffn2_in=p["ffn2_w_in"][l].astype(BF16), ffn2_out=p["ffn2_w_out"][l].astype(BF16),
        ln3_g=row(p["ln3_g"][l]), ln3_b=row(p["ln3_b"][l]),
        rw=dict(mu=row(p["rw_mu"][l]), w0=row(p["rw_w0"][l]),
                w_w2=lora_pad(p["rw_w_w2"][l], 0), a0=row(p["rw_a0"][l]),
                a_w2=lora_pad(p["rw_a_w2"][l], RW_DECAY_LORA),
                g_w2=lora_pad(p["rw_g_w2"][l], RW_DECAY_LORA + RW_AAA_LORA),
                k_k=row(p["rw_k_k"][l]), k_a=row(p["rw_k_a"][l]), r_k=row(p["rw_r_k"][l]),
                gn_g=row(p["rw_gn_g"][l]), gn_b=row(p["rw_gn_b"][l])),
        hg_lb=row(lb[l]), hg_norm_g=row(p["hg_norm_g"][l]),
        cm_ws=p["cm_ws"][l],
        cm_bs=jnp.repeat(p["cm_bs"][l].T, CM_DIM, axis=1),
        cm_ln_g=row(p["cm_ln_g"][l]), cm_ln_b=row(p["cm_ln_b"][l]),
    )


def _short_chunk_mixing(ws, bs_wide, t_pad):
    reps = CM_CHUNK // t_pad
    eye = jnp.eye(reps, dtype=ws.dtype)
    ws_bd = jnp.stack([jnp.kron(eye, ws[h, :t_pad, :t_pad]) for h in range(CM_HEADS)])
    return ws_bd, jnp.tile(bs_wide[:t_pad], (reps, 1))


def _run_trunk(x, rw_s0, rw_shift0, hg_s0, weights):
    B, T, _ = x.shape
    n = B * T
    has_state = rw_s0 is not None
    short = T < RW_CHUNK
    t_pad = -(-T // SUBLANES) * SUBLANES if short else T
    if short:
        chunk, n_seq, seq_rows = t_pad, SAMPLE_STEP_SEQS, t_pad
    else:
        chunk, n_seq, seq_rows = RW_CHUNK, 1, PROMPT_STEP_ROWS
    xf = x.reshape(n, D_MODEL)
    rw_states, rw_shifts, hg_states, cm_vs = [], [], [], []
    for l in range(DEPTH):
        w = weights[l]
        x1, z = _dense_in(xf, w["ffn1_in"], w["ffn1_out"], w["ln1_g"], w["ln1_b"], w["mix_in"])
        z3 = z.reshape(B, T, IN_PROJ)
        rw_shifts.append(z3[:, T - 1, :RW_PROJ])
        if t_pad != T:
            z = jnp.pad(z3, ((0, 0), (0, t_pad - T), (0, 0))).reshape(B * t_pad, IN_PROJ)
        o_rw, s_rw = _rwkv(z, rw_shift0[l][:, None, :] if has_state else None,
                           rw_s0[l] if has_state else None, w["rw"], B, min(T, seq_rows), chunk,
                           n_seq, seq_rows)
        o_hg, s_hg = _hgrn(z, hg_s0[l] if has_state else None, w["hg_lb"], w["hg_norm_g"], B,
                           min(T, seq_rows), HG_CHUNK if not short else chunk, n_seq, seq_rows)
        if short:
            cm_ws, cm_bs = _short_chunk_mixing(w["cm_ws"], w["cm_bs"], t_pad)
        else:
            cm_ws, cm_bs = w["cm_ws"], w["cm_bs"]
        cm_out = _chunk_mlp(z, cm_ws, cm_bs, w["cm_ln_g"], w["cm_ln_b"], want_v=has_state)
        trim = lambda o: o.reshape(B, t_pad, o.shape[-1])[:, :T].reshape(n, o.shape[-1])
        xf = _dense_out(x1, trim(o_rw), trim(o_hg), trim(cm_out[0]), w["mix_out_rw"],
                        w["mix_out_hg"], w["mix_out_cm"], w["ln2_g"], w["ln2_b"], w["ffn2_in"],
                        w["ffn2_out"], w["ln3_g"], w["ln3_b"])
        rw_states.append(s_rw)
        hg_states.append(s_hg)
        if has_state:
            cm_vs.append(cm_out[1].reshape(B, t_pad, CM_WIDTH)[:, :T])
    return (xf.reshape(B, T, D_MODEL), jnp.stack(rw_states), jnp.stack(rw_shifts),
            jnp.stack(hg_states), jnp.stack(cm_vs) if has_state else None)


def kernel(x_prompt, x_sample, state_rwkv, state_rwkv_shift, state_hgrn, ffn1_w_in, ffn1_w_out, ln1_g, ln1_b, mix_w_in, mix_w_out, ln2_g, ln2_b, rw_mu, rw_w0, rw_w_w2, rw_a0, rw_a_w2, rw_g_w2, rw_k_k, rw_k_a, rw_r_k, rw_gn_g, rw_gn_b, hg_lb_logits, hg_norm_g, cm_ws, cm_bs, cm_ln_g, cm_ln_b, ffn2_w_in, ffn2_w_out, ln3_g, ln3_b):
    p = dict(ffn1_w_in=ffn1_w_in, ffn1_w_out=ffn1_w_out, ln1_g=ln1_g, ln1_b=ln1_b,
             mix_w_in=mix_w_in, mix_w_out=mix_w_out, ln2_g=ln2_g, ln2_b=ln2_b,
             rw_mu=rw_mu, rw_w0=rw_w0, rw_w_w2=rw_w_w2, rw_a0=rw_a0, rw_a_w2=rw_a_w2,
             rw_g_w2=rw_g_w2, rw_k_k=rw_k_k, rw_k_a=rw_k_a, rw_r_k=rw_r_k,
             rw_gn_g=rw_gn_g, rw_gn_b=rw_gn_b, hg_norm_g=hg_norm_g,
             cm_ws=cm_ws, cm_bs=cm_bs, cm_ln_g=cm_ln_g, cm_ln_b=cm_ln_b,
             ffn2_w_in=ffn2_w_in, ffn2_w_out=ffn2_w_out, ln3_g=ln3_g, ln3_b=ln3_b)
    s = jax.nn.softmax(hg_lb_logits.astype(F32), axis=0)
    lb = jnp.cumsum(s, axis=0) - s[0]
    weights = [_layer_weights(p, l, lb) for l in range(DEPTH)]
    y_p, rw_s_p, rw_sh_p, hg_s_p, _ = _run_trunk(x_prompt, None, None, None, weights)
    y_s, rw_s_s, rw_sh_s, hg_s_s, cm_v_s = _run_trunk(x_sample, state_rwkv, state_rwkv_shift,
                                                      state_hgrn, weights)
    return (y_p, y_s, rw_s_p, rw_sh_p, hg_s_p, rw_s_s, rw_sh_s, hg_s_s, cm_v_s)
```

```python
import functools

import jax
import jax.numpy as jnp
from jax import lax
from jax.experimental import pallas as pl
from jax.experimental.pallas import tpu as pltpu

F32 = jnp.float32
BF16 = jnp.bfloat16

D_MODEL = 1024
DEPTH = 2
RW_HEADS, RW_DIM = 4, 64
RW_WIDTH = RW_HEADS * RW_DIM
RW_DECAY_LORA, RW_AAA_LORA, RW_GATE_LORA = 64, 64, 128
RW_PROJ = 3 * RW_WIDTH + RW_DECAY_LORA + RW_AAA_LORA + RW_GATE_LORA
RW_GN_EPS = 64e-5
HG_HEADS, HG_DIM = 4, 128
HG_WIDTH = HG_HEADS * HG_DIM
HG_PROJ = 4 * HG_WIDTH
RMS_EPS = 1e-6
F_MIN = 1e-30
CM_HEADS, CM_DIM = 4, 64
CM_WIDTH = CM_HEADS * CM_DIM
CM_CHUNK = 128
CM_PROJ = 2 * CM_WIDTH
MIX_WIDTH = RW_WIDTH + HG_WIDTH + CM_WIDTH
IN_PROJ = RW_PROJ + HG_PROJ + CM_PROJ
D_FF = 2816
LN_EPS = 1e-5
ALPHA = (2.0 * DEPTH) ** 0.25

VMEM_LIMIT_BYTES = 56 * 1024 * 1024
SUBLANES = 8
DENSE_ROWS = 256
FF_CHUNK = 1408
RW_CHUNK = 64
RW_GROUP = 64
HG_CHUNK = 64
HG_SUB = 16
EXP_CLAMP = 80.0
RW_BATCH_GROUPS = 4
RW_STEP_ROWS = 1024
HG_STEP_ROWS = 512
SAMPLE_STEP_SEQS = 8
CM_STEP_ROWS = 512


def _params(semantics):
    return pltpu.CompilerParams(dimension_semantics=semantics,
                                vmem_limit_bytes=VMEM_LIMIT_BYTES)


def _resident(shape):
    nd = len(shape)
    return pl.BlockSpec(shape, lambda *_: (0,) * nd, pipeline_mode=pl.Buffered(1))


_NN = ((1,), (0,))
_NT = ((1,), (1,))
_TN = ((0,), (0,))


def _dot(a, b, dims=_NN):
    return lax.dot_general(a.astype(BF16), b.astype(BF16), (dims, ((), ())),
                           preferred_element_type=F32)


def _mask_dot(mask, x):
    m = mask.astype(BF16)
    hi = x.astype(BF16)
    r1 = x - hi.astype(F32)
    mid = r1.astype(BF16)
    lo = (r1 - mid.astype(F32)).astype(BF16)
    dot = lambda t: jnp.dot(m, t, preferred_element_type=F32)
    return dot(hi) + (dot(mid) + dot(lo))


def _chunk_cumsum(x, chunk):
    n = x.shape[0]
    r = lax.broadcasted_iota(jnp.int32, (n, n), 0)
    c = lax.broadcasted_iota(jnp.int32, (n, n), 1)
    return _mask_dot(((r // chunk) == (c // chunk)) & (c <= r), x)


def _layer_norm(x, g, b):
    mu = jnp.mean(x, axis=-1, keepdims=True)
    xc = x - mu
    var = jnp.mean(xc * xc, axis=-1, keepdims=True)
    return xc * lax.rsqrt(var + LN_EPS) * g + b


def _swiglu(xb, w_in_ref, w_out_ref):
    acc = None
    for lo in range(0, D_FF, FF_CHUNK):
        gate = jnp.dot(xb, w_in_ref[:, lo:lo + FF_CHUNK], preferred_element_type=F32)
        up = jnp.dot(xb, w_in_ref[:, D_FF + lo:D_FF + lo + FF_CHUNK], preferred_element_type=F32)
        h = (gate * jax.nn.sigmoid(gate) * up).astype(BF16)
        part = jnp.dot(h, w_out_ref[lo:lo + FF_CHUNK, :], preferred_element_type=F32)
        acc = part if acc is None else acc + part
    return acc


def _dense_in_kernel(x_ref, w_in_ref, w_out_ref, g_ref, b_ref, w_mix_ref, x1_ref, z_ref):
    x = x_ref[...]
    y = _layer_norm(ALPHA * x + 0.5 * _swiglu(x.astype(BF16), w_in_ref, w_out_ref),
                    g_ref[...], b_ref[...])
    x1_ref[...] = y
    z_ref[...] = jnp.dot(y.astype(BF16), w_mix_ref[...], preferred_element_type=F32)


def _layer_resident(stacked, layer):
    tail = (0,) * (stacked.ndim - 1)
    return pl.BlockSpec((None,) + stacked.shape[1:], lambda *_: (layer,) + tail,
                        pipeline_mode=pl.Buffered(1))


def _dense_in(x, layer, w_in, w_out, g, b, w_mix):
    n = x.shape[0]
    rows = min(DENSE_ROWS, n)
    row_block = lambda width: pl.BlockSpec((rows, width), lambda i: (i, 0))
    return pl.pallas_call(
        _dense_in_kernel,
        grid=(n // rows,),
        in_specs=[row_block(D_MODEL)] + [_layer_resident(w, layer) for w in (w_in, w_out, g, b, w_mix)],
        out_specs=[row_block(D_MODEL), row_block(IN_PROJ)],
        out_shape=[jax.ShapeDtypeStruct((n, D_MODEL), F32),
                   jax.ShapeDtypeStruct((n, IN_PROJ), F32)],
        compiler_params=_params(("parallel",)),
        name="dense_in",
    )(x, w_in, w_out, g, b, w_mix)


def _dense_out_kernel(x_ref, orw_ref, ohg_ref, ocm_ref, wmix_ref, g2_ref, b2_ref,
                      w_in_ref, w_out_ref, g3_ref, b3_ref, y_ref):
    hg0, cm0 = RW_WIDTH, RW_WIDTH + HG_WIDTH
    mix = (jnp.dot(orw_ref[...], wmix_ref[:hg0, :], preferred_element_type=F32)
           + jnp.dot(ohg_ref[...], wmix_ref[hg0:cm0, :], preferred_element_type=F32)
           + jnp.dot(ocm_ref[...], wmix_ref[cm0:, :], preferred_element_type=F32))
    x2 = _layer_norm(ALPHA * x_ref[...] + mix, g2_ref[...], b2_ref[...])
    y_ref[...] = _layer_norm(ALPHA * x2 + 0.5 * _swiglu(x2.astype(BF16), w_in_ref, w_out_ref),
                             g3_ref[...], b3_ref[...])


def _dense_out(x, o_rw, o_hg, o_cm, layer, w_mix, g2, b2, w_in, w_out, g3, b3):
    n = x.shape[0]
    rows = min(DENSE_ROWS, n)
    row_block = lambda width: pl.BlockSpec((rows, width), lambda i: (i, 0))
    weights = (w_mix, g2, b2, w_in, w_out, g3, b3)
    return pl.pallas_call(
        _dense_out_kernel,
        grid=(n // rows,),
        in_specs=[row_block(D_MODEL), row_block(RW_WIDTH), row_block(HG_WIDTH), row_block(CM_WIDTH)]
                 + [_layer_resident(w, layer) for w in weights],
        out_specs=row_block(D_MODEL),
        out_shape=jax.ShapeDtypeStruct((n, D_MODEL), F32),
        compiler_params=_params(("parallel",)),
        name="dense_out",
    )(x, o_rw, o_hg, o_cm, *weights)


def _head_stack(x, head_masks):
    return jnp.concatenate([x * m for m in head_masks], axis=0)


def _rwkv_kernel(*refs, chunk, t_valid, n_seq, has_state, n_aliased):
    refs = refs[n_aliased:]
    if has_state:
        (zr_ref, zk_ref, zv_ref, zx_ref, shift_ref, s0_ref, mu_ref, w0_ref, ww2_ref, a0_ref, aw2_ref,
         gw2_ref, kk_ref, ka_ref, rk_ref, gng_ref, gnb_ref, o_ref, s_out_ref, s_scr, prev_scr) = refs
    else:
        (zr_ref, zk_ref, zv_ref, zx_ref, mu_ref, w0_ref, ww2_ref, a0_ref, aw2_ref,
         gw2_ref, kk_ref, ka_ref, rk_ref, gng_ref, gnb_ref, o_ref, s_out_ref, s_scr, prev_scr) = refs
    C, N, H, W, G = chunk, RW_DIM, RW_HEADS, RW_WIDTH, RW_GROUP
    R = zr_ref.shape[0]
    L = R // n_seq
    step = pl.program_id(1)

    @pl.when(step == 0)
    def _():
        for b in range(n_seq):
            if has_state:
                s_scr[b] = jnp.concatenate([s0_ref[b, h] for h in range(H)], axis=1)
                prev_scr[b:b + 1, :] = shift_ref[b]
            else:
                s_scr[b] = jnp.zeros((N, W), F32)
                prev_scr[b:b + 1, :] = jnp.zeros((1, RW_PROJ), F32)

    RB = min(RW_BATCH_GROUPS * G, R)
    n_batches = R // RB
    assert n_seq == 1 or n_batches == 1
    assert C == G or C == L
    brow = lax.broadcasted_iota(jnp.int32, (RB, 1), 0)
    lane = lax.broadcasted_iota(jnp.int32, (1, W), 1)
    head_masks = [((lane >= h * N) & (lane < (h + 1) * N)).astype(F32) for h in range(H)]
    hr = lax.broadcasted_iota(jnp.int32, (W, W), 0)
    hc = lax.broadcasted_iota(jnp.int32, (W, W), 1)
    head_ones = ((hr // N) == (hc // N)).astype(F32)
    elem = {}

    def prologue(bi):
        lo = bi * RB
        t_in = (brow + lo) % L
        valid = t_in < t_valid

        def load(ref, part):
            cols = slice(part * W, (part + 1) * W)
            z = ref[lo:lo + RB, :]
            if t_valid < L:
                z = jnp.where(valid, z, 0.0)
            if n_seq == 1:
                first = prev_scr[0:1, cols] if bi == 0 else ref[lo - 1:lo, :]
                prev = jnp.where(brow == 0, first, pltpu.roll(z, 1, axis=0))
            else:
                first = jnp.concatenate([jnp.broadcast_to(prev_scr[b:b + 1, cols], (L, W))
                                         for b in range(n_seq)], axis=0)
                prev = jnp.where(t_in == 0, first, pltpu.roll(z, 1, axis=0))
            for b in range(n_seq):
                last = b * L + t_valid - 1 - lo
                if 0 <= last < RB:
                    prev_scr[b:b + 1, cols] = z[last:last + 1, :]
            return z + (prev - z) * mu_ref[:, cols]

        r = load(zr_ref, 0)
        k = load(zk_ref, 1)
        v = load(zv_ref, 2)
        x4 = load(zx_ref, 3)

        w_pre = w0_ref[...] + _dot(jnp.tanh(x4), ww2_ref[...])
        yield
        nw = -w_pre
        softplus = jnp.maximum(nw, 0.0) + jnp.log(1.0 + jnp.exp(-jnp.abs(nw)))
        lw = -jnp.exp(-softplus - 0.5)
        a = jax.nn.sigmoid(a0_ref[...] + _dot(x4, aw2_ref[...]))
        yield
        gate = _dot(jax.nn.sigmoid(x4), gw2_ref[...])
        yield
        kk = k * kk_ref[...]
        k = k * (1.0 + (a - 1.0) * ka_ref[...])
        sums = _dot(jnp.concatenate([kk * kk, r * k * rk_ref[...]], axis=0), head_ones)
        yield
        kk = kk / jnp.maximum(jnp.sqrt(sums[:RB]), 1e-12)
        bonus = sums[RB:] * v
        if t_valid < L:
            lw = jnp.where(valid, lw, 0.0)
            kk = jnp.where(valid, kk, 0.0)
            k = jnp.where(valid, k, 0.0)
            v = jnp.where(valid, v, 0.0)
        cum = _chunk_cumsum(lw, C)
        yield
        g_in = jnp.exp(cum)
        g_inv = jnp.exp(-cum)
        elem[bi] = dict(a_hat=-kk * jnp.exp(cum - lw), b_chk=kk * a * g_inv, k_chk=k * g_inv,
                        r_hat=r * g_in, v=v, g_in=g_in, bonus=bonus, gate=gate)

    HG_ = H * G
    sr = lax.broadcasted_iota(jnp.int32, (HG_, HG_), 0)
    sc = lax.broadcasted_iota(jnp.int32, (HG_, HG_), 1)
    same_chunk = (sr // C) == (sc // C)
    strict = same_chunk & (sc < sr)
    incl = same_chunk & (sc <= sr)
    eye = (sr == sc).astype(F32)
    n_sub = G // C

    def gather(x, sub):
        if n_sub == 1:
            return x
        return jnp.concatenate([x[h * G + sub * C:h * G + (sub + 1) * C] for h in range(H)], axis=0)

    def scatter(pieces):
        if n_sub == 1:
            return pieces[0]
        return jnp.concatenate([pieces[sub][h * C:(h + 1) * C]
                                for h in range(H) for sub in range(n_sub)], axis=0)

    pre = {}
    gis = range(RB // G)

    def precompute(bi):
        e = elem[bi]
        a_st, b_st, k_st, r_st, v_st = {}, {}, {}, {}, {}
        for gi in gis:
            rows = slice(gi * G, (gi + 1) * G)
            a_st[gi] = _head_stack(e["a_hat"][rows], head_masks)
            b_st[gi] = _head_stack(e["b_chk"][rows], head_masks)
            k_st[gi] = _head_stack(e["k_chk"][rows], head_masks)
            r_st[gi] = _head_stack(e["r_hat"][rows], head_masks)
            v_st[gi] = jnp.concatenate([e["v"][rows, h * N:(h + 1) * N] for h in range(H)], axis=0)
        m_ab = {gi: jnp.where(strict, _dot(a_st[gi], b_st[gi], _NT), 0.0) for gi in gis}
        yield
        m_ak = {gi: jnp.where(strict, _dot(a_st[gi], k_st[gi], _NT), 0.0) for gi in gis}
        yield
        p_rb = {gi: jnp.where(incl, _dot(r_st[gi], b_st[gi], _NT), 0.0) for gi in gis}
        yield
        p_rk = {gi: jnp.where(incl, _dot(r_st[gi], k_st[gi], _NT), 0.0) for gi in gis}
        yield
        t_inv = {gi: eye + m_ab[gi] for gi in gis}
        power = dict(m_ab)
        span = 2
        while span < C:
            power = {gi: _dot(power[gi], power[gi]) for gi in gis}
            yield
            t_inv = {gi: t_inv[gi] + _dot(t_inv[gi], power[gi]) for gi in gis}
            yield
            span *= 2
        w_m = {gi: _dot(t_inv[gi], a_st[gi]) for gi in gis}
        yield
        mv = {gi: _dot(m_ak[gi], v_st[gi]) for gi in gis}
        yield
        u_m = {gi: _dot(t_inv[gi], mv[gi]) for gi in gis}
        yield
        vk = {gi: [_dot(gather(v_st[gi], sub), gather(k_st[gi], sub), _TN) for sub in range(n_sub)]
              for gi in gis}
        yield
        for gi in gis:
            pre[bi, gi] = dict(b_s=b_st[gi], r_s=r_st[gi], v_s=v_st[gi], p_rb=p_rb[gi], p_rk=p_rk[gi],
                               w_m=w_m[gi], u_m=u_m[gi], vk=vk[gi])

    states = [s_scr[b] for b in range(n_seq)]

    def chain(bi):
        e = elem[bi]
        outs = []
        for gi in gis:
            g = pre[bi, gi]
            subs = range(n_sub)
            seqs = [(bi * RB + gi * G + sub * C) // L for sub in subs]
            old = [states[b] for b in seqs]
            c_parts = [_dot(gather(g["w_m"], sub), old[sub], _NT) + gather(g["u_m"], sub)
                       for sub in subs]
            yield
            upd = [_dot(c_parts[sub], gather(g["b_s"], sub), _TN) for sub in subs]
            for sub in subs:
                last = gi * G + sub * C + C - 1
                states[seqs[sub]] = (old[sub] + upd[sub] + g["vk"][sub]) * e["g_in"][last:last + 1, :]
            yield
            rs_parts = [_dot(gather(g["r_s"], sub), old[sub], _NT) for sub in subs]
            yield
            o_s = scatter(rs_parts) + _dot(jnp.concatenate([g["p_rb"], g["p_rk"]], axis=1),
                                           jnp.concatenate([scatter(c_parts), g["v_s"]], axis=0))
            mu_o = jnp.mean(o_s, axis=-1, keepdims=True)
            oc = o_s - mu_o
            var_o = jnp.mean(oc * oc, axis=-1, keepdims=True)
            on = oc * lax.rsqrt(var_o + RW_GN_EPS)
            outs.append(jnp.concatenate([on[h * G:(h + 1) * G, :] for h in range(H)], axis=1))
            yield
        o = outs[0] if len(outs) == 1 else jnp.concatenate(outs, axis=0)
        o_ref[bi * RB:(bi + 1) * RB, :] = (
            (o * gng_ref[...] + gnb_ref[...] + e["bonus"]) * e["gate"]).astype(o_ref.dtype)

    def run_interleaved(*gens):
        live = list(gens)
        while live:
            for gen in list(live):
                if next(gen, StopIteration) is StopIteration:
                    live.remove(gen)

    run_interleaved(prologue(0))
    for bi in range(n_batches):
        run_interleaved(*([precompute(bi)]
                          + ([prologue(bi + 1)] if bi + 1 < n_batches else [])
                          + ([chain(bi - 1)] if bi > 0 else [])))
    run_interleaved(chain(n_batches - 1))
    for b in range(n_seq):
        s_scr[b] = states[b]

    @pl.when(step == pl.num_programs(1) - 1)
    def _():
        for b in range(n_seq):
            for h in range(H):
                s_out_ref[b, h] = states[b][:, h * N:(h + 1) * N]


def _seq_grid(n_rows, n_batch, n_seq, seq_rows):
    rows_per_seq = n_rows // n_batch
    time_steps = rows_per_seq // seq_rows
    assert n_seq == 1 or time_steps == 1
    return (n_batch // n_seq, time_steps), (lambda i, t: i * time_steps + t)


def _layer_state_call(kern, layer, states_out, state_shape, n_seq, **kwargs):
    tail = (0,) * (len(state_shape) - 1)
    layer_block = pl.BlockSpec((None, n_seq) + state_shape[1:], lambda i, t: (layer, i) + tail)
    aliased = states_out is not None

    def call(args, specs, o_spec, o_shape):
        if aliased:
            args = [states_out] + args
            specs = [pl.BlockSpec(memory_space=pl.ANY)] + specs
        return pl.pallas_call(
            functools.partial(kern, n_aliased=int(aliased)),
            in_specs=specs,
            out_specs=[o_spec, layer_block],
            out_shape=[o_shape, jax.ShapeDtypeStruct((DEPTH,) + state_shape, F32)],
            input_output_aliases={0: 1} if aliased else {},
            **kwargs,
        )(*args)

    return layer_block, call


def _rwkv(z, shift0, s0, states_out, layer, p, n_batch, t_valid, chunk, n_seq, seq_rows):
    W = RW_WIDTH
    has_state = s0 is not None
    grid, row_block = _seq_grid(z.shape[0], n_batch, n_seq, seq_rows)
    R = n_seq * seq_rows
    zcol = lambda j: pl.BlockSpec((R, W), lambda i, t, j=j: (row_block(i, t), j))
    weights = (p["mu"], p["w0"], p["w_w2"], p["a0"], p["a_w2"], p["g_w2"], p["k_k"], p["k_a"],
               p["r_k"], p["gn_g"], p["gn_b"])
    kern = functools.partial(_rwkv_kernel, chunk=chunk, t_valid=t_valid, n_seq=n_seq,
                             has_state=has_state)
    state_spec, call = _layer_state_call(
        kern, layer, states_out, (n_batch, RW_HEADS, RW_DIM, RW_DIM), n_seq,
        grid=grid,
        scratch_shapes=[pltpu.VMEM((n_seq, RW_DIM, W), F32), pltpu.VMEM((n_seq, RW_PROJ), F32)],
        compiler_params=_params(("parallel", "arbitrary")),
        name="rwkv7")
    args, specs = [z, z, z, z], [zcol(0), zcol(1), zcol(2), zcol(3)]
    if has_state:
        args += [shift0, s0]
        specs += [pl.BlockSpec((None, n_seq, 1, RW_PROJ), lambda i, t: (layer, i, 0, 0)), state_spec]
    args += list(weights)
    specs += [_resident(w.shape) for w in weights]
    return call(args, specs, pl.BlockSpec((R, W), lambda i, t: (row_block(i, t), 0)),
                jax.ShapeDtypeStruct((z.shape[0], W), BF16))


def _hgrn_kernel(*refs, chunk, t_valid, n_seq, has_state, n_aliased):
    refs = refs[n_aliased:]
    if has_state:
        zq_ref, zf_ref, zi_ref, zg_ref, s0_ref, lb_ref, ng_ref, o_ref, s_out_ref, s_scr = refs
    else:
        zq_ref, zf_ref, zi_ref, zg_ref, lb_ref, ng_ref, o_ref, s_out_ref, s_scr = refs
    C, N, H = chunk, HG_DIM, HG_HEADS
    R = zq_ref.shape[0]
    L = R // n_seq
    sub = min(HG_SUB, C)
    step = pl.program_id(1)

    @pl.when(step == 0)
    def _():
        for b in range(n_seq):
            for h in range(H):
                s_scr[b, h] = s0_ref[b, h].T if has_state else jnp.zeros((N, N), F32)

    zq = zq_ref[...]
    q = zq * jax.nn.sigmoid(zq)
    lb = lb_ref[...]
    f = lb + (1.0 - lb) * jax.nn.sigmoid(zf_ref[...])
    log_f = jnp.log(jnp.maximum(f, F_MIN))
    k = 1.0 - f
    v = zi_ref[...]
    zg = zg_ref[...]
    out_gate = ng_ref[...] * (zg * jax.nn.sigmoid(zg))
    if t_valid < L:
        valid = (lax.broadcasted_iota(jnp.int32, (R, 1), 0) % L) < t_valid
        log_f = jnp.where(valid, log_f, 0.0)
        k = jnp.where(valid, k, 0.0)

    cum_all = _chunk_cumsum(log_f, C)
    tr = lax.broadcasted_iota(jnp.int32, (C, C), 0)
    tc = lax.broadcasted_iota(jnp.int32, (C, C), 1)
    causal = tc <= tr

    units = [(c, h) for c in range(R // C) for h in range(H)]
    part = lambda x, c, h: x[c * C:(c + 1) * C, h * N:(h + 1) * N]
    cums = {u: part(cum_all, *u) for u in units}
    totals = {u: cums[u][C - 1:C, :] for u in units}
    scores = {}
    for u in units:
        cum, q_h, k_h = cums[u], part(q, *u), part(k, *u)
        score_rows = []
        for i in range(C // sub):
            blk = slice(i * sub, (i + 1) * sub)
            ref_row = cum[i * sub:i * sub + 1, :]
            q_hat = q_h[blk] * jnp.exp(cum[blk] - ref_row)
            k_hat = k_h * jnp.exp(jnp.minimum(ref_row - cum, EXP_CLAMP))
            score_rows.append(_dot(q_hat, k_hat, _NT))
        rows_ = score_rows[0] if len(score_rows) == 1 else jnp.concatenate(score_rows, axis=0)
        scores[u] = jnp.where(causal, rows_, 0.0)
    intra = {u: _dot(scores[u], part(v, *u)) for u in units}
    updates = {u: _dot(part(v, *u), part(k, *u) * jnp.exp(totals[u] - cums[u]), _TN) for u in units}

    states = [[s_scr[b, h] for h in range(H)] for b in range(n_seq)]
    inter = {}
    for u in units:
        c, h = u
        b = (c * C) // L
        inter[u] = _dot(part(q, *u) * jnp.exp(cums[u]), states[b][h], _NT)
        states[b][h] = states[b][h] * jnp.exp(totals[u]) + updates[u]
    out_rows = []
    for c in range(R // C):
        outs = []
        for h in range(H):
            o_h = intra[(c, h)] + inter[(c, h)]
            outs.append(o_h * lax.rsqrt(jnp.mean(o_h * o_h, axis=-1, keepdims=True) + RMS_EPS))
        out_rows.append(jnp.concatenate(outs, axis=1))
    o = out_rows[0] if len(out_rows) == 1 else jnp.concatenate(out_rows, axis=0)
    o_ref[...] = (o * out_gate).astype(o_ref.dtype)
    for b in range(n_seq):
        for h in range(H):
            s_scr[b, h] = states[b][h]

    @pl.when(step == pl.num_programs(1) - 1)
    def _():
        for b in range(n_seq):
            for h in range(H):
                s_out_ref[b, h] = states[b][h].T


def _hgrn(z, s0, states_out, layer, lb, norm_g, n_batch, t_valid, chunk, n_seq, seq_rows):
    W = HG_WIDTH
    first = RW_PROJ // W
    has_state = s0 is not None
    grid, row_block = _seq_grid(z.shape[0], n_batch, n_seq, seq_rows)
    R = n_seq * seq_rows
    zcol = lambda j: pl.BlockSpec((R, W), lambda i, t, j=j: (row_block(i, t), first + j))
    kern = functools.partial(_hgrn_kernel, chunk=chunk, t_valid=t_valid, n_seq=n_seq,
                             has_state=has_state)
    state_spec, call = _layer_state_call(
        kern, layer, states_out, (n_batch, HG_HEADS, HG_DIM, HG_DIM), n_seq,
        grid=grid,
        scratch_shapes=[pltpu.VMEM((n_seq, HG_HEADS, HG_DIM, HG_DIM), F32)],
        compiler_params=_params(("parallel", "arbitrary")),
        name="hgrn2")
    args, specs = [z, z, z, z], [zcol(0), zcol(1), zcol(2), zcol(3)]
    if has_state:
        args.append(s0)
        specs.append(state_spec)
    args += [lb, norm_g]
    specs += [_resident(lb.shape), _resident(norm_g.shape)]
    return call(args, specs, pl.BlockSpec((R, W), lambda i, t: (row_block(i, t), 0)),
                jax.ShapeDtypeStruct((z.shape[0], W), BF16))


def _gelu(x):
    return 0.5 * x * (1.0 + lax.erf(x * (2.0 ** -0.5)))


def _cm_kernel(zu_ref, zv_ref, ws_ref, bs_ref, g_ref, b_ref, o_ref, *v_out):
    H, N, W, C = CM_HEADS, CM_DIM, CM_WIDTH, CM_CHUNK
    u = _gelu(zu_ref[...])
    v = _gelu(zv_ref[...])
    hr = lax.broadcasted_iota(jnp.int32, (W, W), 0)
    hc = lax.broadcasted_iota(jnp.int32, (W, W), 1)
    head_mean = jnp.where((hr // N) == (hc // N), 1.0 / N, 0.0).astype(F32)
    vc = v - _dot(v, head_mean)
    var = _dot(vc * vc, head_mean)
    vn = vc * lax.rsqrt(var + LN_EPS) * g_ref[...] + b_ref[...]
    if v_out:
        v_out[0][...] = vn
    tr = lax.broadcasted_iota(jnp.int32, (C, C), 0)
    tc = lax.broadcasted_iota(jnp.int32, (C, C), 1)
    lane = lax.broadcasted_iota(jnp.int32, (1, W), 1)
    w_causal = [jnp.where(tc <= tr, ws_ref[h], 0.0).astype(BF16) for h in range(H)]
    head_cols = [(lane >= h * N) & (lane < (h + 1) * N) for h in range(H)]
    vb = vn.astype(BF16)
    mixed = []
    for c in range(zu_ref.shape[0] // C):
        v_c = vb[c * C:(c + 1) * C]
        acc = bs_ref[...]
        for h in range(H):
            acc = acc + jnp.dot(w_causal[h], jnp.where(head_cols[h], v_c, jnp.zeros_like(v_c)),
                                preferred_element_type=F32)
        mixed.append(acc)
    mixed = mixed[0] if len(mixed) == 1 else jnp.concatenate(mixed, axis=0)
    o_ref[...] = (u * mixed).astype(o_ref.dtype)


def _chunk_mlp(z, ws, bs_wide, ln_g, ln_b, want_v):
    n, W = z.shape[0], CM_WIDTH
    rows = min(CM_STEP_ROWS, n)
    first = (RW_PROJ + HG_PROJ) // W
    zcol = lambda j: pl.BlockSpec((rows, W), lambda i, j=j: (i, first + j))
    out_block = pl.BlockSpec((rows, W), lambda i: (i, 0))
    out_specs, out_shape = [out_block], [jax.ShapeDtypeStruct((n, W), BF16)]
    if want_v:
        out_specs.append(out_block)
        out_shape.append(jax.ShapeDtypeStruct((n, W), F32))
    return pl.pallas_call(
        _cm_kernel,
        grid=(n // rows,),
        in_specs=[zcol(0), zcol(1), _resident(ws.shape), _resident(bs_wide.shape),
                  _resident(ln_g.shape), _resident(ln_b.shape)],
        out_specs=out_specs,
        out_shape=out_shape,
        compiler_params=_params(("parallel",)),
        name="chunk_gmlp",
    )(z, z, ws, bs_wide, ln_g, ln_b)


def _dense_weights(p):
    vec = lambda a: a[:, None, :]
    out = {k: p[k].astype(BF16) for k in ("ffn1_w_in", "ffn1_w_out", "mix_w_in", "mix_w_out",
                                           "ffn2_w_in", "ffn2_w_out")}
    out.update({k: vec(p[k]) for k in ("ln1_g", "ln1_b", "ln2_g", "ln2_b", "ln3_g", "ln3_b")})
    return out


def _layer_weights(p, l, lb):
    row = lambda a: a.reshape(1, -1)
    lora_pad = lambda w, start: jnp.zeros((RW_WIDTH, RW_WIDTH), F32).at[start:start + w.shape[0]].set(w)
    return dict(
        rw=dict(mu=row(p["rw_mu"][l]), w0=row(p["rw_w0"][l]),
                w_w2=lora_pad(p["rw_w_w2"][l], 0), a0=row(p["rw_a0"][l]),
                a_w2=lora_pad(p["rw_a_w2"][l], RW_DECAY_LORA),
                g_w2=lora_pad(p["rw_g_w2"][l], RW_DECAY_LORA + RW_AAA_LORA),
                k_k=row(p["rw_k_k"][l]), k_a=row(p["rw_k_a"][l]), r_k=row(p["rw_r_k"][l]),
                gn_g=row(p["rw_gn_g"][l]), gn_b=row(p["rw_gn_b"][l])),
        hg_lb=row(lb[l]), hg_norm_g=row(p["hg_norm_g"][l]),
        cm_ws=p["cm_ws"][l],
        cm_bs=jnp.repeat(p["cm_bs"][l].T, CM_DIM, axis=1),
        cm_ln_g=row(p["cm_ln_g"][l]), cm_ln_b=row(p["cm_ln_b"][l]),
    )


def _short_chunk_mixing(ws, bs_wide, t_pad):
    reps = CM_CHUNK // t_pad
    eye = jnp.eye(reps, dtype=ws.dtype)
    ws_bd = jnp.stack([jnp.kron(eye, ws[h, :t_pad, :t_pad]) for h in range(CM_HEADS)])
    return ws_bd, jnp.tile(bs_wide[:t_pad], (reps, 1))


def _run_trunk(x, rw_s0, rw_shift0, hg_s0, dw, weights):
    B, T, _ = x.shape
    n = B * T
    has_state = rw_s0 is not None
    if has_state:
        rw_shift0 = rw_shift0[:, :, None, :]
    short = T < RW_CHUNK
    t_pad = -(-T // SUBLANES) * SUBLANES if short else T
    if short:
        rw_chunk, hg_chunk, n_seq, rw_rows, hg_rows = t_pad, t_pad, SAMPLE_STEP_SEQS, t_pad, t_pad
    else:
        rw_chunk, hg_chunk, n_seq = RW_CHUNK, HG_CHUNK, 1
        rw_rows, hg_rows = min(T, RW_STEP_ROWS), min(T, HG_STEP_ROWS)
    xf = x.reshape(n, D_MODEL)
    rw_states, hg_states, rw_shifts, cm_vs = None, None, [], []
    for l in range(DEPTH):
        w = weights[l]
        x1, z = _dense_in(xf, l, dw["ffn1_w_in"], dw["ffn1_w_out"], dw["ln1_g"], dw["ln1_b"],
                          dw["mix_w_in"])
        z3 = z.reshape(B, T, IN_PROJ)
        rw_shifts.append(z3[:, T - 1, :RW_PROJ])
        if t_pad != T:
            z = jnp.pad(z3, ((0, 0), (0, t_pad - T), (0, 0))).reshape(B * t_pad, IN_PROJ)
        o_rw, rw_states = _rwkv(z, rw_shift0, rw_s0, rw_states, l, w["rw"], B, min(T, rw_rows),
                                rw_chunk, n_seq, rw_rows)
        o_hg, hg_states = _hgrn(z, hg_s0, hg_states, l, w["hg_lb"], w["hg_norm_g"], B,
                                min(T, hg_rows), hg_chunk, n_seq, hg_rows)
        if short:
            cm_ws, cm_bs = _short_chunk_mixing(w["cm_ws"], w["cm_bs"], t_pad)
        else:
            cm_ws, cm_bs = w["cm_ws"], w["cm_bs"]
        cm_out = _chunk_mlp(z, cm_ws, cm_bs, w["cm_ln_g"], w["cm_ln_b"], want_v=has_state)
        trim = lambda o: o.reshape(B, t_pad, o.shape[-1])[:, :T].reshape(n, o.shape[-1])
        xf = _dense_out(x1, trim(o_rw), trim(o_hg), trim(cm_out[0]), l, dw["mix_w_out"],
                        dw["ln2_g"], dw["ln2_b"], dw["ffn2_w_in"], dw["ffn2_w_out"],
                        dw["ln3_g"], dw["ln3_b"])
        if has_state:
            cm_vs.append(cm_out[1].reshape(B, t_pad, CM_WIDTH)[:, :T])
    return (xf.reshape(B, T, D_MODEL), rw_states, jnp.stack(rw_shifts), hg_states,
            jnp.stack(cm_vs) if has_state else None)


def kernel(x_prompt, x_sample, state_rwkv, state_rwkv_shift, state_hgrn, ffn1_w_in, ffn1_w_out, ln1_g, ln1_b, mix_w_in, mix_w_out, ln2_g, ln2_b, rw_mu, rw_w0, rw_w_w2, rw_a0, rw_a_w2, rw_g_w2, rw_k_k, rw_k_a, rw_r_k, rw_gn_g, rw_gn_b, hg_lb_logits, hg_norm_g, cm_ws, cm_bs, cm_ln_g, cm_ln_b, ffn2_w_in, ffn2_w_out, ln3_g, ln3_b):
    p = dict(ffn1_w_in=ffn1_w_in, ffn1_w_out=ffn1_w_out, ln1_g=ln1_g, ln1_b=ln1_b,
             mix_w_in=mix_w_in, mix_w_out=mix_w_out, ln2_g=ln2_g, ln2_b=ln2_b,
             rw_mu=rw_mu, rw_w0=rw_w0, rw_w_w2=rw_w_w2, rw_a0=rw_a0, rw_a_w2=rw_a_w2,
             rw_g_w2=rw_g_w2, rw_k_k=rw_k_k, rw_k_a=rw_k_a, rw_r_k=rw_r_k,
             rw_gn_g=rw_gn_g, rw_gn_b=rw_gn_b, hg_norm_g=hg_norm_g,
             cm_ws=cm_ws, cm_bs=cm_bs, cm_ln_g=cm_ln_g, cm_ln_b=cm_ln_b,
             ffn2_w_in=ffn2_w_in, ffn2_w_out=ffn2_w_out, ln3_g=ln3_g, ln3_b=ln3_b)
    s = jax.nn.softmax(hg_lb_logits.astype(F32), axis=0)
    lb = jnp.cumsum(s, axis=0) - s[0]
    weights = [_layer_weights(p, l, lb) for l in range(DEPTH)]
    dw = _dense_weights(p)
    y_p, rw_s_p, rw_sh_p, hg_s_p, _ = _run_trunk(x_prompt, None, None, None, dw, weights)
    y_s, rw_s_s, rw_sh_s, hg_s_s, cm_v_s = _run_trunk(x_sample, state_rwkv, state_rwkv_shift,
                                                      state_hgrn, dw, weights)
    return (y_p, y_s, rw_s_p, rw_sh_p, hg_s_p, rw_s_s, rw_sh_s, hg_s_s, cm_v_s)
```

```python
import functools

import jax
import jax.numpy as jnp
from jax import lax
from jax.experimental import pallas as pl
from jax.experimental.pallas import tpu as pltpu

F32 = jnp.float32
BF16 = jnp.bfloat16

D_MODEL = 1024
DEPTH = 2
RW_HEADS, RW_DIM = 4, 64
RW_WIDTH = RW_HEADS * RW_DIM
RW_DECAY_LORA, RW_AAA_LORA, RW_GATE_LORA = 64, 64, 128
RW_PROJ = 3 * RW_WIDTH + RW_DECAY_LORA + RW_AAA_LORA + RW_GATE_LORA
RW_GN_EPS = 64e-5
HG_HEADS, HG_DIM = 4, 128
HG_WIDTH = HG_HEADS * HG_DIM
HG_PROJ = 4 * HG_WIDTH
RMS_EPS = 1e-6
F_MIN = 1e-30
CM_HEADS, CM_DIM = 4, 64
CM_WIDTH = CM_HEADS * CM_DIM
CM_CHUNK = 128
CM_PROJ = 2 * CM_WIDTH
MIX_WIDTH = RW_WIDTH + HG_WIDTH + CM_WIDTH
IN_PROJ = RW_PROJ + HG_PROJ + CM_PROJ
D_FF = 2816
LN_EPS = 1e-5
ALPHA = (2.0 * DEPTH) ** 0.25

VMEM_LIMIT_BYTES = 56 * 1024 * 1024
SUBLANES = 8
MXU_DIM = 256
DENSE_ROWS = 512
DENSE_SUB_ROWS = 256
FF_CHUNK = 1408
RW_CHUNK = 64
RW_GROUP = 64
HG_CHUNK = 64
HG_SUB = 16
EXP_CLAMP = 80.0
RW_BATCH_GROUPS = 4
RW_STEP_ROWS = 1024
HG_STEP_ROWS = 512
SAMPLE_STEP_SEQS = 8
CM_STEP_ROWS = 1024


def _params(semantics):
    return pltpu.CompilerParams(dimension_semantics=semantics,
                                vmem_limit_bytes=VMEM_LIMIT_BYTES)


def _resident(shape):
    nd = len(shape)
    return pl.BlockSpec(shape, lambda *_: (0,) * nd, pipeline_mode=pl.Buffered(1))


_NN = ((1,), (0,))
_NT = ((1,), (1,))
_TN = ((0,), (0,))


def _dot(a, b, dims=_NN):
    return lax.dot_general(a.astype(BF16), b.astype(BF16), (dims, ((), ())),
                           preferred_element_type=F32)


def _mask_dot(mask, x):
    m = mask.astype(BF16)
    hi = x.astype(BF16)
    r1 = x - hi.astype(F32)
    mid = r1.astype(BF16)
    lo = (r1 - mid.astype(F32)).astype(BF16)
    dot = lambda t: jnp.dot(m, t, preferred_element_type=F32)
    return dot(hi) + (dot(mid) + dot(lo))


def _chunk_cumsum(x, chunk):
    n = min(x.shape[0], MXU_DIM)
    r = lax.broadcasted_iota(jnp.int32, (n, n), 0)
    c = lax.broadcasted_iota(jnp.int32, (n, n), 1)
    mask = ((r // chunk) == (c // chunk)) & (c <= r)
    pieces = [_mask_dot(mask, x[lo:lo + n]) for lo in range(0, x.shape[0], n)]
    return pieces[0] if len(pieces) == 1 else jnp.concatenate(pieces, axis=0)


def _layer_norm(x, g, b):
    mu = jnp.mean(x, axis=-1, keepdims=True)
    xc = x - mu
    var = jnp.mean(xc * xc, axis=-1, keepdims=True)
    return xc * lax.rsqrt(var + LN_EPS) * g + b


def _swiglu(xb, w_in_ref, w_out_ref):
    acc = None
    for lo in range(0, D_FF, FF_CHUNK):
        gate = jnp.dot(xb, w_in_ref[:, lo:lo + FF_CHUNK], preferred_element_type=F32)
        up = jnp.dot(xb, w_in_ref[:, D_FF + lo:D_FF + lo + FF_CHUNK], preferred_element_type=F32)
        h = (gate * jax.nn.sigmoid(gate) * up).astype(BF16)
        part = jnp.dot(h, w_out_ref[lo:lo + FF_CHUNK, :], preferred_element_type=F32)
        acc = part if acc is None else acc + part
    return acc


def _sub_tiles(n_rows):
    sub = min(DENSE_SUB_ROWS, n_rows)
    return [slice(lo, lo + sub) for lo in range(0, n_rows, sub)]


def _dense_in_kernel(x_ref, w_in_ref, w_out_ref, g_ref, b_ref, w_mix_ref, x1_ref, z_ref):
    tiles = _sub_tiles(x_ref.shape[0])
    xs = [x_ref[t, :] for t in tiles]
    ffn = [_swiglu(x.astype(BF16), w_in_ref, w_out_ref) for x in xs]
    ys = [_layer_norm(ALPHA * x + 0.5 * f, g_ref[...], b_ref[...]) for x, f in zip(xs, ffn)]
    for t, y in zip(tiles, ys):
        x1_ref[t, :] = y
        z_ref[t, :] = jnp.dot(y.astype(BF16), w_mix_ref[...], preferred_element_type=F32)


def _layer_resident(stacked, layer):
    tail = (0,) * (stacked.ndim - 1)
    return pl.BlockSpec((None,) + stacked.shape[1:], lambda *_: (layer,) + tail,
                        pipeline_mode=pl.Buffered(1))


def _dense_in(x, layer, w_in, w_out, g, b, w_mix):
    n = x.shape[0]
    rows = min(DENSE_ROWS, n)
    row_block = lambda width: pl.BlockSpec((rows, width), lambda i: (i, 0))
    return pl.pallas_call(
        _dense_in_kernel,
        grid=(n // rows,),
        in_specs=[row_block(D_MODEL)] + [_layer_resident(w, layer) for w in (w_in, w_out, g, b, w_mix)],
        out_specs=[row_block(D_MODEL), row_block(IN_PROJ)],
        out_shape=[jax.ShapeDtypeStruct((n, D_MODEL), F32),
                   jax.ShapeDtypeStruct((n, IN_PROJ), F32)],
        compiler_params=_params(("parallel",)),
        name="dense_in",
    )(x, w_in, w_out, g, b, w_mix)


def _dense_out_kernel(x_ref, orw_ref, ohg_ref, ocm_ref, wmix_ref, g2_ref, b2_ref,
                      w_in_ref, w_out_ref, g3_ref, b3_ref, y_ref):
    hg0, cm0 = RW_WIDTH, RW_WIDTH + HG_WIDTH
    tiles = _sub_tiles(x_ref.shape[0])
    mix = [jnp.dot(orw_ref[t, :], wmix_ref[:hg0, :], preferred_element_type=F32)
           + jnp.dot(ohg_ref[t, :], wmix_ref[hg0:cm0, :], preferred_element_type=F32)
           + jnp.dot(ocm_ref[t, :], wmix_ref[cm0:, :], preferred_element_type=F32) for t in tiles]
    x2 = [_layer_norm(ALPHA * x_ref[t, :] + m, g2_ref[...], b2_ref[...]) for t, m in zip(tiles, mix)]
    ffn = [_swiglu(x.astype(BF16), w_in_ref, w_out_ref) for x in x2]
    for t, x, f in zip(tiles, x2, ffn):
        y_ref[t, :] = _layer_norm(ALPHA * x + 0.5 * f, g3_ref[...], b3_ref[...])


def _dense_out(x, o_rw, o_hg, o_cm, layer, w_mix, g2, b2, w_in, w_out, g3, b3):
    n = x.shape[0]
    rows = min(DENSE_ROWS, n)
    row_block = lambda width: pl.BlockSpec((rows, width), lambda i: (i, 0))
    weights = (w_mix, g2, b2, w_in, w_out, g3, b3)
    return pl.pallas_call(
        _dense_out_kernel,
        grid=(n // rows,),
        in_specs=[row_block(D_MODEL), row_block(RW_WIDTH), row_block(HG_WIDTH), row_block(CM_WIDTH)]
                 + [_layer_resident(w, layer) for w in weights],
        out_specs=row_block(D_MODEL),
        out_shape=jax.ShapeDtypeStruct((n, D_MODEL), F32),
        compiler_params=_params(("parallel",)),
        name="dense_out",
    )(x, o_rw, o_hg, o_cm, *weights)


def _head_stack(x, head_masks):
    return jnp.concatenate([x * m for m in head_masks], axis=0)


def _rwkv_kernel(*refs, chunk, t_valid, n_seq, has_state, n_aliased):
    refs = refs[n_aliased:]
    if has_state:
        (zr_ref, zk_ref, zv_ref, zx_ref, shift_ref, s0_ref, mu_ref, w0_ref, ww2_ref, a0_ref, aw2_ref,
         gw2_ref, kk_ref, ka_ref, rk_ref, gng_ref, gnb_ref, o_ref, s_out_ref, s_scr, prev_scr) = refs
    else:
        (zr_ref, zk_ref, zv_ref, zx_ref, mu_ref, w0_ref, ww2_ref, a0_ref, aw2_ref,
         gw2_ref, kk_ref, ka_ref, rk_ref, gng_ref, gnb_ref, o_ref, s_out_ref, s_scr, prev_scr) = refs
    C, N, H, W, G = chunk, RW_DIM, RW_HEADS, RW_WIDTH, RW_GROUP
    R = zr_ref.shape[0]
    L = R // n_seq
    step = pl.program_id(1)

    @pl.when(step == 0)
    def _():
        for b in range(n_seq):
            if has_state:
                s_scr[b] = jnp.concatenate([s0_ref[b, h] for h in range(H)], axis=1)
                prev_scr[b:b + 1, :] = shift_ref[b]
            else:
                s_scr[b] = jnp.zeros((N, W), F32)
                prev_scr[b:b + 1, :] = jnp.zeros((1, RW_PROJ), F32)

    RB = min(RW_BATCH_GROUPS * G, R)
    n_batches = R // RB
    assert n_seq == 1 or n_batches == 1
    assert C == G or C == L
    brow = lax.broadcasted_iota(jnp.int32, (RB, 1), 0)
    lane = lax.broadcasted_iota(jnp.int32, (1, W), 1)
    head_masks = [((lane >= h * N) & (lane < (h + 1) * N)).astype(F32) for h in range(H)]
    hr = lax.broadcasted_iota(jnp.int32, (W, W), 0)
    hc = lax.broadcasted_iota(jnp.int32, (W, W), 1)
    head_ones = ((hr // N) == (hc // N)).astype(F32)
    elem = {}

    def prologue(bi):
        lo = bi * RB
        t_in = (brow + lo) % L
        valid = t_in < t_valid

        def load(ref, part):
            cols = slice(part * W, (part + 1) * W)
            z = ref[lo:lo + RB, :]
            if t_valid < L:
                z = jnp.where(valid, z, 0.0)
            if n_seq == 1:
                first = prev_scr[0:1, cols] if bi == 0 else ref[lo - 1:lo, :]
                prev = jnp.where(brow == 0, first, pltpu.roll(z, 1, axis=0))
            else:
                first = jnp.concatenate([jnp.broadcast_to(prev_scr[b:b + 1, cols], (L, W))
                                         for b in range(n_seq)], axis=0)
                prev = jnp.where(t_in == 0, first, pltpu.roll(z, 1, axis=0))
            for b in range(n_seq):
                last = b * L + t_valid - 1 - lo
                if 0 <= last < RB:
                    prev_scr[b:b + 1, cols] = z[last:last + 1, :]
            return z + (prev - z) * mu_ref[:, cols]

        r = load(zr_ref, 0)
        k = load(zk_ref, 1)
        v = load(zv_ref, 2)
        x4 = load(zx_ref, 3)

        w_pre = w0_ref[...] + _dot(jnp.tanh(x4), ww2_ref[...])
        yield
        nw = -w_pre
        softplus = jnp.maximum(nw, 0.0) + jnp.log(1.0 + jnp.exp(-jnp.abs(nw)))
        lw = -jnp.exp(-softplus - 0.5)
        a = jax.nn.sigmoid(a0_ref[...] + _dot(x4, aw2_ref[...]))
        yield
        gate = _dot(jax.nn.sigmoid(x4), gw2_ref[...])
        yield
        kk = k * kk_ref[...]
        k = k * (1.0 + (a - 1.0) * ka_ref[...])
        sums = _dot(jnp.concatenate([kk * kk, r * k * rk_ref[...]], axis=0), head_ones)
        yield
        kk = kk / jnp.maximum(jnp.sqrt(sums[:RB]), 1e-12)
        bonus = sums[RB:] * v
        if t_valid < L:
            lw = jnp.where(valid, lw, 0.0)
            kk = jnp.where(valid, kk, 0.0)
            k = jnp.where(valid, k, 0.0)
            v = jnp.where(valid, v, 0.0)
        cum = _chunk_cumsum(lw, C)
        yield
        g_in = jnp.exp(cum)
        g_inv = jnp.exp(-cum)
        elem[bi] = dict(a_hat=-kk * jnp.exp(cum - lw), b_chk=kk * a * g_inv, k_chk=k * g_inv,
                        r_hat=r * g_in, v=v, g_in=g_in, bonus=bonus, gate=gate)

    HG_ = H * G
    sr = lax.broadcasted_iota(jnp.int32, (HG_, HG_), 0)
    sc = lax.broadcasted_iota(jnp.int32, (HG_, HG_), 1)
    same_chunk = (sr // C) == (sc // C)
    strict = same_chunk & (sc < sr)
    incl = same_chunk & (sc <= sr)
    eye = (sr == sc).astype(F32)
    n_sub = G // C

    def gather(x, sub):
        if n_sub == 1:
            return x
        return jnp.concatenate([x[h * G + sub * C:h * G + (sub + 1) * C] for h in range(H)], axis=0)

    def scatter(pieces):
        if n_sub == 1:
            return pieces[0]
        return jnp.concatenate([pieces[sub][h * C:(h + 1) * C]
                                for h in range(H) for sub in range(n_sub)], axis=0)

    pre = {}
    gis = range(RB // G)

    def precompute(bi):
        e = elem[bi]
        a_st, b_st, k_st, r_st, v_st = {}, {}, {}, {}, {}
        for gi in gis:
            rows = slice(gi * G, (gi + 1) * G)
            a_st[gi] = _head_stack(e["a_hat"][rows], head_masks)
            b_st[gi] = _head_stack(e["b_chk"][rows], head_masks)
            k_st[gi] = _head_stack(e["k_chk"][rows], head_masks)
            r_st[gi] = _head_stack(e["r_hat"][rows], head_masks)
            v_st[gi] = jnp.concatenate([e["v"][rows, h * N:(h + 1) * N] for h in range(H)], axis=0)
        m_ab = {gi: jnp.where(strict, _dot(a_st[gi], b_st[gi], _NT), 0.0) for gi in gis}
        yield
        m_ak = {gi: jnp.where(strict, _dot(a_st[gi], k_st[gi], _NT), 0.0) for gi in gis}
        yield
        p_rb = {gi: jnp.where(incl, _dot(r_st[gi], b_st[gi], _NT), 0.0) for gi in gis}
        yield
        p_rk = {gi: jnp.where(incl, _dot(r_st[gi], k_st[gi], _NT), 0.0) for gi in gis}
        yield
        t_inv = {gi: eye + m_ab[gi] for gi in gis}
        power = dict(m_ab)
        span = 2
        while span < C:
            power = {gi: _dot(power[gi], power[gi]) for gi in gis}
            yield
            t_inv = {gi: t_inv[gi] + _dot(t_inv[gi], power[gi]) for gi in gis}
            yield
            span *= 2
        w_m = {gi: _dot(t_inv[gi], a_st[gi]) for gi in gis}
        yield
        mv = {gi: _dot(m_ak[gi], v_st[gi]) for gi in gis}
        yield
        u_m = {gi: _dot(t_inv[gi], mv[gi]) for gi in gis}
        yield
        vk = {gi: [_dot(gather(v_st[gi], sub), gather(k_st[gi], sub), _TN) for sub in range(n_sub)]
              for gi in gis}
        yield
        for gi in gis:
            pre[bi, gi] = dict(b_s=b_st[gi], r_s=r_st[gi], v_s=v_st[gi], p_rb=p_rb[gi], p_rk=p_rk[gi],
                               w_m=w_m[gi], u_m=u_m[gi], vk=vk[gi])

    states = [s_scr[b] for b in range(n_seq)]

    def chain(bi):
        e = elem[bi]
        outs = []
        for gi in gis:
            g = pre[bi, gi]
            subs = range(n_sub)
            seqs = [(bi * RB + gi * G + sub * C) // L for sub in subs]
            old = [states[b] for b in seqs]
            c_parts = [_dot(gather(g["w_m"], sub), old[sub], _NT) + gather(g["u_m"], sub)
                       for sub in subs]
            yield
            upd = [_dot(c_parts[sub], gather(g["b_s"], sub), _TN) for sub in subs]
            for sub in subs:
                last = gi * G + sub * C + C - 1
                states[seqs[sub]] = (old[sub] + upd[sub] + g["vk"][sub]) * e["g_in"][last:last + 1, :]
            yield
            rs_parts = [_dot(gather(g["r_s"], sub), old[sub], _NT) for sub in subs]
            yield
            o_s = scatter(rs_parts) + _dot(jnp.concatenate([g["p_rb"], g["p_rk"]], axis=1),
                                           jnp.concatenate([scatter(c_parts), g["v_s"]], axis=0))
            mu_o = jnp.mean(o_s, axis=-1, keepdims=True)
            oc = o_s - mu_o
            var_o = jnp.mean(oc * oc, axis=-1, keepdims=True)
            on = oc * lax.rsqrt(var_o + RW_GN_EPS)
            outs.append(jnp.concatenate([on[h * G:(h + 1) * G, :] for h in range(H)], axis=1))
            yield
        o = outs[0] if len(outs) == 1 else jnp.concatenate(outs, axis=0)
        o_ref[bi * RB:(bi + 1) * RB, :] = (
            (o * gng_ref[...] + gnb_ref[...] + e["bonus"]) * e["gate"]).astype(o_ref.dtype)

    def run_interleaved(*gens):
        live = list(gens)
        while live:
            for gen in list(live):
                if next(gen, StopIteration) is StopIteration:
                    live.remove(gen)

    run_interleaved(prologue(0))
    for bi in range(n_batches):
        run_interleaved(*([precompute(bi)]
                          + ([prologue(bi + 1)] if bi + 1 < n_batches else [])
                          + ([chain(bi - 1)] if bi > 0 else [])))
    run_interleaved(chain(n_batches - 1))
    for b in range(n_seq):
        s_scr[b] = states[b]

    @pl.when(step == pl.num_programs(1) - 1)
    def _():
        for b in range(n_seq):
            for h in range(H):
                s_out_ref[b, h] = states[b][:, h * N:(h + 1) * N]


def _seq_grid(n_rows, n_batch, n_seq, seq_rows):
    rows_per_seq = n_rows // n_batch
    time_steps = rows_per_seq // seq_rows
    assert n_seq == 1 or time_steps == 1
    return (n_batch // n_seq, time_steps), (lambda i, t: i * time_steps + t)


def _layer_state_call(kern, layer, states_out, state_shape, n_seq, **kwargs):
    tail = (0,) * (len(state_shape) - 1)
    layer_block = pl.BlockSpec((None, n_seq) + state_shape[1:], lambda i, t: (layer, i) + tail)
    aliased = states_out is not None

    def call(args, specs, o_spec, o_shape):
        if aliased:
            args = [states_out] + args
            specs = [pl.BlockSpec(memory_space=pl.ANY)] + specs
        return pl.pallas_call(
            functools.partial(kern, n_aliased=int(aliased)),
            in_specs=specs,
            out_specs=[o_spec, layer_block],
            out_shape=[o_shape, jax.ShapeDtypeStruct((DEPTH,) + state_shape, F32)],
            input_output_aliases={0: 1} if aliased else {},
            **kwargs,
        )(*args)

    return layer_block, call


def _rwkv(z, shift0, s0, states_out, layer, p, n_batch, t_valid, chunk, n_seq, seq_rows):
    W = RW_WIDTH
    has_state = s0 is not None
    grid, row_block = _seq_grid(z.shape[0], n_batch, n_seq, seq_rows)
    R = n_seq * seq_rows
    zcol = lambda j: pl.BlockSpec((R, W), lambda i, t, j=j: (row_block(i, t), j))
    weights = (p["mu"], p["w0"], p["w_w2"], p["a0"], p["a_w2"], p["g_w2"], p["k_k"], p["k_a"],
               p["r_k"], p["gn_g"], p["gn_b"])
    kern = functools.partial(_rwkv_kernel, chunk=chunk, t_valid=t_valid, n_seq=n_seq,
                             has_state=has_state)
    state_spec, call = _layer_state_call(
        kern, layer, states_out, (n_batch, RW_HEADS, RW_DIM, RW_DIM), n_seq,
        grid=grid,
        scratch_shapes=[pltpu.VMEM((n_seq, RW_DIM, W), F32), pltpu.VMEM((n_seq, RW_PROJ), F32)],
        compiler_params=_params(("parallel", "arbitrary")),
        name="rwkv7")
    args, specs = [z, z, z, z], [zcol(0), zcol(1), zcol(2), zcol(3)]
    if has_state:
        args += [shift0, s0]
        specs += [pl.BlockSpec((None, n_seq, 1, RW_PROJ), lambda i, t: (layer, i, 0, 0)), state_spec]
    args += list(weights)
    specs += [_resident(w.shape) for w in weights]
    return call(args, specs, pl.BlockSpec((R, W), lambda i, t: (row_block(i, t), 0)),
                jax.ShapeDtypeStruct((z.shape[0], W), BF16))


def _hgrn_kernel(*refs, chunk, t_valid, n_seq, has_state, n_aliased):
    refs = refs[n_aliased:]
    if has_state:
        zq_ref, zf_ref, zi_ref, zg_ref, s0_ref, lb_ref, ng_ref, o_ref, s_out_ref, s_scr = refs
    else:
        zq_ref, zf_ref, zi_ref, zg_ref, lb_ref, ng_ref, o_ref, s_out_ref, s_scr = refs
    C, N, H = chunk, HG_DIM, HG_HEADS
    R = zq_ref.shape[0]
    L = R // n_seq
    sub = min(HG_SUB, C)
    step = pl.program_id(1)

    @pl.when(step == 0)
    def _():
        for b in range(n_seq):
            for h in range(H):
                s_scr[b, h] = s0_ref[b, h].T if has_state else jnp.zeros((N, N), F32)

    zq = zq_ref[...]
    q = zq * jax.nn.sigmoid(zq)
    lb = lb_ref[...]
    f = lb + (1.0 - lb) * jax.nn.sigmoid(zf_ref[...])
    log_f = jnp.log(jnp.maximum(f, F_MIN))
    k = 1.0 - f
    v = zi_ref[...]
    zg = zg_ref[...]
    out_gate = ng_ref[...] * (zg * jax.nn.sigmoid(zg))
    if t_valid < L:
        valid = (lax.broadcasted_iota(jnp.int32, (R, 1), 0) % L) < t_valid
        log_f = jnp.where(valid, log_f, 0.0)
        k = jnp.where(valid, k, 0.0)

    cum_all = _chunk_cumsum(log_f, C)
    tr = lax.broadcasted_iota(jnp.int32, (C, C), 0)
    tc = lax.broadcasted_iota(jnp.int32, (C, C), 1)
    causal = tc <= tr

    units = [(c, h) for c in range(R // C) for h in range(H)]
    part = lambda x, c, h: x[c * C:(c + 1) * C, h * N:(h + 1) * N]
    cums = {u: part(cum_all, *u) for u in units}
    totals = {u: cums[u][C - 1:C, :] for u in units}
    scores, intra, updates, inter = {}, {}, {}, {}
    n_blk = C // sub
    states = [[s_scr[b, h] for h in range(H)] for b in range(n_seq)]

    def score_stage(u):
        cum, q_h, k_h = cums[u], part(q, *u), part(k, *u)
        blks = [slice(i * sub, (i + 1) * sub) for i in range(n_blk)]
        refs = [cum[i * sub:i * sub + 1, :] for i in range(n_blk)]
        k_own = [k_h[blks[j]] * jnp.exp(jnp.minimum(refs[j] - cum[blks[j]], EXP_CLAMP))
                 for j in range(n_blk)]
        score_rows = []
        for i in range(n_blk):
            q_hat = q_h[blks[i]] * jnp.exp(cum[blks[i]] - refs[i])
            pieces = [k_own[j] * jnp.exp(refs[i] - refs[j]) for j in range(i)] + [k_own[i]]
            if i + 1 < n_blk:
                pieces.append(jnp.zeros(((n_blk - 1 - i) * sub, N), F32))
            k_hat = pieces[0] if len(pieces) == 1 else jnp.concatenate(pieces, axis=0)
            score_rows.append(_dot(q_hat, k_hat, _NT))
        rows_ = score_rows[0] if len(score_rows) == 1 else jnp.concatenate(score_rows, axis=0)
        scores[u] = jnp.where(causal, rows_, 0.0)

    def intra_stage(u):
        intra[u] = _dot(scores[u], part(v, *u))
        updates[u] = _dot(part(v, *u), part(k, *u) * jnp.exp(totals[u] - cums[u]), _TN)

    def state_stage(u):
        c, h = u
        b = (c * C) // L
        inter[u] = _dot(part(q, *u) * jnp.exp(cums[u]), states[b][h], _NT)
        states[b][h] = states[b][h] * jnp.exp(totals[u]) + updates[u]

    stages = (score_stage, intra_stage, state_stage)
    for idx in range(len(units) + len(stages) - 1):
        for lag, stage in enumerate(stages):
            if 0 <= idx - lag < len(units):
                stage(units[idx - lag])
    out_rows = []
    for c in range(R // C):
        outs = []
        for h in range(H):
            o_h = intra[(c, h)] + inter[(c, h)]
            outs.append(o_h * lax.rsqrt(jnp.mean(o_h * o_h, axis=-1, keepdims=True) + RMS_EPS))
        out_rows.append(jnp.concatenate(outs, axis=1))
    o = out_rows[0] if len(out_rows) == 1 else jnp.concatenate(out_rows, axis=0)
    o_ref[...] = (o * out_gate).astype(o_ref.dtype)
    for b in range(n_seq):
        for h in range(H):
            s_scr[b, h] = states[b][h]

    @pl.when(step == pl.num_programs(1) - 1)
    def _():
        for b in range(n_seq):
            for h in range(H):
                s_out_ref[b, h] = states[b][h].T


def _hgrn(z, s0, states_out, layer, lb, norm_g, n_batch, t_valid, chunk, n_seq, seq_rows):
    W = HG_WIDTH
    first = RW_PROJ // W
    has_state = s0 is not None
    grid, row_block = _seq_grid(z.shape[0], n_batch, n_seq, seq_rows)
    R = n_seq * seq_rows
    zcol = lambda j: pl.BlockSpec((R, W), lambda i, t, j=j: (row_block(i, t), first + j))
    kern = functools.partial(_hgrn_kernel, chunk=chunk, t_valid=t_valid, n_seq=n_seq,
                             has_state=has_state)
    state_spec, call = _layer_state_call(
        kern, layer, states_out, (n_batch, HG_HEADS, HG_DIM, HG_DIM), n_seq,
        grid=grid,
        scratch_shapes=[pltpu.VMEM((n_seq, HG_HEADS, HG_DIM, HG_DIM), F32)],
        compiler_params=_params(("parallel", "arbitrary")),
        name="hgrn2")
    args, specs = [z, z, z, z], [zcol(0), zcol(1), zcol(2), zcol(3)]
    if has_state:
        args.append(s0)
        specs.append(state_spec)
    args += [lb, norm_g]
    specs += [_resident(lb.shape), _resident(norm_g.shape)]
    return call(args, specs, pl.BlockSpec((R, W), lambda i, t: (row_block(i, t), 0)),
                jax.ShapeDtypeStruct((z.shape[0], W), BF16))


def _gelu(x):
    return 0.5 * x * (1.0 + lax.erf(x * (2.0 ** -0.5)))


def _cm_kernel(zu_ref, zv_ref, ws_ref, bs_ref, g_ref, b_ref, o_ref, *v_out):
    H, N, W, C = CM_HEADS, CM_DIM, CM_WIDTH, CM_CHUNK
    u = _gelu(zu_ref[...])
    v = _gelu(zv_ref[...])
    hr = lax.broadcasted_iota(jnp.int32, (W, W), 0)
    hc = lax.broadcasted_iota(jnp.int32, (W, W), 1)
    head_mean = jnp.where((hr // N) == (hc // N), 1.0 / N, 0.0).astype(F32)
    vc = v - _dot(v, head_mean)
    var = _dot(vc * vc, head_mean)
    vn = vc * lax.rsqrt(var + LN_EPS) * g_ref[...] + b_ref[...]
    if v_out:
        v_out[0][...] = vn
    tr = lax.broadcasted_iota(jnp.int32, (C, C), 0)
    tc = lax.broadcasted_iota(jnp.int32, (C, C), 1)
    lane = lax.broadcasted_iota(jnp.int32, (1, W), 1)
    w_causal = [jnp.where(tc <= tr, ws_ref[h], 0.0).astype(BF16) for h in range(H)]
    head_cols = [(lane >= h * N) & (lane < (h + 1) * N) for h in range(H)]
    vb = vn.astype(BF16)
    mixed = []
    for c in range(zu_ref.shape[0] // C):
        v_c = vb[c * C:(c + 1) * C]
        acc = bs_ref[...]
        for h in range(H):
            acc = acc + jnp.dot(w_causal[h], jnp.where(head_cols[h], v_c, jnp.zeros_like(v_c)),
                                preferred_element_type=F32)
        mixed.append(acc)
    mixed = mixed[0] if len(mixed) == 1 else jnp.concatenate(mixed, axis=0)
    o_ref[...] = (u * mixed).astype(o_ref.dtype)


def _chunk_mlp(z, ws, bs_wide, ln_g, ln_b, want_v):
    n, W = z.shape[0], CM_WIDTH
    rows = min(CM_STEP_ROWS, n)
    first = (RW_PROJ + HG_PROJ) // W
    zcol = lambda j: pl.BlockSpec((rows, W), lambda i, j=j: (i, first + j))
    out_block = pl.BlockSpec((rows, W), lambda i: (i, 0))
    out_specs, out_shape = [out_block], [jax.ShapeDtypeStruct((n, W), BF16)]
    if want_v:
        out_specs.append(out_block)
        out_shape.append(jax.ShapeDtypeStruct((n, W), F32))
    return pl.pallas_call(
        _cm_kernel,
        grid=(n // rows,),
        in_specs=[zcol(0), zcol(1), _resident(ws.shape), _resident(bs_wide.shape),
                  _resident(ln_g.shape), _resident(ln_b.shape)],
        out_specs=out_specs,
        out_shape=out_shape,
        compiler_params=_params(("parallel",)),
        name="chunk_gmlp",
    )(z, z, ws, bs_wide, ln_g, ln_b)


def _dense_weights(p):
    vec = lambda a: a[:, None, :]
    out = {k: p[k].astype(BF16) for k in ("ffn1_w_in", "ffn1_w_out", "mix_w_in", "mix_w_out",
                                           "ffn2_w_in", "ffn2_w_out")}
    out.update({k: vec(p[k]) for k in ("ln1_g", "ln1_b", "ln2_g", "ln2_b", "ln3_g", "ln3_b")})
    return out


def _layer_weights(p, l, lb):
    row = lambda a: a.reshape(1, -1)
    lora_pad = lambda w, start: jnp.zeros((RW_WIDTH, RW_WIDTH), F32).at[start:start + w.shape[0]].set(w)
    return dict(
        rw=dict(mu=row(p["rw_mu"][l]), w0=row(p["rw_w0"][l]),
                w_w2=lora_pad(p["rw_w_w2"][l], 0), a0=row(p["rw_a0"][l]),
                a_w2=lora_pad(p["rw_a_w2"][l], RW_DECAY_LORA),
                g_w2=lora_pad(p["rw_g_w2"][l], RW_DECAY_LORA + RW_AAA_LORA),
                k_k=row(p["rw_k_k"][l]), k_a=row(p["rw_k_a"][l]), r_k=row(p["rw_r_k"][l]),
                gn_g=row(p["rw_gn_g"][l]), gn_b=row(p["rw_gn_b"][l])),
        hg_lb=row(lb[l]), hg_norm_g=row(p["hg_norm_g"][l]),
        cm_ws=p["cm_ws"][l],
        cm_bs=jnp.repeat(p["cm_bs"][l].T, CM_DIM, axis=1),
        cm_ln_g=row(p["cm_ln_g"][l]), cm_ln_b=row(p["cm_ln_b"][l]),
    )


def _short_chunk_mixing(ws, bs_wide, t_pad):
    reps = CM_CHUNK // t_pad
    eye = jnp.eye(reps, dtype=ws.dtype)
    ws_bd = jnp.stack([jnp.kron(eye, ws[h, :t_pad, :t_pad]) for h in range(CM_HEADS)])
    return ws_bd, jnp.tile(bs_wide[:t_pad], (reps, 1))


def _run_trunk(x, rw_s0, rw_shift0, hg_s0, dw, weights):
    B, T, _ = x.shape
    n = B * T
    has_state = rw_s0 is not None
    if has_state:
        rw_shift0 = rw_shift0[:, :, None, :]
    short = T < RW_CHUNK
    t_pad = -(-T // SUBLANES) * SUBLANES if short else T
    if short:
        rw_chunk, hg_chunk, n_seq, rw_rows, hg_rows = t_pad, t_pad, SAMPLE_STEP_SEQS, t_pad, t_pad
    else:
        rw_chunk, hg_chunk, n_seq = RW_CHUNK, HG_CHUNK, 1
        rw_rows, hg_rows = min(T, RW_STEP_ROWS), min(T, HG_STEP_ROWS)
    xf = x.reshape(n, D_MODEL)
    rw_states, hg_states, rw_shifts, cm_vs = None, None, [], []
    for l in range(DEPTH):
        w = weights[l]
        x1, z = _dense_in(xf, l, dw["ffn1_w_in"], dw["ffn1_w_out"], dw["ln1_g"], dw["ln1_b"],
                          dw["mix_w_in"])
        z3 = z.reshape(B, T, IN_PROJ)
        rw_shifts.append(z3[:, T - 1, :RW_PROJ])
        if t_pad != T:
            z = jnp.pad(z3, ((0, 0), (0, t_pad - T), (0, 0))).reshape(B * t_pad, IN_PROJ)
        o_rw, rw_states = _rwkv(z, rw_shift0, rw_s0, rw_states, l, w["rw"], B, min(T, rw_rows),
                                rw_chunk, n_seq, rw_rows)
        o_hg, hg_states = _hgrn(z, hg_s0, hg_states, l, w["hg_lb"], w["hg_norm_g"], B,
                                min(T, hg_rows), hg_chunk, n_seq, hg_rows)
        if short:
            cm_ws, cm_bs = _short_chunk_mixing(w["cm_ws"], w["cm_bs"], t_pad)
        else:
            cm_ws, cm_bs = w["cm_ws"], w["cm_bs"]
        cm_out = _chunk_mlp(z, cm_ws, cm_bs, w["cm_ln_g"], w["cm_ln_b"], want_v=has_state)
        trim = lambda o: o.reshape(B, t_pad, o.shape[-1])[:, :T].reshape(n, o.shape[-1])
        xf = _dense_out(x1, trim(o_rw), trim(o_hg), trim(cm_out[0]), l, dw["mix_w_out"],
                        dw["ln2_g"], dw["ln2_b"], dw["ffn2_w_in"], dw["ffn2_w_out"],
                        dw["ln3_g"], dw["ln3_b"])
        if has_state:
            cm_vs.append(cm_out[1].reshape(B, t_pad, CM_WIDTH)[:, :T])
    return (xf.reshape(B, T, D_MODEL), rw_states, jnp.stack(rw_shifts), hg_states,
            jnp.stack(cm_vs) if has_state else None)


def kernel(x_prompt, x_sample, state_rwkv, state_rwkv_shift, state_hgrn, ffn1_w_in, ffn1_w_out, ln1_g, ln1_b, mix_w_in, mix_w_out, ln2_g, ln2_b, rw_mu, rw_w0, rw_w_w2, rw_a0, rw_a_w2, rw_g_w2, rw_k_k, rw_k_a, rw_r_k, rw_gn_g, rw_gn_b, hg_lb_logits, hg_norm_g, cm_ws, cm_bs, cm_ln_g, cm_ln_b, ffn2_w_in, ffn2_w_out, ln3_g, ln3_b):
    p = dict(ffn1_w_in=ffn1_w_in, ffn1_w_out=ffn1_w_out, ln1_g=ln1_g, ln1_b=ln1_b,
             mix_w_in=mix_w_in, mix_w_out=mix_w_out, ln2_g=ln2_g, ln2_b=ln2_b,
             rw_mu=rw_mu, rw_w0=rw_w0, rw_w_w2=rw_w_w2, rw_a0=rw_a0, rw_a_w2=rw_a_w2,
             rw_g_w2=rw_g_w2, rw_k_k=rw_k_k, rw_k_a=rw_k_a, rw_r_k=rw_r_k,
             rw_gn_g=rw_gn_g, rw_gn_b=rw_gn_b, hg_norm_g=hg_norm_g,
             cm_ws=cm_ws, cm_bs=cm_bs, cm_ln_g=cm_ln_g, cm_ln_b=cm_ln_b,
             ffn2_w_in=ffn2_w_in, ffn2_w_out=ffn2_w_out, ln3_g=ln3_g, ln3_b=ln3_b)
    s = jax.nn.softmax(hg_lb_logits.astype(F32), axis=0)
    lb = jnp.cumsum(s, axis=0) - s[0]
    weights = [_layer_weights(p, l, lb) for l in range(DEPTH)]
    dw = _dense_weights(p)
    y_p, rw_s_p, rw_sh_p, hg_s_p, _ = _run_trunk(x_prompt, None, None, None, dw, weights)
    y_s, rw_s_s, rw_sh_s, hg_s_s, cm_v_s = _run_trunk(x_sample, state_rwkv, state_rwkv_shift,
                                                      state_hgrn, dw, weights)
    return (y_p, y_s, rw_s_p, rw_sh_p, hg_s_p, rw_s_s, rw_sh_s, hg_s_s, cm_v_s)
```

```python
import functools

import jax
import jax.numpy as jnp
from jax import lax
from jax.experimental import pallas as pl
from jax.experimental.pallas import tpu as pltpu

F32 = jnp.float32
BF16 = jnp.bfloat16

D_MODEL = 1024
DEPTH = 2
RW_HEADS, RW_DIM = 4, 64
RW_WIDTH = RW_HEADS * RW_DIM
RW_DECAY_LORA, RW_AAA_LORA, RW_GATE_LORA = 64, 64, 128
RW_PROJ = 3 * RW_WIDTH + RW_DECAY_LORA + RW_AAA_LORA + RW_GATE_LORA
RW_GN_EPS = 64e-5
HG_HEADS, HG_DIM = 4, 128
HG_WIDTH = HG_HEADS * HG_DIM
HG_PROJ = 4 * HG_WIDTH
RMS_EPS = 1e-6
F_MIN = 1e-30
CM_HEADS, CM_DIM = 4, 64
CM_WIDTH = CM_HEADS * CM_DIM
CM_CHUNK = 128
CM_PROJ = 2 * CM_WIDTH
MIX_WIDTH = RW_WIDTH + HG_WIDTH + CM_WIDTH
IN_PROJ = RW_PROJ + HG_PROJ + CM_PROJ
D_FF = 2816
LN_EPS = 1e-5
ALPHA = (2.0 * DEPTH) ** 0.25

VMEM_LIMIT_BYTES = 56 * 1024 * 1024
SUBLANES = 8
MXU_DIM = 256
DENSE_ROWS = 512
DENSE_SUB_ROWS = 256
FF_CHUNK = 1408
RW_CHUNK = 64
RW_GROUP = 64
HG_CHUNK = 64
HG_SUB = 16
EXP_CLAMP = 80.0
RW_BATCH_GROUPS = 4
RW_STEP_ROWS = 1024
HG_STEP_ROWS = 512
SAMPLE_STEP_SEQS = 8
CM_STEP_ROWS = 1024


def _params(semantics):
    return pltpu.CompilerParams(dimension_semantics=semantics,
                                vmem_limit_bytes=VMEM_LIMIT_BYTES)


_NN = ((1,), (0,))
_NT = ((1,), (1,))
_TN = ((0,), (0,))


def _dot(a, b, dims=_NN):
    return lax.dot_general(a.astype(BF16), b.astype(BF16), (dims, ((), ())),
                           preferred_element_type=F32)


def _mask_dot(mask, x):
    m = mask.astype(BF16)
    hi = x.astype(BF16)
    r1 = x - hi.astype(F32)
    mid = r1.astype(BF16)
    lo = (r1 - mid.astype(F32)).astype(BF16)
    dot = lambda t: jnp.dot(m, t, preferred_element_type=F32)
    return dot(hi) + (dot(mid) + dot(lo))


def _chunk_cumsum(x, chunk):
    n = min(x.shape[0], MXU_DIM)
    r = lax.broadcasted_iota(jnp.int32, (n, n), 0)
    c = lax.broadcasted_iota(jnp.int32, (n, n), 1)
    mask = ((r // chunk) == (c // chunk)) & (c <= r)
    pieces = [_mask_dot(mask, x[lo:lo + n]) for lo in range(0, x.shape[0], n)]
    return pieces[0] if len(pieces) == 1 else jnp.concatenate(pieces, axis=0)


def _layer_norm(x, g, b):
    mu = jnp.mean(x, axis=-1, keepdims=True)
    xc = x - mu
    var = jnp.mean(xc * xc, axis=-1, keepdims=True)
    return xc * lax.rsqrt(var + LN_EPS) * g + b


def _swiglu(xb, w_in_ref, w_out_ref):
    acc = None
    for lo in range(0, D_FF, FF_CHUNK):
        gate = jnp.dot(xb, w_in_ref[:, lo:lo + FF_CHUNK], preferred_element_type=F32)
        up = jnp.dot(xb, w_in_ref[:, D_FF + lo:D_FF + lo + FF_CHUNK], preferred_element_type=F32)
        h = (gate * jax.nn.sigmoid(gate) * up).astype(BF16)
        part = jnp.dot(h, w_out_ref[lo:lo + FF_CHUNK, :], preferred_element_type=F32)
        acc = part if acc is None else acc + part
    return acc


def _sub_tiles(n_rows):
    sub = min(DENSE_SUB_ROWS, n_rows)
    return [slice(lo, lo + sub) for lo in range(0, n_rows, sub)]


def _dense_in_kernel(x_ref, w_in_ref, w_out_ref, g_ref, b_ref, w_mix_ref, x1_ref, z_ref):
    tiles = _sub_tiles(x_ref.shape[0])
    xs = [x_ref[t, :] for t in tiles]
    ffn = [_swiglu(x.astype(BF16), w_in_ref, w_out_ref) for x in xs]
    ys = [_layer_norm(ALPHA * x + 0.5 * f, g_ref[...], b_ref[...]) for x, f in zip(xs, ffn)]
    for t, y in zip(tiles, ys):
        x1_ref[t, :] = y
        z_ref[t, :] = jnp.dot(y.astype(BF16), w_mix_ref[...], preferred_element_type=F32)


def _layer_resident(stacked, layer):
    tail = (0,) * (stacked.ndim - 1)
    return pl.BlockSpec((None,) + stacked.shape[1:], lambda *_: (layer,) + tail,
                        pipeline_mode=pl.Buffered(1))


def _dense_in(x, layer, w_in, w_out, g, b, w_mix):
    n = x.shape[0]
    rows = min(DENSE_ROWS, n)
    row_block = lambda width: pl.BlockSpec((rows, width), lambda i: (i, 0))
    return pl.pallas_call(
        _dense_in_kernel,
        grid=(n // rows,),
        in_specs=[row_block(D_MODEL)] + [_layer_resident(w, layer) for w in (w_in, w_out, g, b, w_mix)],
        out_specs=[row_block(D_MODEL), row_block(IN_PROJ)],
        out_shape=[jax.ShapeDtypeStruct((n, D_MODEL), F32),
                   jax.ShapeDtypeStruct((n, IN_PROJ), F32)],
        compiler_params=_params(("parallel",)),
        name="dense_in",
    )(x, w_in, w_out, g, b, w_mix)


def _dense_out_kernel(x_ref, orw_ref, ohg_ref, ocm_ref, wmix_ref, g2_ref, b2_ref,
                      w_in_ref, w_out_ref, g3_ref, b3_ref, y_ref):
    hg0, cm0 = RW_WIDTH, RW_WIDTH + HG_WIDTH
    tiles = _sub_tiles(x_ref.shape[0])
    mix = [jnp.dot(orw_ref[t, :], wmix_ref[:hg0, :], preferred_element_type=F32)
           + jnp.dot(ohg_ref[t, :], wmix_ref[hg0:cm0, :], preferred_element_type=F32)
           + jnp.dot(ocm_ref[t, :], wmix_ref[cm0:, :], preferred_element_type=F32) for t in tiles]
    x2 = [_layer_norm(ALPHA * x_ref[t, :] + m, g2_ref[...], b2_ref[...]) for t, m in zip(tiles, mix)]
    ffn = [_swiglu(x.astype(BF16), w_in_ref, w_out_ref) for x in x2]
    for t, x, f in zip(tiles, x2, ffn):
        y_ref[t, :] = _layer_norm(ALPHA * x + 0.5 * f, g3_ref[...], b3_ref[...])


def _dense_out(x, o_rw, o_hg, o_cm, layer, w_mix, g2, b2, w_in, w_out, g3, b3):
    n = x.shape[0]
    rows = min(DENSE_ROWS, n)
    row_block = lambda width: pl.BlockSpec((rows, width), lambda i: (i, 0))
    weights = (w_mix, g2, b2, w_in, w_out, g3, b3)
    return pl.pallas_call(
        _dense_out_kernel,
        grid=(n // rows,),
        in_specs=[row_block(D_MODEL), row_block(RW_WIDTH), row_block(HG_WIDTH), row_block(CM_WIDTH)]
                 + [_layer_resident(w, layer) for w in weights],
        out_specs=row_block(D_MODEL),
        out_shape=jax.ShapeDtypeStruct((n, D_MODEL), F32),
        compiler_params=_params(("parallel",)),
        name="dense_out",
    )(x, o_rw, o_hg, o_cm, *weights)


def _head_stack(x, head_masks):
    return jnp.concatenate([x * m for m in head_masks], axis=0)


def _rwkv_kernel(*refs, chunk, t_valid, n_seq, has_state, layer):
    earlier_ref, refs = (refs[0], refs[1:]) if layer else (None, refs)
    if has_state:
        (zr_ref, zk_ref, zv_ref, zx_ref, shift_ref, s0_ref, mu_ref, w0_ref, ww2_ref, a0_ref, aw2_ref,
         gw2_ref, kk_ref, ka_ref, rk_ref, gng_ref, gnb_ref, o_ref, s_out_ref, s_scr, prev_scr) = refs
    else:
        (zr_ref, zk_ref, zv_ref, zx_ref, mu_ref, w0_ref, ww2_ref, a0_ref, aw2_ref,
         gw2_ref, kk_ref, ka_ref, rk_ref, gng_ref, gnb_ref, o_ref, s_out_ref, s_scr, prev_scr) = refs
    C, N, H, W, G = chunk, RW_DIM, RW_HEADS, RW_WIDTH, RW_GROUP
    R = zr_ref.shape[0]
    L = R // n_seq
    step = pl.program_id(1)

    @pl.when(step == 0)
    def _():
        for b in range(n_seq):
            if has_state:
                s_scr[b] = jnp.concatenate([s0_ref[b, h] for h in range(H)], axis=1)
                prev_scr[b:b + 1, :] = shift_ref[b]
            else:
                s_scr[b] = jnp.zeros((N, W), F32)
                prev_scr[b:b + 1, :] = jnp.zeros((1, RW_PROJ), F32)

    RB = min(RW_BATCH_GROUPS * G, R)
    n_batches = R // RB
    assert n_seq == 1 or n_batches == 1
    assert C == G or C == L
    brow = lax.broadcasted_iota(jnp.int32, (RB, 1), 0)
    lane = lax.broadcasted_iota(jnp.int32, (1, W), 1)
    head_masks = [((lane >= h * N) & (lane < (h + 1) * N)).astype(F32) for h in range(H)]
    hr = lax.broadcasted_iota(jnp.int32, (W, W), 0)
    hc = lax.broadcasted_iota(jnp.int32, (W, W), 1)
    head_ones = ((hr // N) == (hc // N)).astype(F32)
    elem = {}

    def prologue(bi):
        lo = bi * RB
        t_in = (brow + lo) % L
        valid = t_in < t_valid

        def load(ref, part):
            cols = slice(part * W, (part + 1) * W)
            z = ref[lo:lo + RB, :]
            if t_valid < L:
                z = jnp.where(valid, z, 0.0)
            if n_seq == 1:
                first = prev_scr[0:1, cols] if bi == 0 else ref[lo - 1:lo, :]
                prev = jnp.where(brow == 0, first, pltpu.roll(z, 1, axis=0))
            else:
                first = jnp.concatenate([jnp.broadcast_to(prev_scr[b:b + 1, cols], (L, W))
                                         for b in range(n_seq)], axis=0)
                prev = jnp.where(t_in == 0, first, pltpu.roll(z, 1, axis=0))
            for b in range(n_seq):
                last = b * L + t_valid - 1 - lo
                if 0 <= last < RB:
                    prev_scr[b:b + 1, cols] = z[last:last + 1, :]
            return z + (prev - z) * mu_ref[:, cols]

        r = load(zr_ref, 0)
        k = load(zk_ref, 1)
        v = load(zv_ref, 2)
        x4 = load(zx_ref, 3)

        w_pre = w0_ref[...] + _dot(jnp.tanh(x4), ww2_ref[...])
        yield
        nw = -w_pre
        softplus = jnp.maximum(nw, 0.0) + jnp.log(1.0 + jnp.exp(-jnp.abs(nw)))
        lw = -jnp.exp(-softplus - 0.5)
        a = jax.nn.sigmoid(a0_ref[...] + _dot(x4, aw2_ref[...]))
        yield
        gate = _dot(jax.nn.sigmoid(x4), gw2_ref[...])
        yield
        kk = k * kk_ref[...]
        k = k * (1.0 + (a - 1.0) * ka_ref[...])
        sums = _dot(jnp.concatenate([kk * kk, r * k * rk_ref[...]], axis=0), head_ones)
        yield
        kk = kk / jnp.maximum(jnp.sqrt(sums[:RB]), 1e-12)
        bonus = sums[RB:] * v
        if t_valid < L:
            lw = jnp.where(valid, lw, 0.0)
            kk = jnp.where(valid, kk, 0.0)
            k = jnp.where(valid, k, 0.0)
            v = jnp.where(valid, v, 0.0)
        cum = _chunk_cumsum(lw, C)
        yield
        g_in = jnp.exp(cum)
        g_inv = jnp.exp(-cum)
        elem[bi] = dict(a_hat=-kk * jnp.exp(cum - lw), b_chk=kk * a * g_inv, k_chk=k * g_inv,
                        r_hat=r * g_in, v=v, g_in=g_in, bonus=bonus, gate=gate)

    HG_ = H * G
    sr = lax.broadcasted_iota(jnp.int32, (HG_, HG_), 0)
    sc = lax.broadcasted_iota(jnp.int32, (HG_, HG_), 1)
    same_chunk = (sr // C) == (sc // C)
    strict = same_chunk & (sc < sr)
    incl = same_chunk & (sc <= sr)
    eye = (sr == sc).astype(F32)
    n_sub = G // C

    def gather(x, sub):
        if n_sub == 1:
            return x
        return jnp.concatenate([x[h * G + sub * C:h * G + (sub + 1) * C] for h in range(H)], axis=0)

    def scatter(pieces):
        if n_sub == 1:
            return pieces[0]
        return jnp.concatenate([pieces[sub][h * C:(h + 1) * C]
                                for h in range(H) for sub in range(n_sub)], axis=0)

    pre = {}
    gis = range(RB // G)

    def precompute(bi):
        e = elem[bi]
        a_st, b_st, k_st, r_st, v_st = {}, {}, {}, {}, {}
        for gi in gis:
            rows = slice(gi * G, (gi + 1) * G)
            a_st[gi] = _head_stack(e["a_hat"][rows], head_masks)
            b_st[gi] = _head_stack(e["b_chk"][rows], head_masks)
            k_st[gi] = _head_stack(e["k_chk"][rows], head_masks)
            r_st[gi] = _head_stack(e["r_hat"][rows], head_masks)
            v_st[gi] = jnp.concatenate([e["v"][rows, h * N:(h + 1) * N] for h in range(H)], axis=0)
        m_ab = {gi: jnp.where(strict, _dot(a_st[gi], b_st[gi], _NT), 0.0) for gi in gis}
        yield
        m_ak = {gi: jnp.where(strict, _dot(a_st[gi], k_st[gi], _NT), 0.0) for gi in gis}
        yield
        p_rb = {gi: jnp.where(incl, _dot(r_st[gi], b_st[gi], _NT), 0.0) for gi in gis}
        yield
        p_rk = {gi: jnp.where(incl, _dot(r_st[gi], k_st[gi], _NT), 0.0) for gi in gis}
        yield
        t_inv = {gi: eye + m_ab[gi] for gi in gis}
        power = dict(m_ab)
        span = 2
        while span < C:
            power = {gi: _dot(power[gi], power[gi]) for gi in gis}
            yield
            t_inv = {gi: t_inv[gi] + _dot(t_inv[gi], power[gi]) for gi in gis}
            yield
            span *= 2
        w_m = {gi: _dot(t_inv[gi], a_st[gi]) for gi in gis}
        yield
        mv = {gi: _dot(m_ak[gi], v_st[gi]) for gi in gis}
        yield
        u_m = {gi: _dot(t_inv[gi], mv[gi]) for gi in gis}
        yield
        vk = {gi: [_dot(gather(v_st[gi], sub), gather(k_st[gi], sub), _TN) for sub in range(n_sub)]
              for gi in gis}
        yield
        for gi in gis:
            pre[bi, gi] = dict(b_s=b_st[gi], r_s=r_st[gi], v_s=v_st[gi], p_rb=p_rb[gi], p_rk=p_rk[gi],
                               w_m=w_m[gi], u_m=u_m[gi], vk=vk[gi])

    states = [s_scr[b] for b in range(n_seq)]

    def chain(bi):
        e = elem[bi]
        outs = []
        for gi in gis:
            g = pre[bi, gi]
            subs = range(n_sub)
            seqs = [(bi * RB + gi * G + sub * C) // L for sub in subs]
            old = [states[b] for b in seqs]
            c_parts = [_dot(gather(g["w_m"], sub), old[sub], _NT) + gather(g["u_m"], sub)
                       for sub in subs]
            yield
            upd = [_dot(c_parts[sub], gather(g["b_s"], sub), _TN) for sub in subs]
            for sub in subs:
                last = gi * G + sub * C + C - 1
                states[seqs[sub]] = (old[sub] + upd[sub] + g["vk"][sub]) * e["g_in"][last:last + 1, :]
            yield
            rs_parts = [_dot(gather(g["r_s"], sub), old[sub], _NT) for sub in subs]
            yield
            o_s = scatter(rs_parts) + _dot(jnp.concatenate([g["p_rb"], g["p_rk"]], axis=1),
                                           jnp.concatenate([scatter(c_parts), g["v_s"]], axis=0))
            mu_o = jnp.mean(o_s, axis=-1, keepdims=True)
            oc = o_s - mu_o
            var_o = jnp.mean(oc * oc, axis=-1, keepdims=True)
            on = oc * lax.rsqrt(var_o + RW_GN_EPS)
            outs.append(jnp.concatenate([on[h * G:(h + 1) * G, :] for h in range(H)], axis=1))
            yield
        o = outs[0] if len(outs) == 1 else jnp.concatenate(outs, axis=0)
        o_ref[bi * RB:(bi + 1) * RB, :] = (
            (o * gng_ref[...] + gnb_ref[...] + e["bonus"]) * e["gate"]).astype(o_ref.dtype)

    def run_interleaved(*gens):
        live = list(gens)
        while live:
            for gen in list(live):
                if next(gen, StopIteration) is StopIteration:
                    live.remove(gen)

    run_interleaved(prologue(0))
    for bi in range(n_batches):
        run_interleaved(*([precompute(bi)]
                          + ([prologue(bi + 1)] if bi + 1 < n_batches else [])
                          + ([chain(bi - 1)] if bi > 0 else [])))
    run_interleaved(chain(n_batches - 1))
    for b in range(n_seq):
        s_scr[b] = states[b]

    @pl.when(step == pl.num_programs(1) - 1)
    def _():
        if layer:
            s_out_ref[:layer] = earlier_ref[...]
        for b in range(n_seq):
            for h in range(H):
                s_out_ref[layer, b, h] = states[b][:, h * N:(h + 1) * N]


def _seq_grid(n_rows, n_batch, n_seq, seq_rows):
    rows_per_seq = n_rows // n_batch
    time_steps = rows_per_seq // seq_rows
    assert n_seq == 1 or time_steps == 1
    return (n_batch // n_seq, time_steps), (lambda i, t: i * time_steps + t)


def _layer_state_call(kern, layer, earlier, state_shape, n_seq, **kwargs):
    tail = (0,) * (len(state_shape) - 1)
    block = lambda depth: pl.BlockSpec((depth, n_seq) + state_shape[1:], lambda i, t: (0, i) + tail)
    layer_block = pl.BlockSpec((None, n_seq) + state_shape[1:], lambda i, t: (layer, i) + tail)

    def call(args, specs, o_spec, o_shape):
        if layer:
            args = [earlier] + args
            specs = [block(layer)] + specs
        return pl.pallas_call(
            functools.partial(kern, layer=layer),
            in_specs=specs,
            out_specs=[o_spec, block(layer + 1)],
            out_shape=[o_shape, jax.ShapeDtypeStruct((layer + 1,) + state_shape, F32)],
            **kwargs,
        )(*args)

    return layer_block, call


def _rwkv(z, shift0, s0, states_out, layer, p, n_batch, t_valid, chunk, n_seq, seq_rows):
    W = RW_WIDTH
    has_state = s0 is not None
    grid, row_block = _seq_grid(z.shape[0], n_batch, n_seq, seq_rows)
    R = n_seq * seq_rows
    zcol = lambda j: pl.BlockSpec((R, W), lambda i, t, j=j: (row_block(i, t), j))
    weights = (p["mu"], p["w0"], p["w_w2"], p["a0"], p["a_w2"], p["g_w2"], p["k_k"], p["k_a"],
               p["r_k"], p["gn_g"], p["gn_b"])
    kern = functools.partial(_rwkv_kernel, chunk=chunk, t_valid=t_valid, n_seq=n_seq,
                             has_state=has_state)
    state_spec, call = _layer_state_call(
        kern, layer, states_out, (n_batch, RW_HEADS, RW_DIM, RW_DIM), n_seq,
        grid=grid,
        scratch_shapes=[pltpu.VMEM((n_seq, RW_DIM, W), F32), pltpu.VMEM((n_seq, RW_PROJ), F32)],
        compiler_params=_params(("parallel", "arbitrary")),
        name="rwkv7")
    args, specs = [z, z, z, z], [zcol(0), zcol(1), zcol(2), zcol(3)]
    if has_state:
        args += [shift0, s0]
        specs += [pl.BlockSpec((None, n_seq, 1, RW_PROJ), lambda i, t: (layer, i, 0, 0)), state_spec]
    args += list(weights)
    specs += [_layer_resident(w, layer) for w in weights]
    return call(args, specs, pl.BlockSpec((R, W), lambda i, t: (row_block(i, t), 0)),
                jax.ShapeDtypeStruct((z.shape[0], W), BF16))


def _hgrn_kernel(*refs, chunk, t_valid, n_seq, has_state, layer):
    earlier_ref, refs = (refs[0], refs[1:]) if layer else (None, refs)
    if has_state:
        zq_ref, zf_ref, zi_ref, zg_ref, s0_ref, lb_ref, ng_ref, o_ref, s_out_ref, s_scr = refs
    else:
        zq_ref, zf_ref, zi_ref, zg_ref, lb_ref, ng_ref, o_ref, s_out_ref, s_scr = refs
    C, N, H = chunk, HG_DIM, HG_HEADS
    R = zq_ref.shape[0]
    L = R // n_seq
    sub = min(HG_SUB, C)
    step = pl.program_id(1)

    @pl.when(step == 0)
    def _():
        for b in range(n_seq):
            for h in range(H):
                s_scr[b, h] = s0_ref[b, h].T if has_state else jnp.zeros((N, N), F32)

    zq = zq_ref[...]
    q = zq * jax.nn.sigmoid(zq)
    lb = lb_ref[...]
    f = lb + (1.0 - lb) * jax.nn.sigmoid(zf_ref[...])
    log_f = jnp.log(jnp.maximum(f, F_MIN))
    k = 1.0 - f
    v = zi_ref[...]
    zg = zg_ref[...]
    out_gate = ng_ref[...] * (zg * jax.nn.sigmoid(zg))
    if t_valid < L:
        valid = (lax.broadcasted_iota(jnp.int32, (R, 1), 0) % L) < t_valid
        log_f = jnp.where(valid, log_f, 0.0)
        k = jnp.where(valid, k, 0.0)

    cum_all = _chunk_cumsum(log_f, C)
    tr = lax.broadcasted_iota(jnp.int32, (C, C), 0)
    tc = lax.broadcasted_iota(jnp.int32, (C, C), 1)
    causal = tc <= tr

    units = [(c, h) for c in range(R // C) for h in range(H)]
    part = lambda x, c, h: x[c * C:(c + 1) * C, h * N:(h + 1) * N]
    cums = {u: part(cum_all, *u) for u in units}
    totals = {u: cums[u][C - 1:C, :] for u in units}
    scores, intra, updates, inter = {}, {}, {}, {}
    n_blk = C // sub
    states = [[s_scr[b, h] for h in range(H)] for b in range(n_seq)]

    def score_stage(u):
        cum, q_h, k_h = cums[u], part(q, *u), part(k, *u)
        blks = [slice(i * sub, (i + 1) * sub) for i in range(n_blk)]
        refs = [cum[i * sub:i * sub + 1, :] for i in range(n_blk)]
        k_own = [k_h[blks[j]] * jnp.exp(jnp.minimum(refs[j] - cum[blks[j]], EXP_CLAMP))
                 for j in range(n_blk)]
        score_rows = []
        for i in range(n_blk):
            q_hat = q_h[blks[i]] * jnp.exp(cum[blks[i]] - refs[i])
            pieces = [k_own[j] * jnp.exp(refs[i] - refs[j]) for j in range(i)] + [k_own[i]]
            if i + 1 < n_blk:
                pieces.append(jnp.zeros(((n_blk - 1 - i) * sub, N), F32))
            k_hat = pieces[0] if len(pieces) == 1 else jnp.concatenate(pieces, axis=0)
            score_rows.append(_dot(q_hat, k_hat, _NT))
        rows_ = score_rows[0] if len(score_rows) == 1 else jnp.concatenate(score_rows, axis=0)
        scores[u] = jnp.where(causal, rows_, 0.0)

    def intra_stage(u):
        intra[u] = _dot(scores[u], part(v, *u))
        updates[u] = _dot(part(v, *u), part(k, *u) * jnp.exp(totals[u] - cums[u]), _TN)

    def state_stage(u):
        c, h = u
        b = (c * C) // L
        inter[u] = _dot(part(q, *u) * jnp.exp(cums[u]), states[b][h], _NT)
        states[b][h] = states[b][h] * jnp.exp(totals[u]) + updates[u]

    stages = (score_stage, intra_stage, state_stage)
    for idx in range(len(units) + len(stages) - 1):
        for lag, stage in enumerate(stages):
            if 0 <= idx - lag < len(units):
                stage(units[idx - lag])
    out_rows = []
    for c in range(R // C):
        outs = []
        for h in range(H):
            o_h = intra[(c, h)] + inter[(c, h)]
            outs.append(o_h * lax.rsqrt(jnp.mean(o_h * o_h, axis=-1, keepdims=True) + RMS_EPS))
        out_rows.append(jnp.concatenate(outs, axis=1))
    o = out_rows[0] if len(out_rows) == 1 else jnp.concatenate(out_rows, axis=0)
    o_ref[...] = (o * out_gate).astype(o_ref.dtype)
    for b in range(n_seq):
        for h in range(H):
            s_scr[b, h] = states[b][h]

    @pl.when(step == pl.num_programs(1) - 1)
    def _():
        if layer:
            s_out_ref[:layer] = earlier_ref[...]
        for b in range(n_seq):
            for h in range(H):
                s_out_ref[layer, b, h] = states[b][h].T


def _hgrn(z, s0, states_out, layer, lb, norm_g, n_batch, t_valid, chunk, n_seq, seq_rows):
    W = HG_WIDTH
    first = RW_PROJ // W
    has_state = s0 is not None
    grid, row_block = _seq_grid(z.shape[0], n_batch, n_seq, seq_rows)
    R = n_seq * seq_rows
    zcol = lambda j: pl.BlockSpec((R, W), lambda i, t, j=j: (row_block(i, t), first + j))
    kern = functools.partial(_hgrn_kernel, chunk=chunk, t_valid=t_valid, n_seq=n_seq,
                             has_state=has_state)
    state_spec, call = _layer_state_call(
        kern, layer, states_out, (n_batch, HG_HEADS, HG_DIM, HG_DIM), n_seq,
        grid=grid,
        scratch_shapes=[pltpu.VMEM((n_seq, HG_HEADS, HG_DIM, HG_DIM), F32)],
        compiler_params=_params(("parallel", "arbitrary")),
        name="hgrn2")
    args, specs = [z, z, z, z], [zcol(0), zcol(1), zcol(2), zcol(3)]
    if has_state:
        args.append(s0)
        specs.append(state_spec)
    args += [lb, norm_g]
    specs += [_layer_resident(lb, layer), _layer_resident(norm_g, layer)]
    return call(args, specs, pl.BlockSpec((R, W), lambda i, t: (row_block(i, t), 0)),
                jax.ShapeDtypeStruct((z.shape[0], W), BF16))


def _gelu(x):
    return 0.5 * x * (1.0 + lax.erf(x * (2.0 ** -0.5)))


def _cm_kernel(zu_ref, zv_ref, ws_ref, bs_ref, g_ref, b_ref, o_ref, *v_out):
    H, N, W, C = CM_HEADS, CM_DIM, CM_WIDTH, CM_CHUNK
    u = _gelu(zu_ref[...])
    v = _gelu(zv_ref[...])
    hr = lax.broadcasted_iota(jnp.int32, (W, W), 0)
    hc = lax.broadcasted_iota(jnp.int32, (W, W), 1)
    head_mean = jnp.where((hr // N) == (hc // N), 1.0 / N, 0.0).astype(F32)
    vc = v - _dot(v, head_mean)
    var = _dot(vc * vc, head_mean)
    vn = vc * lax.rsqrt(var + LN_EPS) * g_ref[...] + b_ref[...]
    if v_out:
        v_out[0][...] = vn
    tr = lax.broadcasted_iota(jnp.int32, (C, C), 0)
    tc = lax.broadcasted_iota(jnp.int32, (C, C), 1)
    lane = lax.broadcasted_iota(jnp.int32, (1, W), 1)
    w_causal = [jnp.where(tc <= tr, ws_ref[h], 0.0).astype(BF16) for h in range(H)]
    head_cols = [(lane >= h * N) & (lane < (h + 1) * N) for h in range(H)]
    vb = vn.astype(BF16)
    mixed = []
    for c in range(zu_ref.shape[0] // C):
        v_c = vb[c * C:(c + 1) * C]
        acc = bs_ref[...]
        for h in range(H):
            acc = acc + jnp.dot(w_causal[h], jnp.where(head_cols[h], v_c, jnp.zeros_like(v_c)),
                                preferred_element_type=F32)
        mixed.append(acc)
    mixed = mixed[0] if len(mixed) == 1 else jnp.concatenate(mixed, axis=0)
    o_ref[...] = (u * mixed).astype(o_ref.dtype)


def _chunk_mlp(z, layer, ws, bs_wide, ln_g, ln_b, want_v):
    n, W = z.shape[0], CM_WIDTH
    rows = min(CM_STEP_ROWS, n)
    first = (RW_PROJ + HG_PROJ) // W
    zcol = lambda j: pl.BlockSpec((rows, W), lambda i, j=j: (i, first + j))
    out_block = pl.BlockSpec((rows, W), lambda i: (i, 0))
    out_specs, out_shape = [out_block], [jax.ShapeDtypeStruct((n, W), BF16)]
    if want_v:
        out_specs.append(out_block)
        out_shape.append(jax.ShapeDtypeStruct((n, W), F32))
    return pl.pallas_call(
        _cm_kernel,
        grid=(n // rows,),
        in_specs=[zcol(0), zcol(1)] + [_layer_resident(w, layer) for w in (ws, bs_wide, ln_g, ln_b)],
        out_specs=out_specs,
        out_shape=out_shape,
        compiler_params=_params(("parallel",)),
        name="chunk_gmlp",
    )(z, z, ws, bs_wide, ln_g, ln_b)


def _dense_weights(p):
    vec = lambda a: a[:, None, :]
    out = {k: p[k].astype(BF16) for k in ("ffn1_w_in", "ffn1_w_out", "mix_w_in", "mix_w_out",
                                           "ffn2_w_in", "ffn2_w_out")}
    out.update({k: vec(p[k]) for k in ("ln1_g", "ln1_b", "ln2_g", "ln2_b", "ln3_g", "ln3_b")})
    return out


def _mixer_weights(p, lb):
    vec = lambda a: a.reshape(DEPTH, 1, -1)
    lora_pad = lambda w, start: jnp.pad(w, ((0, 0), (start, RW_WIDTH - start - w.shape[1]), (0, 0)))
    return dict(
        rw=dict(mu=vec(p["rw_mu"]), w0=vec(p["rw_w0"]),
                w_w2=lora_pad(p["rw_w_w2"], 0), a0=vec(p["rw_a0"]),
                a_w2=lora_pad(p["rw_a_w2"], RW_DECAY_LORA),
                g_w2=lora_pad(p["rw_g_w2"], RW_DECAY_LORA + RW_AAA_LORA),
                k_k=vec(p["rw_k_k"]), k_a=vec(p["rw_k_a"]), r_k=vec(p["rw_r_k"]),
                gn_g=vec(p["rw_gn_g"]), gn_b=vec(p["rw_gn_b"])),
        hg_lb=vec(lb), hg_norm_g=vec(p["hg_norm_g"]),
        cm_ws=p["cm_ws"],
        cm_bs=jnp.repeat(jnp.swapaxes(p["cm_bs"], 1, 2), CM_DIM, axis=2),
        cm_ln_g=vec(p["cm_ln_g"]), cm_ln_b=vec(p["cm_ln_b"]),
    )


def _short_chunk_mixing(ws, bs_wide, t_pad):
    reps = CM_CHUNK // t_pad
    eye = jnp.eye(reps, dtype=ws.dtype)
    ws_bd = jnp.einsum("ab,lhij->lhaibj", eye, ws[:, :, :t_pad, :t_pad])
    return (ws_bd.reshape(DEPTH, CM_HEADS, CM_CHUNK, CM_CHUNK),
            jnp.tile(bs_wide[:, :t_pad], (1, reps, 1)))


def _run_trunk(x, rw_s0, rw_shift0, hg_s0, dw, w):
    B, T, _ = x.shape
    n = B * T
    has_state = rw_s0 is not None
    if has_state:
        rw_shift0 = rw_shift0[:, :, None, :]
    short = T < RW_CHUNK
    t_pad = -(-T // SUBLANES) * SUBLANES if short else T
    if short:
        rw_chunk, hg_chunk, n_seq, rw_rows, hg_rows = t_pad, t_pad, SAMPLE_STEP_SEQS, t_pad, t_pad
    else:
        rw_chunk, hg_chunk, n_seq = RW_CHUNK, HG_CHUNK, 1
        rw_rows, hg_rows = min(T, RW_STEP_ROWS), min(T, HG_STEP_ROWS)
    if short:
        cm_ws, cm_bs = _short_chunk_mixing(w["cm_ws"], w["cm_bs"], t_pad)
    else:
        cm_ws, cm_bs = w["cm_ws"], w["cm_bs"]
    xf = x.reshape(n, D_MODEL)
    rw_states, hg_states, rw_shifts, cm_vs = None, None, [], []
    for l in range(DEPTH):
        x1, z = _dense_in(xf, l, dw["ffn1_w_in"], dw["ffn1_w_out"], dw["ln1_g"], dw["ln1_b"],
                          dw["mix_w_in"])
        z3 = z.reshape(B, T, IN_PROJ)
        rw_shifts.append(z3[:, T - 1, :RW_PROJ])
        if t_pad != T:
            z = jnp.pad(z3, ((0, 0), (0, t_pad - T), (0, 0))).reshape(B * t_pad, IN_PROJ)
        o_rw, rw_states = _rwkv(z, rw_shift0, rw_s0, rw_states, l, w["rw"], B, min(T, rw_rows),
                                rw_chunk, n_seq, rw_rows)
        o_hg, hg_states = _hgrn(z, hg_s0, hg_states, l, w["hg_lb"], w["hg_norm_g"], B,
                                min(T, hg_rows), hg_chunk, n_seq, hg_rows)
        cm_out = _chunk_mlp(z, l, cm_ws, cm_bs, w["cm_ln_g"], w["cm_ln_b"], want_v=has_state)
        trim = lambda o: o.reshape(B, t_pad, o.shape[-1])[:, :T].reshape(n, o.shape[-1])
        xf = _dense_out(x1, trim(o_rw), trim(o_hg), trim(cm_out[0]), l, dw["mix_w_out"],
                        dw["ln2_g"], dw["ln2_b"], dw["ffn2_w_in"], dw["ffn2_w_out"],
                        dw["ln3_g"], dw["ln3_b"])
        if has_state:
            cm_vs.append(cm_out[1].reshape(B, t_pad, CM_WIDTH)[:, :T])
    return (xf.reshape(B, T, D_MODEL), rw_states, jnp.stack(rw_shifts), hg_states,
            jnp.stack(cm_vs) if has_state else None)


def kernel(x_prompt, x_sample, state_rwkv, state_rwkv_shift, state_hgrn, ffn1_w_in, ffn1_w_out, ln1_g, ln1_b, mix_w_in, mix_w_out, ln2_g, ln2_b, rw_mu, rw_w0, rw_w_w2, rw_a0, rw_a_w2, rw_g_w2, rw_k_k, rw_k_a, rw_r_k, rw_gn_g, rw_gn_b, hg_lb_logits, hg_norm_g, cm_ws, cm_bs, cm_ln_g, cm_ln_b, ffn2_w_in, ffn2_w_out, ln3_g, ln3_b):
    p = dict(ffn1_w_in=ffn1_w_in, ffn1_w_out=ffn1_w_out, ln1_g=ln1_g, ln1_b=ln1_b,
             mix_w_in=mix_w_in, mix_w_out=mix_w_out, ln2_g=ln2_g, ln2_b=ln2_b,
             rw_mu=rw_mu, rw_w0=rw_w0, rw_w_w2=rw_w_w2, rw_a0=rw_a0, rw_a_w2=rw_a_w2,
             rw_g_w2=rw_g_w2, rw_k_k=rw_k_k, rw_k_a=rw_k_a, rw_r_k=rw_r_k,
             rw_gn_g=rw_gn_g, rw_gn_b=rw_gn_b, hg_norm_g=hg_norm_g,
             cm_ws=cm_ws, cm_bs=cm_bs, cm_ln_g=cm_ln_g, cm_ln_b=cm_ln_b,
             ffn2_w_in=ffn2_w_in, ffn2_w_out=ffn2_w_out, ln3_g=ln3_g, ln3_b=ln3_b)
    s = jax.nn.softmax(hg_lb_logits.astype(F32), axis=0)
    lb = jnp.cumsum(s, axis=0) - s[0]
    weights = _mixer_weights(p, lb)
    dw = _dense_weights(p)
    y_p, rw_s_p, rw_sh_p, hg_s_p, _ = _run_trunk(x_prompt, None, None, None, dw, weights)
    y_s, rw_s_s, rw_sh_s, hg_s_s, cm_v_s = _run_trunk(x_sample, state_rwkv, state_rwkv_shift,
                                                      state_hgrn, dw, weights)
    return (y_p, y_s, rw_s_p, rw_sh_p, hg_s_p, rw_s_s, rw_sh_s, hg_s_s, cm_v_s)
```

```python
import functools

import jax
import jax.numpy as jnp
from jax import lax
from jax.experimental import pallas as pl
from jax.experimental.pallas import tpu as pltpu

F32 = jnp.float32
BF16 = jnp.bfloat16

D_MODEL = 1024
DEPTH = 2
RW_HEADS, RW_DIM = 4, 64
RW_WIDTH = RW_HEADS * RW_DIM
RW_DECAY_LORA, RW_AAA_LORA, RW_GATE_LORA = 64, 64, 128
RW_PROJ = 3 * RW_WIDTH + RW_DECAY_LORA + RW_AAA_LORA + RW_GATE_LORA
RW_GN_EPS = 64e-5
HG_HEADS, HG_DIM = 4, 128
HG_WIDTH = HG_HEADS * HG_DIM
HG_PROJ = 4 * HG_WIDTH
RMS_EPS = 1e-6
F_MIN = 1e-30
CM_HEADS, CM_DIM = 4, 64
CM_WIDTH = CM_HEADS * CM_DIM
CM_CHUNK = 128
CM_PROJ = 2 * CM_WIDTH
MIX_WIDTH = RW_WIDTH + HG_WIDTH + CM_WIDTH
IN_PROJ = RW_PROJ + HG_PROJ + CM_PROJ
D_FF = 2816
LN_EPS = 1e-5
ALPHA = (2.0 * DEPTH) ** 0.25

VMEM_LIMIT_BYTES = 56 * 1024 * 1024
SUBLANES = 8
MXU_DIM = 256
DENSE_ROWS = 512
DENSE_SUB_ROWS = 256
FF_CHUNK = 1408
RW_CHUNK = 64
RW_GROUP = 64
HG_CHUNK = 64
HG_SUB = 16
EXP_CLAMP = 80.0
RW_BATCH_GROUPS = 4
RW_STEP_ROWS = 1024
HG_STEP_ROWS = 512
SAMPLE_STEP_SEQS = 16
CM_STEP_ROWS = 1024


def _params(semantics):
    return pltpu.CompilerParams(dimension_semantics=semantics,
                                vmem_limit_bytes=VMEM_LIMIT_BYTES)


_NN = ((1,), (0,))
_NT = ((1,), (1,))
_TN = ((0,), (0,))


def _dot(a, b, dims=_NN):
    return lax.dot_general(a.astype(BF16), b.astype(BF16), (dims, ((), ())),
                           preferred_element_type=F32)


def _mask_dot(mask, x):
    m = mask.astype(BF16)
    hi = x.astype(BF16)
    r1 = x - hi.astype(F32)
    mid = r1.astype(BF16)
    lo = (r1 - mid.astype(F32)).astype(BF16)
    dot = lambda t: jnp.dot(m, t, preferred_element_type=F32)
    return dot(hi) + (dot(mid) + dot(lo))


def _chunk_cumsum(x, chunk):
    n = min(x.shape[0], MXU_DIM)
    r = lax.broadcasted_iota(jnp.int32, (n, n), 0)
    c = lax.broadcasted_iota(jnp.int32, (n, n), 1)
    mask = ((r // chunk) == (c // chunk)) & (c <= r)
    pieces = [_mask_dot(mask, x[lo:lo + n]) for lo in range(0, x.shape[0], n)]
    return pieces[0] if len(pieces) == 1 else jnp.concatenate(pieces, axis=0)


def _layer_norm(x, g, b):
    mu = jnp.mean(x, axis=-1, keepdims=True)
    xc = x - mu
    var = jnp.mean(xc * xc, axis=-1, keepdims=True)
    return xc * lax.rsqrt(var + LN_EPS) * g + b


def _swiglu(xb, w_in_ref, w_out_ref):
    acc = None
    for lo in range(0, D_FF, FF_CHUNK):
        gate = jnp.dot(xb, w_in_ref[:, lo:lo + FF_CHUNK], preferred_element_type=F32)
        up = jnp.dot(xb, w_in_ref[:, D_FF + lo:D_FF + lo + FF_CHUNK], preferred_element_type=F32)
        h = (gate * jax.nn.sigmoid(gate) * up).astype(BF16)
        part = jnp.dot(h, w_out_ref[lo:lo + FF_CHUNK, :], preferred_element_type=F32)
        acc = part if acc is None else acc + part
    return acc


def _sub_tiles(n_rows):
    sub = min(DENSE_SUB_ROWS, n_rows)
    return [slice(lo, lo + sub) for lo in range(0, n_rows, sub)]


def _layer_resident(stacked, layer):
    tail = (0,) * (stacked.ndim - 1)
    return pl.BlockSpec((None,) + stacked.shape[1:], lambda *_: (layer,) + tail,
                        pipeline_mode=pl.Buffered(1))


def _grouped_row_call(body, name, groups, in_widths, out_widths, weights, layer):
    rows = min(DENSE_ROWS, min(g[0].shape[0] for g in groups))
    steps = [g[0].shape[0] // rows for g in groups]
    starts = [sum(steps[:k]) for k in range(len(groups))]
    n_in, n_out, n_w = len(in_widths), len(out_widths), len(weights)

    def kern(*refs):
        ins, w_refs, outs = refs[:len(groups) * n_in], refs[len(groups) * n_in:][:n_w], refs[-len(groups) * n_out:]
        step = pl.program_id(0)
        for k in range(len(groups)):
            run = functools.partial(body, *ins[k * n_in:(k + 1) * n_in], *w_refs,
                                    *outs[k * n_out:(k + 1) * n_out])
            if len(groups) == 1:
                run()
            else:
                pl.when((step >= starts[k]) & (step < starts[k] + steps[k]))(run)

    def block(width, k):
        return pl.BlockSpec((rows, width), lambda i: (jnp.clip(i - starts[k], 0, steps[k] - 1), 0))

    outs = pl.pallas_call(
        kern,
        grid=(sum(steps),),
        in_specs=[block(wd, k) for k in range(len(groups)) for wd in in_widths]
                 + [_layer_resident(w, layer) for w in weights],
        out_specs=[block(wd, k) for k in range(len(groups)) for wd in out_widths],
        out_shape=[jax.ShapeDtypeStruct((g[0].shape[0], wd), F32) for g in groups for wd in out_widths],
        compiler_params=_params(("arbitrary",)),
        name=name,
    )(*[a for g in groups for a in g], *weights)
    return [tuple(outs[k * n_out:(k + 1) * n_out]) for k in range(len(groups))]


def _dense_in_body(x_ref, w_in_ref, w_out_ref, g_ref, b_ref, w_mix_ref, x1_ref, z_ref):
    tiles = _sub_tiles(x_ref.shape[0])
    xs = [x_ref[t, :] for t in tiles]
    ffn = [_swiglu(x.astype(BF16), w_in_ref, w_out_ref) for x in xs]
    ys = [_layer_norm(ALPHA * x + 0.5 * f, g_ref[...], b_ref[...]) for x, f in zip(xs, ffn)]
    for t, y in zip(tiles, ys):
        x1_ref[t, :] = y
        z_ref[t, :] = jnp.dot(y.astype(BF16), w_mix_ref[...], preferred_element_type=F32)


def _dense_in(xs, layer, w_in, w_out, g, b, w_mix):
    return [_grouped_row_call(_dense_in_body, "dense_in", [(x,)], (D_MODEL,), (D_MODEL, IN_PROJ),
                              (w_in, w_out, g, b, w_mix), layer)[0] for x in xs]


def _dense_out_body(x_ref, orw_ref, ohg_ref, ocm_ref, wmix_ref, g2_ref, b2_ref,
                    w_in_ref, w_out_ref, g3_ref, b3_ref, y_ref):
    hg0, cm0 = RW_WIDTH, RW_WIDTH + HG_WIDTH
    tiles = _sub_tiles(x_ref.shape[0])
    mix = [jnp.dot(orw_ref[t, :], wmix_ref[:hg0, :], preferred_element_type=F32)
           + jnp.dot(ohg_ref[t, :], wmix_ref[hg0:cm0, :], preferred_element_type=F32)
           + jnp.dot(ocm_ref[t, :], wmix_ref[cm0:, :], preferred_element_type=F32) for t in tiles]
    x2 = [_layer_norm(ALPHA * x_ref[t, :] + m, g2_ref[...], b2_ref[...]) for t, m in zip(tiles, mix)]
    ffn = [_swiglu(x.astype(BF16), w_in_ref, w_out_ref) for x in x2]
    for t, x, f in zip(tiles, x2, ffn):
        y_ref[t, :] = _layer_norm(ALPHA * x + 0.5 * f, g3_ref[...], b3_ref[...])


def _dense_out(groups, layer, w_mix, g2, b2, w_in, w_out, g3, b3):
    ys = _grouped_row_call(_dense_out_body, "dense_out", groups,
                           (D_MODEL, RW_WIDTH, HG_WIDTH, CM_WIDTH), (D_MODEL,),
                           (w_mix, g2, b2, w_in, w_out, g3, b3), layer)
    return [y for (y,) in ys]


def _head_stack(x, head_masks):
    return jnp.concatenate([x * m for m in head_masks], axis=0)


def _rwkv_kernel(*refs, chunk, t_valid, n_seq, has_state, multi_step, layer):
    earlier_ref, refs = (refs[0], refs[1:]) if layer else (None, refs)
    if multi_step:
        *refs, s_scr, prev_scr = refs
    if has_state:
        (zr_ref, zk_ref, zv_ref, zx_ref, shift_ref, s0_ref, mu_ref, w0_ref, ww2_ref, a0_ref, aw2_ref,
         gw2_ref, kk_ref, ka_ref, rk_ref, gng_ref, gnb_ref, o_ref, s_out_ref) = refs
    else:
        (zr_ref, zk_ref, zv_ref, zx_ref, mu_ref, w0_ref, ww2_ref, a0_ref, aw2_ref,
         gw2_ref, kk_ref, ka_ref, rk_ref, gng_ref, gnb_ref, o_ref, s_out_ref) = refs
    C, N, H, W, G = chunk, RW_DIM, RW_HEADS, RW_WIDTH, RW_GROUP
    R = zr_ref.shape[0]
    L = R // n_seq

    def initial_state(b):
        if has_state:
            return jnp.concatenate([s0_ref[b, h] for h in range(H)], axis=1)
        return jnp.zeros((N, W), F32)

    def initial_prev(b):
        return shift_ref[b] if has_state else jnp.zeros((1, RW_PROJ), F32)

    if multi_step:
        step = pl.program_id(1)

        @pl.when(step == 0)
        def _():
            for b in range(n_seq):
                s_scr[b] = initial_state(b)
                prev_scr[b:b + 1, :] = initial_prev(b)

        carried_state = lambda b: s_scr[b]
        carried_prev = lambda b, cols: prev_scr[b:b + 1, cols]
    else:
        carried_state = initial_state
        carried_prev = lambda b, cols: initial_prev(b)[:, cols]

    RB = min(RW_BATCH_GROUPS * G, R)
    n_batches = R // RB
    assert n_seq == 1 or n_batches == 1
    assert C == G or C == L
    brow = lax.broadcasted_iota(jnp.int32, (RB, 1), 0)
    lane = lax.broadcasted_iota(jnp.int32, (1, W), 1)
    head_masks = [((lane >= h * N) & (lane < (h + 1) * N)).astype(F32) for h in range(H)]
    hr = lax.broadcasted_iota(jnp.int32, (W, W), 0)
    hc = lax.broadcasted_iota(jnp.int32, (W, W), 1)
    head_ones = ((hr // N) == (hc // N)).astype(F32)
    elem = {}

    def prologue(bi):
        lo = bi * RB
        t_in = (brow + lo) % L
        valid = t_in < t_valid

        def load(ref, part):
            cols = slice(part * W, (part + 1) * W)
            z = ref[lo:lo + RB, :]
            if t_valid < L:
                z = jnp.where(valid, z, 0.0)
            if n_seq == 1:
                first = carried_prev(0, cols) if bi == 0 else ref[lo - 1:lo, :]
                prev = jnp.where(brow == 0, first, pltpu.roll(z, 1, axis=0))
            else:
                first = jnp.concatenate([jnp.broadcast_to(carried_prev(b, cols), (L, W))
                                         for b in range(n_seq)], axis=0)
                prev = jnp.where(t_in == 0, first, pltpu.roll(z, 1, axis=0))
            if multi_step:
                for b in range(n_seq):
                    last = b * L + t_valid - 1 - lo
                    if 0 <= last < RB:
                        prev_scr[b:b + 1, cols] = z[last:last + 1, :]
            return z + (prev - z) * mu_ref[:, cols]

        r = load(zr_ref, 0)
        k = load(zk_ref, 1)
        v = load(zv_ref, 2)
        x4 = load(zx_ref, 3)

        w_pre = w0_ref[...] + _dot(jnp.tanh(x4), ww2_ref[...])
        yield
        nw = -w_pre
        softplus = jnp.maximum(nw, 0.0) + jnp.log(1.0 + jnp.exp(-jnp.abs(nw)))
        lw = -jnp.exp(-softplus - 0.5)
        a = jax.nn.sigmoid(a0_ref[...] + _dot(x4, aw2_ref[...]))
        yield
        gate = _dot(jax.nn.sigmoid(x4), gw2_ref[...])
        yield
        kk = k * kk_ref[...]
        k = k * (1.0 + (a - 1.0) * ka_ref[...])
        sums = _dot(jnp.concatenate([kk * kk, r * k * rk_ref[...]], axis=0), head_ones)
        yield
        kk = kk / jnp.maximum(jnp.sqrt(sums[:RB]), 1e-12)
        bonus = sums[RB:] * v
        if t_valid < L:
            lw = jnp.where(valid, lw, 0.0)
            kk = jnp.where(valid, kk, 0.0)
            k = jnp.where(valid, k, 0.0)
            v = jnp.where(valid, v, 0.0)
        cum = _chunk_cumsum(lw, C)
        yield
        g_in = jnp.exp(cum)
        g_inv = jnp.exp(-cum)
        elem[bi] = dict(a_hat=-kk * jnp.exp(cum - lw), b_chk=kk * a * g_inv, k_chk=k * g_inv,
                        r_hat=r * g_in, v=v, g_in=g_in, bonus=bonus, gate=gate)

    HG_ = H * G
    sr = lax.broadcasted_iota(jnp.int32, (HG_, HG_), 0)
    sc = lax.broadcasted_iota(jnp.int32, (HG_, HG_), 1)
    same_chunk = (sr // C) == (sc // C)
    strict = same_chunk & (sc < sr)
    incl = same_chunk & (sc <= sr)
    eye = (sr == sc).astype(F32)
    n_sub = G // C

    def gather(x, sub):
        if n_sub == 1:
            return x
        return jnp.concatenate([x[h * G + sub * C:h * G + (sub + 1) * C] for h in range(H)], axis=0)

    def scatter(pieces):
        if n_sub == 1:
            return pieces[0]
        return jnp.concatenate([pieces[sub][h * C:(h + 1) * C]
                                for h in range(H) for sub in range(n_sub)], axis=0)

    pre = {}
    gis = range(RB // G)

    def precompute(bi):
        e = elem[bi]
        a_st, b_st, k_st, r_st, v_st = {}, {}, {}, {}, {}
        for gi in gis:
            rows = slice(gi * G, (gi + 1) * G)
            a_st[gi] = _head_stack(e["a_hat"][rows], head_masks)
            b_st[gi] = _head_stack(e["b_chk"][rows], head_masks)
            k_st[gi] = _head_stack(e["k_chk"][rows], head_masks)
            r_st[gi] = _head_stack(e["r_hat"][rows], head_masks)
            v_st[gi] = jnp.concatenate([e["v"][rows, h * N:(h + 1) * N] for h in range(H)], axis=0)
        m_ab = {gi: jnp.where(strict, _dot(a_st[gi], b_st[gi], _NT), 0.0) for gi in gis}
        yield
        m_ak = {gi: jnp.where(strict, _dot(a_st[gi], k_st[gi], _NT), 0.0) for gi in gis}
        yield
        p_rb = {gi: jnp.where(incl, _dot(r_st[gi], b_st[gi], _NT), 0.0) for gi in gis}
        yield
        p_rk = {gi: jnp.where(incl, _dot(r_st[gi], k_st[gi], _NT), 0.0) for gi in gis}
        yield
        t_inv = {gi: eye + m_ab[gi] for gi in gis}
        power = dict(m_ab)
        span = 2
        while span < C:
            power = {gi: _dot(power[gi], power[gi]) for gi in gis}
            yield
            t_inv = {gi: t_inv[gi] + _dot(t_inv[gi], power[gi]) for gi in gis}
            yield
            span *= 2
        w_m = {gi: _dot(t_inv[gi], a_st[gi]) for gi in gis}
        yield
        mv = {gi: _dot(m_ak[gi], v_st[gi]) for gi in gis}
        yield
        u_m = {gi: _dot(t_inv[gi], mv[gi]) for gi in gis}
        yield
        vk = {gi: [_dot(gather(v_st[gi], sub), gather(k_st[gi], sub), _TN) for sub in range(n_sub)]
              for gi in gis}
        yield
        for gi in gis:
            pre[bi, gi] = dict(b_s=b_st[gi], r_s=r_st[gi], v_s=v_st[gi], p_rb=p_rb[gi], p_rk=p_rk[gi],
                               w_m=w_m[gi], u_m=u_m[gi], vk=vk[gi])

    states = [carried_state(b) for b in range(n_seq)]

    def chain(bi):
        e = elem[bi]
        outs = []
        for gi in gis:
            g = pre[bi, gi]
            subs = range(n_sub)
            seqs = [(bi * RB + gi * G + sub * C) // L for sub in subs]
            old = [states[b] for b in seqs]
            c_parts = [_dot(gather(g["w_m"], sub), old[sub], _NT) + gather(g["u_m"], sub)
                       for sub in subs]
            yield
            upd = [_dot(c_parts[sub], gather(g["b_s"], sub), _TN) for sub in subs]
            for sub in subs:
                last = gi * G + sub * C + C - 1
                states[seqs[sub]] = (old[sub] + upd[sub] + g["vk"][sub]) * e["g_in"][last:last + 1, :]
            yield
            rs_parts = [_dot(gather(g["r_s"], sub), old[sub], _NT) for sub in subs]
            yield
            o_s = scatter(rs_parts) + _dot(jnp.concatenate([g["p_rb"], g["p_rk"]], axis=1),
                                           jnp.concatenate([scatter(c_parts), g["v_s"]], axis=0))
            mu_o = jnp.mean(o_s, axis=-1, keepdims=True)
            oc = o_s - mu_o
            var_o = jnp.mean(oc * oc, axis=-1, keepdims=True)
            on = oc * lax.rsqrt(var_o + RW_GN_EPS)
            outs.append(jnp.concatenate([on[h * G:(h + 1) * G, :] for h in range(H)], axis=1))
            yield
        o = outs[0] if len(outs) == 1 else jnp.concatenate(outs, axis=0)
        o_ref[bi * RB:(bi + 1) * RB, :] = (
            (o * gng_ref[...] + gnb_ref[...] + e["bonus"]) * e["gate"]).astype(o_ref.dtype)

    def run_interleaved(*gens):
        live = list(gens)
        while live:
            for gen in list(live):
                if next(gen, StopIteration) is StopIteration:
                    live.remove(gen)

    run_interleaved(prologue(0))
    for bi in range(n_batches):
        run_interleaved(*([precompute(bi)]
                          + ([prologue(bi + 1)] if bi + 1 < n_batches else [])
                          + ([chain(bi - 1)] if bi > 0 else [])))
    run_interleaved(chain(n_batches - 1))

    def write_final_states():
        if layer:
            s_out_ref[:layer] = earlier_ref[...]
        for b in range(n_seq):
            for h in range(H):
                s_out_ref[layer, b, h] = states[b][:, h * N:(h + 1) * N]

    if multi_step:
        for b in range(n_seq):
            s_scr[b] = states[b]
        pl.when(step == pl.num_programs(1) - 1)(write_final_states)
    else:
        write_final_states()


def _seq_grid(n_rows, n_batch, n_seq, seq_rows):
    rows_per_seq = n_rows // n_batch
    time_steps = rows_per_seq // seq_rows
    assert n_seq == 1 or time_steps == 1
    return (n_batch // n_seq, time_steps), (lambda i, t: i * time_steps + t)


def _layer_state_call(kern, layer, earlier, state_shape, n_seq, **kwargs):
    tail = (0,) * (len(state_shape) - 1)
    block = lambda depth: pl.BlockSpec((depth, n_seq) + state_shape[1:], lambda i, t: (0, i) + tail)
    layer_block = pl.BlockSpec((None, n_seq) + state_shape[1:], lambda i, t: (layer, i) + tail)

    def call(args, specs, o_spec, o_shape):
        if layer:
            args = [earlier] + args
            specs = [block(layer)] + specs
        return pl.pallas_call(
            functools.partial(kern, layer=layer),
            in_specs=specs,
            out_specs=[o_spec, block(layer + 1)],
            out_shape=[o_shape, jax.ShapeDtypeStruct((layer + 1,) + state_shape, F32)],
            **kwargs,
        )(*args)

    return layer_block, call


def _rwkv(z, shift0, s0, states_out, layer, p, n_batch, t_valid, chunk, n_seq, seq_rows):
    W = RW_WIDTH
    has_state = s0 is not None
    grid, row_block = _seq_grid(z.shape[0], n_batch, n_seq, seq_rows)
    R = n_seq * seq_rows
    zcol = lambda j: pl.BlockSpec((R, W), lambda i, t, j=j: (row_block(i, t), j))
    weights = (p["mu"], p["w0"], p["w_w2"], p["a0"], p["a_w2"], p["g_w2"], p["k_k"], p["k_a"],
               p["r_k"], p["gn_g"], p["gn_b"])
    multi_step = grid[1] > 1
    kern = functools.partial(_rwkv_kernel, chunk=chunk, t_valid=t_valid, n_seq=n_seq,
                             has_state=has_state, multi_step=multi_step)
    state_spec, call = _layer_state_call(
        kern, layer, states_out, (n_batch, RW_HEADS, RW_DIM, RW_DIM), n_seq,
        grid=grid,
        scratch_shapes=[pltpu.VMEM((n_seq, RW_DIM, W), F32),
                        pltpu.VMEM((n_seq, RW_PROJ), F32)] if multi_step else [],
        compiler_params=_params(("parallel", "arbitrary")),
        name="rwkv7")
    args, specs = [z, z, z, z], [zcol(0), zcol(1), zcol(2), zcol(3)]
    if has_state:
        args += [shift0, s0]
        specs += [pl.BlockSpec((None, n_seq, 1, RW_PROJ), lambda i, t: (layer, i, 0, 0)), state_spec]
    args += list(weights)
    specs += [_layer_resident(w, layer) for w in weights]
    return call(args, specs, pl.BlockSpec((R, W), lambda i, t: (row_block(i, t), 0)),
                jax.ShapeDtypeStruct((z.shape[0], W), BF16))


def _hgrn_kernel(*refs, chunk, t_valid, n_seq, has_state, multi_step, layer):
    earlier_ref, refs = (refs[0], refs[1:]) if layer else (None, refs)
    if multi_step:
        *refs, s_scr = refs
    if has_state:
        zq_ref, zf_ref, zi_ref, zg_ref, s0_ref, lb_ref, ng_ref, o_ref, s_out_ref = refs
    else:
        zq_ref, zf_ref, zi_ref, zg_ref, lb_ref, ng_ref, o_ref, s_out_ref = refs
    C, N, H = chunk, HG_DIM, HG_HEADS
    R = zq_ref.shape[0]
    L = R // n_seq
    sub = min(HG_SUB, C)

    def initial_state(b, h):
        return s0_ref[b, h].T if has_state else jnp.zeros((N, N), F32)

    if multi_step:
        step = pl.program_id(1)

        @pl.when(step == 0)
        def _():
            for b in range(n_seq):
                for h in range(H):
                    s_scr[b, h] = initial_state(b, h)

        carried_state = lambda b, h: s_scr[b, h]
    else:
        carried_state = initial_state

    zq = zq_ref[...]
    q = zq * jax.nn.sigmoid(zq)
    lb = lb_ref[...]
    f = lb + (1.0 - lb) * jax.nn.sigmoid(zf_ref[...])
    log_f = jnp.log(jnp.maximum(f, F_MIN))
    k = 1.0 - f
    v = zi_ref[...]
    zg = zg_ref[...]
    out_gate = ng_ref[...] * (zg * jax.nn.sigmoid(zg))
    if t_valid < L:
        valid = (lax.broadcasted_iota(jnp.int32, (R, 1), 0) % L) < t_valid
        log_f = jnp.where(valid, log_f, 0.0)
        k = jnp.where(valid, k, 0.0)

    cum_all = _chunk_cumsum(log_f, C)
    tr = lax.broadcasted_iota(jnp.int32, (C, C), 0)
    tc = lax.broadcasted_iota(jnp.int32, (C, C), 1)
    causal = tc <= tr

    units = [(c, h) for c in range(R // C) for h in range(H)]
    part = lambda x, c, h: x[c * C:(c + 1) * C, h * N:(h + 1) * N]
    cums = {u: part(cum_all, *u) for u in units}
    totals = {u: cums[u][C - 1:C, :] for u in units}
    scores, intra, updates, inter = {}, {}, {}, {}
    n_blk = C // sub
    states = [[carried_state(b, h) for h in range(H)] for b in range(n_seq)]

    def score_stage(u):
        cum, q_h, k_h = cums[u], part(q, *u), part(k, *u)
        blks = [slice(i * sub, (i + 1) * sub) for i in range(n_blk)]
        refs = [cum[i * sub:i * sub + 1, :] for i in range(n_blk)]
        k_own = [k_h[blks[j]] * jnp.exp(jnp.minimum(refs[j] - cum[blks[j]], EXP_CLAMP))
                 for j in range(n_blk)]
        score_rows = []
        for i in range(n_blk):
            q_hat = q_h[blks[i]] * jnp.exp(cum[blks[i]] - refs[i])
            pieces = [k_own[j] * jnp.exp(refs[i] - refs[j]) for j in range(i)] + [k_own[i]]
            if i + 1 < n_blk:
                pieces.append(jnp.zeros(((n_blk - 1 - i) * sub, N), F32))
            k_hat = pieces[0] if len(pieces) == 1 else jnp.concatenate(pieces, axis=0)
            score_rows.append(_dot(q_hat, k_hat, _NT))
        rows_ = score_rows[0] if len(score_rows) == 1 else jnp.concatenate(score_rows, axis=0)
        scores[u] = jnp.where(causal, rows_, 0.0)

    def intra_stage(u):
        intra[u] = _dot(scores[u], part(v, *u))
        updates[u] = _dot(part(v, *u), part(k, *u) * jnp.exp(totals[u] - cums[u]), _TN)

    def state_stage(u):
        c, h = u
        b = (c * C) // L
        inter[u] = _dot(part(q, *u) * jnp.exp(cums[u]), states[b][h], _NT)
        states[b][h] = states[b][h] * jnp.exp(totals[u]) + updates[u]

    stages = (score_stage, intra_stage, state_stage)
    for idx in range(len(units) + len(stages) - 1):
        for lag, stage in enumerate(stages):
            if 0 <= idx - lag < len(units):
                stage(units[idx - lag])
    out_rows = []
    for c in range(R // C):
        outs = []
        for h in range(H):
            o_h = intra[(c, h)] + inter[(c, h)]
            outs.append(o_h * lax.rsqrt(jnp.mean(o_h * o_h, axis=-1, keepdims=True) + RMS_EPS))
        out_rows.append(jnp.concatenate(outs, axis=1))
    o = out_rows[0] if len(out_rows) == 1 else jnp.concatenate(out_rows, axis=0)
    o_ref[...] = (o * out_gate).astype(o_ref.dtype)

    def write_final_states():
        if layer:
            s_out_ref[:layer] = earlier_ref[...]
        for b in range(n_seq):
            for h in range(H):
                s_out_ref[layer, b, h] = states[b][h].T

    if multi_step:
        for b in range(n_seq):
            for h in range(H):
                s_scr[b, h] = states[b][h]
        pl.when(step == pl.num_programs(1) - 1)(write_final_states)
    else:
        write_final_states()


def _hgrn(z, s0, states_out, layer, lb, norm_g, n_batch, t_valid, chunk, n_seq, seq_rows):
    W = HG_WIDTH
    first = RW_PROJ // W
    has_state = s0 is not None
    grid, row_block = _seq_grid(z.shape[0], n_batch, n_seq, seq_rows)
    R = n_seq * seq_rows
    zcol = lambda j: pl.BlockSpec((R, W), lambda i, t, j=j: (row_block(i, t), first + j))
    multi_step = grid[1] > 1
    kern = functools.partial(_hgrn_kernel, chunk=chunk, t_valid=t_valid, n_seq=n_seq,
                             has_state=has_state, multi_step=multi_step)
    state_spec, call = _layer_state_call(
        kern, layer, states_out, (n_batch, HG_HEADS, HG_DIM, HG_DIM), n_seq,
        grid=grid,
        scratch_shapes=[pltpu.VMEM((n_seq, HG_HEADS, HG_DIM, HG_DIM), F32)] if multi_step else [],
        compiler_params=_params(("parallel", "arbitrary")),
        name="hgrn2")
    args, specs = [z, z, z, z], [zcol(0), zcol(1), zcol(2), zcol(3)]
    if has_state:
        args.append(s0)
        specs.append(state_spec)
    args += [lb, norm_g]
    specs += [_layer_resident(lb, layer), _layer_resident(norm_g, layer)]
    return call(args, specs, pl.BlockSpec((R, W), lambda i, t: (row_block(i, t), 0)),
                jax.ShapeDtypeStruct((z.shape[0], W), BF16))


def _gelu(x):
    return 0.5 * x * (1.0 + lax.erf(x * (2.0 ** -0.5)))


def _cm_kernel(zu_ref, zv_ref, ws_ref, bs_ref, g_ref, b_ref, o_ref, *v_out):
    H, N, W, C = CM_HEADS, CM_DIM, CM_WIDTH, CM_CHUNK
    u = _gelu(zu_ref[...])
    v = _gelu(zv_ref[...])
    hr = lax.broadcasted_iota(jnp.int32, (W, W), 0)
    hc = lax.broadcasted_iota(jnp.int32, (W, W), 1)
    head_mean = jnp.where((hr // N) == (hc // N), 1.0 / N, 0.0).astype(F32)
    vc = v - _dot(v, head_mean)
    var = _dot(vc * vc, head_mean)
    vn = vc * lax.rsqrt(var + LN_EPS) * g_ref[...] + b_ref[...]
    if v_out:
        v_out[0][...] = vn
    tr = lax.broadcasted_iota(jnp.int32, (C, C), 0)
    tc = lax.broadcasted_iota(jnp.int32, (C, C), 1)
    lane = lax.broadcasted_iota(jnp.int32, (1, W), 1)
    w_causal = [jnp.where(tc <= tr, ws_ref[h], 0.0).astype(BF16) for h in range(H)]
    head_cols = [(lane >= h * N) & (lane < (h + 1) * N) for h in range(H)]
    vb = vn.astype(BF16)
    mixed = []
    for c in range(zu_ref.shape[0] // C):
        v_c = vb[c * C:(c + 1) * C]
        acc = bs_ref[...]
        for h in range(H):
            acc = acc + jnp.dot(w_causal[h], jnp.where(head_cols[h], v_c, jnp.zeros_like(v_c)),
                                preferred_element_type=F32)
        mixed.append(acc)
    mixed = mixed[0] if len(mixed) == 1 else jnp.concatenate(mixed, axis=0)
    o_ref[...] = (u * mixed).astype(o_ref.dtype)


def _chunk_mlp(z, layer, ws, bs_wide, ln_g, ln_b, want_v):
    n, W = z.shape[0], CM_WIDTH
    rows = min(CM_STEP_ROWS, n)
    first = (RW_PROJ + HG_PROJ) // W
    zcol = lambda j: pl.BlockSpec((rows, W), lambda i, j=j: (i, first + j))
    out_block = pl.BlockSpec((rows, W), lambda i: (i, 0))
    out_specs, out_shape = [out_block], [jax.ShapeDtypeStruct((n, W), BF16)]
    if want_v:
        out_specs.append(out_block)
        out_shape.append(jax.ShapeDtypeStruct((n, W), F32))
    return pl.pallas_call(
        _cm_kernel,
        grid=(n // rows,),
        in_specs=[zcol(0), zcol(1)] + [_layer_resident(w, layer) for w in (ws, bs_wide, ln_g, ln_b)],
        out_specs=out_specs,
        out_shape=out_shape,
        compiler_params=_params(("parallel",)),
        name="chunk_gmlp",
    )(z, z, ws, bs_wide, ln_g, ln_b)


def _dense_weights(p):
    vec = lambda a: a[:, None, :]
    out = {k: p[k].astype(BF16) for k in ("ffn1_w_in", "ffn1_w_out", "mix_w_in", "mix_w_out",
                                           "ffn2_w_in", "ffn2_w_out")}
    out.update({k: vec(p[k]) for k in ("ln1_g", "ln1_b", "ln2_g", "ln2_b", "ln3_g", "ln3_b")})
    return out


def _mixer_weights(p, lb):
    vec = lambda a: a.reshape(DEPTH, 1, -1)
    lora_pad = lambda w, start: jnp.pad(w, ((0, 0), (start, RW_WIDTH - start - w.shape[1]), (0, 0)))
    return dict(
        rw=dict(mu=vec(p["rw_mu"]), w0=vec(p["rw_w0"]),
                w_w2=lora_pad(p["rw_w_w2"], 0), a0=vec(p["rw_a0"]),
                a_w2=lora_pad(p["rw_a_w2"], RW_DECAY_LORA),
                g_w2=lora_pad(p["rw_g_w2"], RW_DECAY_LORA + RW_AAA_LORA),
                k_k=vec(p["rw_k_k"]), k_a=vec(p["rw_k_a"]), r_k=vec(p["rw_r_k"]),
                gn_g=vec(p["rw_gn_g"]), gn_b=vec(p["rw_gn_b"])),
        hg_lb=vec(lb), hg_norm_g=vec(p["hg_norm_g"]),
        cm_ws=p["cm_ws"],
        cm_bs=jnp.repeat(jnp.swapaxes(p["cm_bs"], 1, 2), CM_DIM, axis=2),
        cm_ln_g=vec(p["cm_ln_g"]), cm_ln_b=vec(p["cm_ln_b"]),
    )


def _short_chunk_mixing(ws, bs_wide, t_pad):
    reps = CM_CHUNK // t_pad
    eye = jnp.eye(reps, dtype=ws.dtype)
    ws_bd = jnp.einsum("ab,lhij->lhaibj", eye, ws[:, :, :t_pad, :t_pad])
    return (ws_bd.reshape(DEPTH, CM_HEADS, CM_CHUNK, CM_CHUNK),
            jnp.tile(bs_wide[:, :t_pad], (1, reps, 1)))


class _TokenGroup:
    def __init__(self, x, rw_s0, rw_shift0, hg_s0, w):
        self.B, self.T, _ = x.shape
        self.x = x.reshape(self.B * self.T, D_MODEL)
        self.has_state = rw_s0 is not None
        self.rw_s0, self.hg_s0 = rw_s0, hg_s0
        self.rw_shift0 = rw_shift0[:, :, None, :] if self.has_state else None
        T = self.T
        self.short = T < RW_CHUNK
        if self.short:
            self.t_pad = -(-T // SUBLANES) * SUBLANES
            self.rw_chunk = self.hg_chunk = self.rw_rows = self.hg_rows = self.t_pad
            self.n_seq = SAMPLE_STEP_SEQS
            self.cm_ws, self.cm_bs = _short_chunk_mixing(w["cm_ws"], w["cm_bs"], self.t_pad)
        else:
            self.t_pad = T
            self.rw_chunk, self.hg_chunk, self.n_seq = RW_CHUNK, HG_CHUNK, 1
            self.rw_rows, self.hg_rows = min(T, RW_STEP_ROWS), min(T, HG_STEP_ROWS)
            self.cm_ws, self.cm_bs = w["cm_ws"], w["cm_bs"]
        self.rw_states, self.hg_states, self.rw_shifts, self.cm_vs = None, None, [], []

    def mix(self, z, layer, w):
        B, T, t_pad = self.B, self.T, self.t_pad
        z3 = z.reshape(B, T, IN_PROJ)
        self.rw_shifts.append(z3[:, T - 1, :RW_PROJ])
        if t_pad != T:
            z = jnp.pad(z3, ((0, 0), (0, t_pad - T), (0, 0))).reshape(B * t_pad, IN_PROJ)
        o_rw, self.rw_states = _rwkv(z, self.rw_shift0, self.rw_s0, self.rw_states, layer, w["rw"], B,
                                     min(T, self.rw_rows), self.rw_chunk, self.n_seq, self.rw_rows)
        o_hg, self.hg_states = _hgrn(z, self.hg_s0, self.hg_states, layer, w["hg_lb"], w["hg_norm_g"],
                                     B, min(T, self.hg_rows), self.hg_chunk, self.n_seq, self.hg_rows)
        cm_out = _chunk_mlp(z, layer, self.cm_ws, self.cm_bs, w["cm_ln_g"], w["cm_ln_b"],
                            want_v=self.has_state)
        if self.has_state:
            self.cm_vs.append(cm_out[1].reshape(B, t_pad, CM_WIDTH)[:, :T])
        trim = lambda o: o.reshape(B, t_pad, o.shape[-1])[:, :T].reshape(B * T, o.shape[-1])
        return trim(o_rw), trim(o_hg), trim(cm_out[0])


def _run_trunk(groups, dw, w):
    for l in range(DEPTH):
        projected = _dense_in([g.x for g in groups], l, dw["ffn1_w_in"], dw["ffn1_w_out"],
                              dw["ln1_g"], dw["ln1_b"], dw["mix_w_in"])
        mixed = [(x1,) + g.mix(z, l, w) for g, (x1, z) in zip(groups, projected)]
        ys = _dense_out(mixed, l, dw["mix_w_out"], dw["ln2_g"], dw["ln2_b"], dw["ffn2_w_in"],
                        dw["ffn2_w_out"], dw["ln3_g"], dw["ln3_b"])
        for g, y in zip(groups, ys):
            g.x = y


def kernel(x_prompt, x_sample, state_rwkv, state_rwkv_shift, state_hgrn, ffn1_w_in, ffn1_w_out, ln1_g, ln1_b, mix_w_in, mix_w_out, ln2_g, ln2_b, rw_mu, rw_w0, rw_w_w2, rw_a0, rw_a_w2, rw_g_w2, rw_k_k, rw_k_a, rw_r_k, rw_gn_g, rw_gn_b, hg_lb_logits, hg_norm_g, cm_ws, cm_bs, cm_ln_g, cm_ln_b, ffn2_w_in, ffn2_w_out, ln3_g, ln3_b):
    p = dict(ffn1_w_in=ffn1_w_in, ffn1_w_out=ffn1_w_out, ln1_g=ln1_g, ln1_b=ln1_b,
             mix_w_in=mix_w_in, mix_w_out=mix_w_out, ln2_g=ln2_g, ln2_b=ln2_b,
             rw_mu=rw_mu, rw_w0=rw_w0, rw_w_w2=rw_w_w2, rw_a0=rw_a0, rw_a_w2=rw_a_w2,
             rw_g_w2=rw_g_w2, rw_k_k=rw_k_k, rw_k_a=rw_k_a, rw_r_k=rw_r_k,
             rw_gn_g=rw_gn_g, rw_gn_b=rw_gn_b, hg_norm_g=hg_norm_g,
             cm_ws=cm_ws, cm_bs=cm_bs, cm_ln_g=cm_ln_g, cm_ln_b=cm_ln_b,
             ffn2_w_in=ffn2_w_in, ffn2_w_out=ffn2_w_out, ln3_g=ln3_g, ln3_b=ln3_b)
    s = jax.nn.softmax(hg_lb_logits.astype(F32), axis=0)
    lb = jnp.cumsum(s, axis=0) - s[0]
    weights = _mixer_weights(p, lb)
    dw = _dense_weights(p)
    prompt = _TokenGroup(x_prompt, None, None, None, weights)
    sample = _TokenGroup(x_sample, state_rwkv, state_rwkv_shift, state_hgrn, weights)
    _run_trunk([prompt, sample], dw, weights)
    return (prompt.x.reshape(x_prompt.shape), sample.x.reshape(x_sample.shape),
            prompt.rw_states, jnp.stack(prompt.rw_shifts), prompt.hg_states,
            sample.rw_states, jnp.stack(sample.rw_shifts), sample.hg_states,
            jnp.stack(sample.cm_vs))
```

```python
import functools

import jax
import jax.numpy as jnp
from jax import lax
from jax.experimental import pallas as pl
from jax.experimental.pallas import tpu as pltpu

F32 = jnp.float32
BF16 = jnp.bfloat16

D_MODEL = 1024
DEPTH = 2
RW_HEADS, RW_DIM = 4, 64
RW_WIDTH = RW_HEADS * RW_DIM
RW_DECAY_LORA, RW_AAA_LORA, RW_GATE_LORA = 64, 64, 128
RW_PROJ = 3 * RW_WIDTH + RW_DECAY_LORA + RW_AAA_LORA + RW_GATE_LORA
RW_GN_EPS = 64e-5
HG_HEADS, HG_DIM = 4, 128
HG_WIDTH = HG_HEADS * HG_DIM
HG_PROJ = 4 * HG_WIDTH
RMS_EPS = 1e-6
F_MIN = 1e-30
CM_HEADS, CM_DIM = 4, 64
CM_WIDTH = CM_HEADS * CM_DIM
CM_CHUNK = 128
CM_PROJ = 2 * CM_WIDTH
MIX_WIDTH = RW_WIDTH + HG_WIDTH + CM_WIDTH
IN_PROJ = RW_PROJ + HG_PROJ + CM_PROJ
D_FF = 2816
LN_EPS = 1e-5
ALPHA = (2.0 * DEPTH) ** 0.25

VMEM_LIMIT_BYTES = 56 * 1024 * 1024
SUBLANES = 8
MXU_DIM = 256
DENSE_ROWS = 512
DENSE_SUB_ROWS = 256
FF_CHUNK = 1408
RW_CHUNK = 64
RW_GROUP = 64
HG_CHUNK = 64
HG_SUB = 16
EXP_CLAMP = 80.0
RW_BATCH_GROUPS = 4
RW_STEP_ROWS = 1024
HG_STEP_ROWS = 512
SAMPLE_STEP_SEQS = 16
CM_STEP_ROWS = 1024


def _params(semantics):
    return pltpu.CompilerParams(dimension_semantics=semantics,
                                vmem_limit_bytes=VMEM_LIMIT_BYTES)


_NN = ((1,), (0,))
_NT = ((1,), (1,))
_TN = ((0,), (0,))


def _dot(a, b, dims=_NN):
    return lax.dot_general(a.astype(BF16), b.astype(BF16), (dims, ((), ())),
                           preferred_element_type=F32)


def _mask_dot(mask, x):
    m = mask.astype(BF16)
    hi = x.astype(BF16)
    r1 = x - hi.astype(F32)
    mid = r1.astype(BF16)
    lo = (r1 - mid.astype(F32)).astype(BF16)
    dot = lambda t: jnp.dot(m, t, preferred_element_type=F32)
    return dot(hi) + (dot(mid) + dot(lo))


def _chunk_cumsum(x, chunk):
    n = min(x.shape[0], MXU_DIM)
    r = lax.broadcasted_iota(jnp.int32, (n, n), 0)
    c = lax.broadcasted_iota(jnp.int32, (n, n), 1)
    mask = ((r // chunk) == (c // chunk)) & (c <= r)
    pieces = [_mask_dot(mask, x[lo:lo + n]) for lo in range(0, x.shape[0], n)]
    return pieces[0] if len(pieces) == 1 else jnp.concatenate(pieces, axis=0)


def _slot_matrix(n_seq, t_valid, slot, transpose=False):
    shape = (n_seq * slot, n_seq * t_valid)
    padded = lax.broadcasted_iota(jnp.int32, shape[::-1] if transpose else shape, 1 if transpose else 0)
    packed = lax.broadcasted_iota(jnp.int32, shape[::-1] if transpose else shape, 0 if transpose else 1)
    return ((padded // slot) == (packed // t_valid)) & ((padded % slot) == (packed % t_valid))


def _unpack_rows(x, n_seq, t_valid, slot):
    return _mask_dot(_slot_matrix(n_seq, t_valid, slot), x)


def _pack_rows(x, n_seq, t_valid, slot):
    sel = _slot_matrix(n_seq, t_valid, slot, transpose=True).astype(BF16)
    return jnp.dot(sel, x, preferred_element_type=F32).astype(x.dtype)


def _layer_norm(x, g, b):
    mu = jnp.mean(x, axis=-1, keepdims=True)
    xc = x - mu
    var = jnp.mean(xc * xc, axis=-1, keepdims=True)
    return xc * lax.rsqrt(var + LN_EPS) * g + b


def _swiglu(xb, w_in_ref, w_out_ref):
    acc = None
    for lo in range(0, D_FF, FF_CHUNK):
        gate = jnp.dot(xb, w_in_ref[:, lo:lo + FF_CHUNK], preferred_element_type=F32)
        up = jnp.dot(xb, w_in_ref[:, D_FF + lo:D_FF + lo + FF_CHUNK], preferred_element_type=F32)
        h = (gate * jax.nn.sigmoid(gate) * up).astype(BF16)
        part = jnp.dot(h, w_out_ref[lo:lo + FF_CHUNK, :], preferred_element_type=F32)
        acc = part if acc is None else acc + part
    return acc


def _sub_tiles(n_rows):
    sub = min(DENSE_SUB_ROWS, n_rows)
    return [slice(lo, lo + sub) for lo in range(0, n_rows, sub)]


def _layer_resident(stacked, layer):
    tail = (0,) * (stacked.ndim - 1)
    return pl.BlockSpec((None,) + stacked.shape[1:], lambda *_: (layer,) + tail,
                        pipeline_mode=pl.Buffered(1))


def _grouped_row_call(body, name, groups, in_widths, out_widths, weights, layer):
    rows = min(DENSE_ROWS, min(g[0].shape[0] for g in groups))
    steps = [g[0].shape[0] // rows for g in groups]
    starts = [sum(steps[:k]) for k in range(len(groups))]
    n_in, n_out, n_w = len(in_widths), len(out_widths), len(weights)

    def kern(*refs):
        ins, w_refs, outs = refs[:len(groups) * n_in], refs[len(groups) * n_in:][:n_w], refs[-len(groups) * n_out:]
        step = pl.program_id(0)
        for k in range(len(groups)):
            run = functools.partial(body, *ins[k * n_in:(k + 1) * n_in], *w_refs,
                                    *outs[k * n_out:(k + 1) * n_out])
            if len(groups) == 1:
                run()
            else:
                pl.when((step >= starts[k]) & (step < starts[k] + steps[k]))(run)

    def block(width, k):
        return pl.BlockSpec((rows, width), lambda i: (jnp.clip(i - starts[k], 0, steps[k] - 1), 0))

    outs = pl.pallas_call(
        kern,
        grid=(sum(steps),),
        in_specs=[block(wd, k) for k in range(len(groups)) for wd in in_widths]
                 + [_layer_resident(w, layer) for w in weights],
        out_specs=[block(wd, k) for k in range(len(groups)) for wd in out_widths],
        out_shape=[jax.ShapeDtypeStruct((g[0].shape[0], wd), F32) for g in groups for wd in out_widths],
        compiler_params=_params(("arbitrary",)),
        name=name,
    )(*[a for g in groups for a in g], *weights)
    return [tuple(outs[k * n_out:(k + 1) * n_out]) for k in range(len(groups))]


def _dense_in_body(x_ref, w_in_ref, w_out_ref, g_ref, b_ref, w_mix_ref, x1_ref, z_ref):
    tiles = _sub_tiles(x_ref.shape[0])
    xs = [x_ref[t, :] for t in tiles]
    ffn = [_swiglu(x.astype(BF16), w_in_ref, w_out_ref) for x in xs]
    ys = [_layer_norm(ALPHA * x + 0.5 * f, g_ref[...], b_ref[...]) for x, f in zip(xs, ffn)]
    for t, y in zip(tiles, ys):
        x1_ref[t, :] = y
        z_ref[t, :] = jnp.dot(y.astype(BF16), w_mix_ref[...], preferred_element_type=F32)


def _dense_in(xs, layer, w_in, w_out, g, b, w_mix):
    return [_grouped_row_call(_dense_in_body, "dense_in", [(x,)], (D_MODEL,), (D_MODEL, IN_PROJ),
                              (w_in, w_out, g, b, w_mix), layer)[0] for x in xs]


def _dense_out_body(x_ref, orw_ref, ohg_ref, ocm_ref, wmix_ref, g2_ref, b2_ref,
                    w_in_ref, w_out_ref, g3_ref, b3_ref, y_ref):
    hg0, cm0 = RW_WIDTH, RW_WIDTH + HG_WIDTH
    tiles = _sub_tiles(x_ref.shape[0])
    mix = [jnp.dot(orw_ref[t, :], wmix_ref[:hg0, :], preferred_element_type=F32)
           + jnp.dot(ohg_ref[t, :], wmix_ref[hg0:cm0, :], preferred_element_type=F32)
           + jnp.dot(ocm_ref[t, :], wmix_ref[cm0:, :], preferred_element_type=F32) for t in tiles]
    x2 = [_layer_norm(ALPHA * x_ref[t, :] + m, g2_ref[...], b2_ref[...]) for t, m in zip(tiles, mix)]
    ffn = [_swiglu(x.astype(BF16), w_in_ref, w_out_ref) for x in x2]
    for t, x, f in zip(tiles, x2, ffn):
        y_ref[t, :] = _layer_norm(ALPHA * x + 0.5 * f, g3_ref[...], b3_ref[...])


def _dense_out(groups, layer, w_mix, g2, b2, w_in, w_out, g3, b3):
    ys = _grouped_row_call(_dense_out_body, "dense_out", groups,
                           (D_MODEL, RW_WIDTH, HG_WIDTH, CM_WIDTH), (D_MODEL,),
                           (w_mix, g2, b2, w_in, w_out, g3, b3), layer)
    return [y for (y,) in ys]


def _head_stack(x, head_masks):
    return jnp.concatenate([x * m for m in head_masks], axis=0)


def _rwkv_kernel(*refs, chunk, t_valid, n_seq, seq_rows, has_state, multi_step, layer):
    earlier_ref, refs = (refs[0], refs[1:]) if layer else (None, refs)
    if multi_step:
        *refs, s_scr, prev_scr = refs
    if has_state:
        (zr_ref, zk_ref, zv_ref, zx_ref, shift_ref, s0_ref, mu_ref, w0_ref, ww2_ref, a0_ref, aw2_ref,
         gw2_ref, kk_ref, ka_ref, rk_ref, gng_ref, gnb_ref, o_ref, s_out_ref) = refs
    else:
        (zr_ref, zk_ref, zv_ref, zx_ref, mu_ref, w0_ref, ww2_ref, a0_ref, aw2_ref,
         gw2_ref, kk_ref, ka_ref, rk_ref, gng_ref, gnb_ref, o_ref, s_out_ref) = refs
    C, N, H, W, G = chunk, RW_DIM, RW_HEADS, RW_WIDTH, RW_GROUP
    L = seq_rows
    R = n_seq * L

    def initial_state(b):
        if has_state:
            return jnp.concatenate([s0_ref[b, h] for h in range(H)], axis=1)
        return jnp.zeros((N, W), F32)

    def initial_prev(b):
        return shift_ref[b] if has_state else jnp.zeros((1, RW_PROJ), F32)

    if multi_step:
        step = pl.program_id(1)

        @pl.when(step == 0)
        def _():
            for b in range(n_seq):
                s_scr[b] = initial_state(b)
                prev_scr[b:b + 1, :] = initial_prev(b)

        carried_state = lambda b: s_scr[b]
        carried_prev = lambda b, cols: prev_scr[b:b + 1, cols]
    else:
        carried_state = initial_state
        carried_prev = lambda b, cols: initial_prev(b)[:, cols]

    RB = min(RW_BATCH_GROUPS * G, R)
    n_batches = R // RB
    assert (n_seq == 1 and t_valid == L) or n_batches == 1
    assert C == G or C == L
    brow = lax.broadcasted_iota(jnp.int32, (RB, 1), 0)
    lane = lax.broadcasted_iota(jnp.int32, (1, W), 1)
    head_masks = [((lane >= h * N) & (lane < (h + 1) * N)).astype(F32) for h in range(H)]
    hr = lax.broadcasted_iota(jnp.int32, (W, W), 0)
    hc = lax.broadcasted_iota(jnp.int32, (W, W), 1)
    head_ones = ((hr // N) == (hc // N)).astype(F32)
    elem = {}

    def prologue(bi):
        lo = bi * RB
        t_in = (brow + lo) % L
        valid = t_in < t_valid

        def load(ref, part):
            cols = slice(part * W, (part + 1) * W)
            if t_valid < L:
                z = _unpack_rows(ref[...], n_seq, t_valid, L)
            else:
                z = ref[lo:lo + RB, :]
            if n_seq == 1:
                first = carried_prev(0, cols) if bi == 0 else ref[lo - 1:lo, :]
                prev = jnp.where(brow == 0, first, pltpu.roll(z, 1, axis=0))
            else:
                first = jnp.concatenate([jnp.broadcast_to(carried_prev(b, cols), (L, W))
                                         for b in range(n_seq)], axis=0)
                prev = jnp.where(t_in == 0, first, pltpu.roll(z, 1, axis=0))
            if multi_step:
                for b in range(n_seq):
                    last = b * L + t_valid - 1 - lo
                    if 0 <= last < RB:
                        prev_scr[b:b + 1, cols] = z[last:last + 1, :]
            return z + (prev - z) * mu_ref[:, cols]

        r = load(zr_ref, 0)
        k = load(zk_ref, 1)
        v = load(zv_ref, 2)
        x4 = load(zx_ref, 3)

        w_pre = w0_ref[...] + _dot(jnp.tanh(x4), ww2_ref[...])
        yield
        nw = -w_pre
        softplus = jnp.maximum(nw, 0.0) + jnp.log(1.0 + jnp.exp(-jnp.abs(nw)))
        lw = -jnp.exp(-softplus - 0.5)
        a = jax.nn.sigmoid(a0_ref[...] + _dot(x4, aw2_ref[...]))
        yield
        gate = _dot(jax.nn.sigmoid(x4), gw2_ref[...])
        yield
        kk = k * kk_ref[...]
        k = k * (1.0 + (a - 1.0) * ka_ref[...])
        sums = _dot(jnp.concatenate([kk * kk, r * k * rk_ref[...]], axis=0), head_ones)
        yield
        kk = kk / jnp.maximum(jnp.sqrt(sums[:RB]), 1e-12)
        bonus = sums[RB:] * v
        if t_valid < L:
            lw = jnp.where(valid, lw, 0.0)
            kk = jnp.where(valid, kk, 0.0)
            k = jnp.where(valid, k, 0.0)
            v = jnp.where(valid, v, 0.0)
        cum = _chunk_cumsum(lw, C)
        yield
        g_in = jnp.exp(cum)
        g_inv = jnp.exp(-cum)
        elem[bi] = dict(a_hat=-kk * jnp.exp(cum - lw), b_chk=kk * a * g_inv, k_chk=k * g_inv,
                        r_hat=r * g_in, v=v, g_in=g_in, bonus=bonus, gate=gate)

    HG_ = H * G
    sr = lax.broadcasted_iota(jnp.int32, (HG_, HG_), 0)
    sc = lax.broadcasted_iota(jnp.int32, (HG_, HG_), 1)
    same_chunk = (sr // C) == (sc // C)
    strict = same_chunk & (sc < sr)
    incl = same_chunk & (sc <= sr)
    eye = (sr == sc).astype(F32)
    n_sub = G // C

    def gather(x, sub):
        if n_sub == 1:
            return x
        return jnp.concatenate([x[h * G + sub * C:h * G + (sub + 1) * C] for h in range(H)], axis=0)

    def scatter(pieces):
        if n_sub == 1:
            return pieces[0]
        return jnp.concatenate([pieces[sub][h * C:(h + 1) * C]
                                for h in range(H) for sub in range(n_sub)], axis=0)

    pre = {}
    gis = range(RB // G)

    def precompute(bi):
        e = elem[bi]
        a_st, b_st, k_st, r_st, v_st = {}, {}, {}, {}, {}
        for gi in gis:
            rows = slice(gi * G, (gi + 1) * G)
            a_st[gi] = _head_stack(e["a_hat"][rows], head_masks)
            b_st[gi] = _head_stack(e["b_chk"][rows], head_masks)
            k_st[gi] = _head_stack(e["k_chk"][rows], head_masks)
            r_st[gi] = _head_stack(e["r_hat"][rows], head_masks)
            v_st[gi] = jnp.concatenate([e["v"][rows, h * N:(h + 1) * N] for h in range(H)], axis=0)
        m_ab = {gi: jnp.where(strict, _dot(a_st[gi], b_st[gi], _NT), 0.0) for gi in gis}
        yield
        m_ak = {gi: jnp.where(strict, _dot(a_st[gi], k_st[gi], _NT), 0.0) for gi in gis}
        yield
        p_rb = {gi: jnp.where(incl, _dot(r_st[gi], b_st[gi], _NT), 0.0) for gi in gis}
        yield
        p_rk = {gi: jnp.where(incl, _dot(r_st[gi], k_st[gi], _NT), 0.0) for gi in gis}
        yield
        t_inv = {gi: eye + m_ab[gi] for gi in gis}
        power = dict(m_ab)
        span = 2
        while span < C:
            power = {gi: _dot(power[gi], power[gi]) for gi in gis}
            yield
            t_inv = {gi: t_inv[gi] + _dot(t_inv[gi], power[gi]) for gi in gis}
            yield
            span *= 2
        w_m = {gi: _dot(t_inv[gi], a_st[gi]) for gi in gis}
        yield
        mv = {gi: _dot(m_ak[gi], v_st[gi]) for gi in gis}
        yield
        u_m = {gi: _dot(t_inv[gi], mv[gi]) for gi in gis}
        yield
        vk = {gi: [_dot(gather(v_st[gi], sub), gather(k_st[gi], sub), _TN) for sub in range(n_sub)]
              for gi in gis}
        yield
        for gi in gis:
            pre[bi, gi] = dict(b_s=b_st[gi], r_s=r_st[gi], v_s=v_st[gi], p_rb=p_rb[gi], p_rk=p_rk[gi],
                               w_m=w_m[gi], u_m=u_m[gi], vk=vk[gi])

    states = [carried_state(b) for b in range(n_seq)]

    def chain(bi):
        e = elem[bi]
        outs = []
        for gi in gis:
            g = pre[bi, gi]
            subs = range(n_sub)
            seqs = [(bi * RB + gi * G + sub * C) // L for sub in subs]
            old = [states[b] for b in seqs]
            c_parts = [_dot(gather(g["w_m"], sub), old[sub], _NT) + gather(g["u_m"], sub)
                       for sub in subs]
            yield
            upd = [_dot(c_parts[sub], gather(g["b_s"], sub), _TN) for sub in subs]
            for sub in subs:
                last = gi * G + sub * C + C - 1
                states[seqs[sub]] = (old[sub] + upd[sub] + g["vk"][sub]) * e["g_in"][last:last + 1, :]
            yield
            rs_parts = [_dot(gather(g["r_s"], sub), old[sub], _NT) for sub in subs]
            yield
            o_s = scatter(rs_parts) + _dot(jnp.concatenate([g["p_rb"], g["p_rk"]], axis=1),
                                           jnp.concatenate([scatter(c_parts), g["v_s"]], axis=0))
            mu_o = jnp.mean(o_s, axis=-1, keepdims=True)
            oc = o_s - mu_o
            var_o = jnp.mean(oc * oc, axis=-1, keepdims=True)
            on = oc * lax.rsqrt(var_o + RW_GN_EPS)
            outs.append(jnp.concatenate([on[h * G:(h + 1) * G, :] for h in range(H)], axis=1))
            yield
        o = outs[0] if len(outs) == 1 else jnp.concatenate(outs, axis=0)
        o = ((o * gng_ref[...] + gnb_ref[...] + e["bonus"]) * e["gate"]).astype(o_ref.dtype)
        if t_valid < L:
            o_ref[...] = _pack_rows(o, n_seq, t_valid, L)
        else:
            o_ref[bi * RB:(bi + 1) * RB, :] = o

    def run_interleaved(*gens):
        live = list(gens)
        while live:
            for gen in list(live):
                if next(gen, StopIteration) is StopIteration:
                    live.remove(gen)

    run_interleaved(prologue(0))
    for bi in range(n_batches):
        run_interleaved(*([precompute(bi)]
                          + ([prologue(bi + 1)] if bi + 1 < n_batches else [])
                          + ([chain(bi - 1)] if bi > 0 else [])))
    run_interleaved(chain(n_batches - 1))

    def write_final_states():
        if layer:
            s_out_ref[:layer] = earlier_ref[...]
        for b in range(n_seq):
            for h in range(H):
                s_out_ref[layer, b, h] = states[b][:, h * N:(h + 1) * N]

    if multi_step:
        for b in range(n_seq):
            s_scr[b] = states[b]
        pl.when(step == pl.num_programs(1) - 1)(write_final_states)
    else:
        write_final_states()


def _seq_grid(n_rows, n_batch, n_seq, step_rows):
    rows_per_seq = n_rows // n_batch
    time_steps = rows_per_seq // step_rows
    assert n_seq == 1 or time_steps == 1
    return (n_batch // n_seq, time_steps), (lambda i, t: i * time_steps + t)


def _layer_state_call(kern, layer, earlier, state_shape, n_seq, **kwargs):
    tail = (0,) * (len(state_shape) - 1)
    block = lambda depth: pl.BlockSpec((depth, n_seq) + state_shape[1:], lambda i, t: (0, i) + tail)
    layer_block = pl.BlockSpec((None, n_seq) + state_shape[1:], lambda i, t: (layer, i) + tail)

    def call(args, specs, o_spec, o_shape):
        if layer:
            args = [earlier] + args
            specs = [block(layer)] + specs
        return pl.pallas_call(
            functools.partial(kern, layer=layer),
            in_specs=specs,
            out_specs=[o_spec, block(layer + 1)],
            out_shape=[o_shape, jax.ShapeDtypeStruct((layer + 1,) + state_shape, F32)],
            **kwargs,
        )(*args)

    return layer_block, call


def _rwkv(z, shift0, s0, states_out, layer, p, n_batch, t_valid, chunk, n_seq, seq_rows):
    W = RW_WIDTH
    has_state = s0 is not None
    grid, row_block = _seq_grid(z.shape[0], n_batch, n_seq, t_valid)
    R = n_seq * t_valid
    zcol = lambda j: pl.BlockSpec((R, W), lambda i, t, j=j: (row_block(i, t), j))
    weights = (p["mu"], p["w0"], p["w_w2"], p["a0"], p["a_w2"], p["g_w2"], p["k_k"], p["k_a"],
               p["r_k"], p["gn_g"], p["gn_b"])
    multi_step = grid[1] > 1
    kern = functools.partial(_rwkv_kernel, chunk=chunk, t_valid=t_valid, n_seq=n_seq,
                             seq_rows=seq_rows, has_state=has_state, multi_step=multi_step)
    state_spec, call = _layer_state_call(
        kern, layer, states_out, (n_batch, RW_HEADS, RW_DIM, RW_DIM), n_seq,
        grid=grid,
        scratch_shapes=[pltpu.VMEM((n_seq, RW_DIM, W), F32),
                        pltpu.VMEM((n_seq, RW_PROJ), F32)] if multi_step else [],
        compiler_params=_params(("parallel", "arbitrary")),
        name="rwkv7")
    args, specs = [z, z, z, z], [zcol(0), zcol(1), zcol(2), zcol(3)]
    if has_state:
        args += [shift0, s0]
        specs += [pl.BlockSpec((None, n_seq, 1, RW_PROJ), lambda i, t: (layer, i, 0, 0)), state_spec]
    args += list(weights)
    specs += [_layer_resident(w, layer) for w in weights]
    return call(args, specs, pl.BlockSpec((R, W), lambda i, t: (row_block(i, t), 0)),
                jax.ShapeDtypeStruct((z.shape[0], W), BF16))


def _hgrn_kernel(*refs, chunk, t_valid, n_seq, seq_rows, has_state, multi_step, layer):
    earlier_ref, refs = (refs[0], refs[1:]) if layer else (None, refs)
    if multi_step:
        *refs, s_scr = refs
    if has_state:
        zq_ref, zf_ref, zi_ref, zg_ref, s0_ref, lb_ref, ng_ref, o_ref, s_out_ref = refs
    else:
        zq_ref, zf_ref, zi_ref, zg_ref, lb_ref, ng_ref, o_ref, s_out_ref = refs
    C, N, H = chunk, HG_DIM, HG_HEADS
    L = seq_rows
    R = n_seq * L
    sub = min(HG_SUB, C)
    if t_valid < L:
        load = lambda ref: _unpack_rows(ref[...], n_seq, t_valid, L)
    else:
        load = lambda ref: ref[...]

    def initial_state(b, h):
        return s0_ref[b, h].T if has_state else jnp.zeros((N, N), F32)

    if multi_step:
        step = pl.program_id(1)

        @pl.when(step == 0)
        def _():
            for b in range(n_seq):
                for h in range(H):
                    s_scr[b, h] = initial_state(b, h)

        carried_state = lambda b, h: s_scr[b, h]
    else:
        carried_state = initial_state

    zq = load(zq_ref)
    q = zq * jax.nn.sigmoid(zq)
    lb = lb_ref[...]
    f = lb + (1.0 - lb) * jax.nn.sigmoid(load(zf_ref))
    log_f = jnp.log(jnp.maximum(f, F_MIN))
    k = 1.0 - f
    v = load(zi_ref)
    zg = load(zg_ref)
    out_gate = ng_ref[...] * (zg * jax.nn.sigmoid(zg))
    if t_valid < L:
        valid = (lax.broadcasted_iota(jnp.int32, (R, 1), 0) % L) < t_valid
        log_f = jnp.where(valid, log_f, 0.0)
        k = jnp.where(valid, k, 0.0)

    cum_all = _chunk_cumsum(log_f, C)
    tr = lax.broadcasted_iota(jnp.int32, (C, C), 0)
    tc = lax.broadcasted_iota(jnp.int32, (C, C), 1)
    causal = tc <= tr

    units = [(c, h) for c in range(R // C) for h in range(H)]
    part = lambda x, c, h: x[c * C:(c + 1) * C, h * N:(h + 1) * N]
    cums = {u: part(cum_all, *u) for u in units}
    totals = {u: cums[u][C - 1:C, :] for u in units}
    scores, intra, updates, inter = {}, {}, {}, {}
    n_blk = C // sub
    states = [[carried_state(b, h) for h in range(H)] for b in range(n_seq)]

    def score_stage(u):
        cum, q_h, k_h = cums[u], part(q, *u), part(k, *u)
        blks = [slice(i * sub, (i + 1) * sub) for i in range(n_blk)]
        refs = [cum[i * sub:i * sub + 1, :] for i in range(n_blk)]
        k_own = [k_h[blks[j]] * jnp.exp(jnp.minimum(refs[j] - cum[blks[j]], EXP_CLAMP))
                 for j in range(n_blk)]
        score_rows = []
        for i in range(n_blk):
            q_hat = q_h[blks[i]] * jnp.exp(cum[blks[i]] - refs[i])
            pieces = [k_own[j] * jnp.exp(refs[i] - refs[j]) for j in range(i)] + [k_own[i]]
            if i + 1 < n_blk:
                pieces.append(jnp.zeros(((n_blk - 1 - i) * sub, N), F32))
            k_hat = pieces[0] if len(pieces) == 1 else jnp.concatenate(pieces, axis=0)
            score_rows.append(_dot(q_hat, k_hat, _NT))
        rows_ = score_rows[0] if len(score_rows) == 1 else jnp.concatenate(score_rows, axis=0)
        scores[u] = jnp.where(causal, rows_, 0.0)

    def intra_stage(u):
        intra[u] = _dot(scores[u], part(v, *u))
        updates[u] = _dot(part(v, *u), part(k, *u) * jnp.exp(totals[u] - cums[u]), _TN)

    def state_stage(u):
        c, h = u
        b = (c * C) // L
        inter[u] = _dot(part(q, *u) * jnp.exp(cums[u]), states[b][h], _NT)
        states[b][h] = states[b][h] * jnp.exp(totals[u]) + updates[u]

    stages = (score_stage, intra_stage, state_stage)
    for idx in range(len(units) + len(stages) - 1):
        for lag, stage in enumerate(stages):
            if 0 <= idx - lag < len(units):
                stage(units[idx - lag])
    out_rows = []
    for c in range(R // C):
        outs = []
        for h in range(H):
            o_h = intra[(c, h)] + inter[(c, h)]
            outs.append(o_h * lax.rsqrt(jnp.mean(o_h * o_h, axis=-1, keepdims=True) + RMS_EPS))
        out_rows.append(jnp.concatenate(outs, axis=1))
    o = out_rows[0] if len(out_rows) == 1 else jnp.concatenate(out_rows, axis=0)
    o = (o * out_gate).astype(o_ref.dtype)
    o_ref[...] = _pack_rows(o, n_seq, t_valid, L) if t_valid < L else o

    def write_final_states():
        if layer:
            s_out_ref[:layer] = earlier_ref[...]
        for b in range(n_seq):
            for h in range(H):
                s_out_ref[layer, b, h] = states[b][h].T

    if multi_step:
        for b in range(n_seq):
            for h in range(H):
                s_scr[b, h] = states[b][h]
        pl.when(step == pl.num_programs(1) - 1)(write_final_states)
    else:
        write_final_states()


def _hgrn(z, s0, states_out, layer, lb, norm_g, n_batch, t_valid, chunk, n_seq, seq_rows):
    W = HG_WIDTH
    first = RW_PROJ // W
    has_state = s0 is not None
    grid, row_block = _seq_grid(z.shape[0], n_batch, n_seq, t_valid)
    R = n_seq * t_valid
    zcol = lambda j: pl.BlockSpec((R, W), lambda i, t, j=j: (row_block(i, t), first + j))
    multi_step = grid[1] > 1
    kern = functools.partial(_hgrn_kernel, chunk=chunk, t_valid=t_valid, n_seq=n_seq,
                             seq_rows=seq_rows, has_state=has_state, multi_step=multi_step)
    state_spec, call = _layer_state_call(
        kern, layer, states_out, (n_batch, HG_HEADS, HG_DIM, HG_DIM), n_seq,
        grid=grid,
        scratch_shapes=[pltpu.VMEM((n_seq, HG_HEADS, HG_DIM, HG_DIM), F32)] if multi_step else [],
        compiler_params=_params(("parallel", "arbitrary")),
        name="hgrn2")
    args, specs = [z, z, z, z], [zcol(0), zcol(1), zcol(2), zcol(3)]
    if has_state:
        args.append(s0)
        specs.append(state_spec)
    args += [lb, norm_g]
    specs += [_layer_resident(lb, layer), _layer_resident(norm_g, layer)]
    return call(args, specs, pl.BlockSpec((R, W), lambda i, t: (row_block(i, t), 0)),
                jax.ShapeDtypeStruct((z.shape[0], W), BF16))


def _gelu(x):
    return 0.5 * x * (1.0 + lax.erf(x * (2.0 ** -0.5)))


def _cm_kernel(zu_ref, zv_ref, ws_ref, bs_ref, g_ref, b_ref, o_ref, *v_out):
    H, N, W, C = CM_HEADS, CM_DIM, CM_WIDTH, CM_CHUNK
    u = _gelu(zu_ref[...])
    v = _gelu(zv_ref[...])
    hr = lax.broadcasted_iota(jnp.int32, (W, W), 0)
    hc = lax.broadcasted_iota(jnp.int32, (W, W), 1)
    head_mean = jnp.where((hr // N) == (hc // N), 1.0 / N, 0.0).astype(F32)
    vc = v - _dot(v, head_mean)
    var = _dot(vc * vc, head_mean)
    vn = vc * lax.rsqrt(var + LN_EPS) * g_ref[...] + b_ref[...]
    if v_out:
        v_out[0][...] = vn
    tr = lax.broadcasted_iota(jnp.int32, (C, C), 0)
    tc = lax.broadcasted_iota(jnp.int32, (C, C), 1)
    lane = lax.broadcasted_iota(jnp.int32, (1, W), 1)
    w_causal = [jnp.where(tc <= tr, ws_ref[h], 0.0).astype(BF16) for h in range(H)]
    head_cols = [(lane >= h * N) & (lane < (h + 1) * N) for h in range(H)]
    vb = vn.astype(BF16)
    mixed = []
    for c in range(zu_ref.shape[0] // C):
        v_c = vb[c * C:(c + 1) * C]
        acc = bs_ref[...]
        for h in range(H):
            acc = acc + jnp.dot(w_causal[h], jnp.where(head_cols[h], v_c, jnp.zeros_like(v_c)),
                                preferred_element_type=F32)
        mixed.append(acc)
    mixed = mixed[0] if len(mixed) == 1 else jnp.concatenate(mixed, axis=0)
    o_ref[...] = (u * mixed).astype(o_ref.dtype)


def _chunk_mlp(z, layer, ws, bs_wide, ln_g, ln_b, want_v):
    n, W = z.shape[0], CM_WIDTH
    rows = min(CM_STEP_ROWS, n)
    first = (RW_PROJ + HG_PROJ) // W
    zcol = lambda j: pl.BlockSpec((rows, W), lambda i, j=j: (i, first + j))
    out_block = pl.BlockSpec((rows, W), lambda i: (i, 0))
    out_specs, out_shape = [out_block], [jax.ShapeDtypeStruct((n, W), BF16)]
    if want_v:
        out_specs.append(out_block)
        out_shape.append(jax.ShapeDtypeStruct((n, W), F32))
    return pl.pallas_call(
        _cm_kernel,
        grid=(n // rows,),
        in_specs=[zcol(0), zcol(1)] + [_layer_resident(w, layer) for w in (ws, bs_wide, ln_g, ln_b)],
        out_specs=out_specs,
        out_shape=out_shape,
        compiler_params=_params(("parallel",)),
        name="chunk_gmlp",
    )(z, z, ws, bs_wide, ln_g, ln_b)


def _dense_weights(p):
    vec = lambda a: a[:, None, :]
    out = {k: p[k].astype(BF16) for k in ("ffn1_w_in", "ffn1_w_out", "mix_w_in", "mix_w_out",
                                           "ffn2_w_in", "ffn2_w_out")}
    out.update({k: vec(p[k]) for k in ("ln1_g", "ln1_b", "ln2_g", "ln2_b", "ln3_g", "ln3_b")})
    return out


def _mixer_weights(p, lb):
    vec = lambda a: a.reshape(DEPTH, 1, -1)
    lora_pad = lambda w, start: jnp.pad(w, ((0, 0), (start, RW_WIDTH - start - w.shape[1]), (0, 0)))
    return dict(
        rw=dict(mu=vec(p["rw_mu"]), w0=vec(p["rw_w0"]),
                w_w2=lora_pad(p["rw_w_w2"], 0), a0=vec(p["rw_a0"]),
                a_w2=lora_pad(p["rw_a_w2"], RW_DECAY_LORA),
                g_w2=lora_pad(p["rw_g_w2"], RW_DECAY_LORA + RW_AAA_LORA),
                k_k=vec(p["rw_k_k"]), k_a=vec(p["rw_k_a"]), r_k=vec(p["rw_r_k"]),
                gn_g=vec(p["rw_gn_g"]), gn_b=vec(p["rw_gn_b"])),
        hg_lb=vec(lb), hg_norm_g=vec(p["hg_norm_g"]),
        cm_ws=p["cm_ws"],
        cm_bs=jnp.repeat(jnp.swapaxes(p["cm_bs"], 1, 2), CM_DIM, axis=2),
        cm_ln_g=vec(p["cm_ln_g"]), cm_ln_b=vec(p["cm_ln_b"]),
    )


def _short_chunk_mixing(ws, bs_wide, t_pad):
    reps = CM_CHUNK // t_pad
    eye = jnp.eye(reps, dtype=ws.dtype)
    ws_bd = jnp.einsum("ab,lhij->lhaibj", eye, ws[:, :, :t_pad, :t_pad])
    return (ws_bd.reshape(DEPTH, CM_HEADS, CM_CHUNK, CM_CHUNK),
            jnp.tile(bs_wide[:, :t_pad], (1, reps, 1)))


class _TokenGroup:
    def __init__(self, x, rw_s0, rw_shift0, hg_s0, w):
        self.B, self.T, _ = x.shape
        self.x = x.reshape(self.B * self.T, D_MODEL)
        self.has_state = rw_s0 is not None
        self.rw_s0, self.hg_s0 = rw_s0, hg_s0
        self.rw_shift0 = rw_shift0[:, :, None, :] if self.has_state else None
        T = self.T
        self.short = T < RW_CHUNK
        if self.short:
            slot = -(-T // SUBLANES) * SUBLANES
            self.rw_chunk = self.hg_chunk = self.rw_rows = self.hg_rows = slot
            self.n_seq = SAMPLE_STEP_SEQS
            self.cm_ws, self.cm_bs = _short_chunk_mixing(w["cm_ws"], w["cm_bs"], T)
        else:
            self.rw_chunk, self.hg_chunk, self.n_seq = RW_CHUNK, HG_CHUNK, 1
            self.rw_rows, self.hg_rows = min(T, RW_STEP_ROWS), min(T, HG_STEP_ROWS)
            self.cm_ws, self.cm_bs = w["cm_ws"], w["cm_bs"]
        self.rw_states, self.hg_states, self.rw_shifts, self.cm_vs = None, None, [], []

    def mix(self, z, layer, w):
        B, T = self.B, self.T
        self.rw_shifts.append(z.reshape(B, T, IN_PROJ)[:, T - 1, :RW_PROJ])
        o_rw, self.rw_states = _rwkv(z, self.rw_shift0, self.rw_s0, self.rw_states, layer, w["rw"], B,
                                     min(T, self.rw_rows), self.rw_chunk, self.n_seq, self.rw_rows)
        o_hg, self.hg_states = _hgrn(z, self.hg_s0, self.hg_states, layer, w["hg_lb"], w["hg_norm_g"],
                                     B, min(T, self.hg_rows), self.hg_chunk, self.n_seq, self.hg_rows)
        cm_out = _chunk_mlp(z, layer, self.cm_ws, self.cm_bs, w["cm_ln_g"], w["cm_ln_b"],
                            want_v=self.has_state)
        if self.has_state:
            self.cm_vs.append(cm_out[1].reshape(B, T, CM_WIDTH))
        return o_rw, o_hg, cm_out[0]


def _run_trunk(groups, dw, w):
    for l in range(DEPTH):
        projected = _dense_in([g.x for g in groups], l, dw["ffn1_w_in"], dw["ffn1_w_out"],
                              dw["ln1_g"], dw["ln1_b"], dw["mix_w_in"])
        mixed = [(x1,) + g.mix(z, l, w) for g, (x1, z) in zip(groups, projected)]
        ys = _dense_out(mixed, l, dw["mix_w_out"], dw["ln2_g"], dw["ln2_b"], dw["ffn2_w_in"],
                        dw["ffn2_w_out"], dw["ln3_g"], dw["ln3_b"])
        for g, y in zip(groups, ys):
            g.x = y


def kernel(x_prompt, x_sample, state_rwkv, state_rwkv_shift, state_hgrn, ffn1_w_in, ffn1_w_out, ln1_g, ln1_b, mix_w_in, mix_w_out, ln2_g, ln2_b, rw_mu, rw_w0, rw_w_w2, rw_a0, rw_a_w2, rw_g_w2, rw_k_k, rw_k_a, rw_r_k, rw_gn_g, rw_gn_b, hg_lb_logits, hg_norm_g, cm_ws, cm_bs, cm_ln_g, cm_ln_b, ffn2_w_in, ffn2_w_out, ln3_g, ln3_b):
    p = dict(ffn1_w_in=ffn1_w_in, ffn1_w_out=ffn1_w_out, ln1_g=ln1_g, ln1_b=ln1_b,
             mix_w_in=mix_w_in, mix_w_out=mix_w_out, ln2_g=ln2_g, ln2_b=ln2_b,
             rw_mu=rw_mu, rw_w0=rw_w0, rw_w_w2=rw_w_w2, rw_a0=rw_a0, rw_a_w2=rw_a_w2,
             rw_g_w2=rw_g_w2, rw_k_k=rw_k_k, rw_k_a=rw_k_a, rw_r_k=rw_r_k,
             rw_gn_g=rw_gn_g, rw_gn_b=rw_gn_b, hg_norm_g=hg_norm_g,
             cm_ws=cm_ws, cm_bs=cm_bs, cm_ln_g=cm_ln_g, cm_ln_b=cm_ln_b,
             ffn2_w_in=ffn2_w_in, ffn2_w_out=ffn2_w_out, ln3_g=ln3_g, ln3_b=ln3_b)
    s = jax.nn.softmax(hg_lb_logits.astype(F32), axis=0)
    lb = jnp.cumsum(s, axis=0) - s[0]
    weights = _mixer_weights(p, lb)
    dw = _dense_weights(p)
    prompt = _TokenGroup(x_prompt, None, None, None, weights)
    sample = _TokenGroup(x_sample, state_rwkv, state_rwkv_shift, state_hgrn, weights)
    _run_trunk([prompt, sample], dw, weights)
    return (prompt.x.reshape(x_prompt.shape), sample.x.reshape(x_sample.shape),
            prompt.rw_states, jnp.stack(prompt.rw_shifts), prompt.hg_states,
            sample.rw_states, jnp.stack(sample.rw_shifts), sample.hg_states,
            jnp.stack(sample.cm_vs))
```

```python
import functools

import jax
import jax.numpy as jnp
from jax import lax
from jax.experimental import pallas as pl
from jax.experimental.pallas import tpu as pltpu

F32 = jnp.float32
BF16 = jnp.bfloat16

D_MODEL = 1024
DEPTH = 2
RW_HEADS, RW_DIM = 4, 64
RW_WIDTH = RW_HEADS * RW_DIM
RW_DECAY_LORA, RW_AAA_LORA, RW_GATE_LORA = 64, 64, 128
RW_PROJ = 3 * RW_WIDTH + RW_DECAY_LORA + RW_AAA_LORA + RW_GATE_LORA
RW_GN_EPS = 64e-5
HG_HEADS, HG_DIM = 4, 128
HG_WIDTH = HG_HEADS * HG_DIM
HG_PROJ = 4 * HG_WIDTH
RMS_EPS = 1e-6
F_MIN = 1e-30
CM_HEADS, CM_DIM = 4, 64
CM_WIDTH = CM_HEADS * CM_DIM
CM_CHUNK = 128
CM_PROJ = 2 * CM_WIDTH
MIX_WIDTH = RW_WIDTH + HG_WIDTH + CM_WIDTH
IN_PROJ = RW_PROJ + HG_PROJ + CM_PROJ
D_FF = 2816
LN_EPS = 1e-5
ALPHA = (2.0 * DEPTH) ** 0.25

VMEM_LIMIT_BYTES = 56 * 1024 * 1024
SUBLANES = 8
MXU_DIM = 256
DENSE_ROWS = 512
DENSE_SUB_ROWS = 256
FF_CHUNK = 1408
RW_CHUNK = 64
RW_GROUP = 64
HG_CHUNK = 64
HG_SUB = 16
EXP_CLAMP = 80.0
RW_BATCH_GROUPS = 4
RW_STEP_ROWS = 1024
HG_STEP_ROWS = 512
SAMPLE_STEP_SEQS = 16
CM_STEP_ROWS = 1024


def _params(semantics):
    return pltpu.CompilerParams(dimension_semantics=semantics,
                                vmem_limit_bytes=VMEM_LIMIT_BYTES)


_NN = ((1,), (0,))
_NT = ((1,), (1,))
_TN = ((0,), (0,))


def _dot(a, b, dims=_NN):
    return lax.dot_general(a.astype(BF16), b.astype(BF16), (dims, ((), ())),
                           preferred_element_type=F32)


def _mask_dot(mask, x):
    m = mask.astype(BF16)
    hi = x.astype(BF16)
    r1 = x - hi.astype(F32)
    mid = r1.astype(BF16)
    lo = (r1 - mid.astype(F32)).astype(BF16)
    dot = lambda t: jnp.dot(m, t, preferred_element_type=F32)
    return dot(hi) + (dot(mid) + dot(lo))


def _chunk_cumsum(x, chunk):
    n = min(x.shape[0], MXU_DIM)
    r = lax.broadcasted_iota(jnp.int32, (n, n), 0)
    c = lax.broadcasted_iota(jnp.int32, (n, n), 1)
    mask = ((r // chunk) == (c // chunk)) & (c <= r)
    pieces = [_mask_dot(mask, x[lo:lo + n]) for lo in range(0, x.shape[0], n)]
    return pieces[0] if len(pieces) == 1 else jnp.concatenate(pieces, axis=0)


def _slot_matrix(n_seq, t_valid, slot, transpose=False):
    shape = (n_seq * slot, n_seq * t_valid)
    padded = lax.broadcasted_iota(jnp.int32, shape[::-1] if transpose else shape, 1 if transpose else 0)
    packed = lax.broadcasted_iota(jnp.int32, shape[::-1] if transpose else shape, 0 if transpose else 1)
    return ((padded // slot) == (packed // t_valid)) & ((padded % slot) == (packed % t_valid))


def _unpack_rows(x, n_seq, t_valid, slot):
    return _mask_dot(_slot_matrix(n_seq, t_valid, slot), x)


def _pack_rows(x, n_seq, t_valid, slot):
    sel = _slot_matrix(n_seq, t_valid, slot, transpose=True).astype(BF16)
    return jnp.dot(sel, x, preferred_element_type=F32).astype(x.dtype)


def _layer_norm(x, g, b):
    mu = jnp.mean(x, axis=-1, keepdims=True)
    xc = x - mu
    var = jnp.mean(xc * xc, axis=-1, keepdims=True)
    return xc * lax.rsqrt(var + LN_EPS) * g + b


def _swiglu(xb, w_in_ref, w_out_ref):
    acc = None
    for lo in range(0, D_FF, FF_CHUNK):
        gate = jnp.dot(xb, w_in_ref[:, lo:lo + FF_CHUNK], preferred_element_type=F32)
        up = jnp.dot(xb, w_in_ref[:, D_FF + lo:D_FF + lo + FF_CHUNK], preferred_element_type=F32)
        h = (gate * jax.nn.sigmoid(gate) * up).astype(BF16)
        part = jnp.dot(h, w_out_ref[lo:lo + FF_CHUNK, :], preferred_element_type=F32)
        acc = part if acc is None else acc + part
    return acc


def _sub_tiles(n_rows):
    sub = min(DENSE_SUB_ROWS, n_rows)
    return [slice(lo, lo + sub) for lo in range(0, n_rows, sub)]


def _layer_resident(stacked, layer):
    tail = (0,) * (stacked.ndim - 1)
    return pl.BlockSpec((None,) + stacked.shape[1:], lambda *_: (layer,) + tail,
                        pipeline_mode=pl.Buffered(1))


def _grouped_row_call(body, name, groups, in_widths, out_widths, weights, layer):
    rows = min(DENSE_ROWS, min(g[0].shape[0] for g in groups))
    steps = [g[0].shape[0] // rows for g in groups]
    starts = [sum(steps[:k]) for k in range(len(groups))]
    n_in, n_out, n_w = len(in_widths), len(out_widths), len(weights)

    def kern(*refs):
        ins, w_refs, outs = refs[:len(groups) * n_in], refs[len(groups) * n_in:][:n_w], refs[-len(groups) * n_out:]
        step = pl.program_id(0)
        for k in range(len(groups)):
            run = functools.partial(body, *ins[k * n_in:(k + 1) * n_in], *w_refs,
                                    *outs[k * n_out:(k + 1) * n_out])
            if len(groups) == 1:
                run()
            else:
                pl.when((step >= starts[k]) & (step < starts[k] + steps[k]))(run)

    def block(width, k):
        return pl.BlockSpec((rows, width), lambda i: (jnp.clip(i - starts[k], 0, steps[k] - 1), 0))

    outs = pl.pallas_call(
        kern,
        grid=(sum(steps),),
        in_specs=[block(wd, k) for k in range(len(groups)) for wd in in_widths]
                 + [_layer_resident(w, layer) for w in weights],
        out_specs=[block(wd, k) for k in range(len(groups)) for wd in out_widths],
        out_shape=[jax.ShapeDtypeStruct((g[0].shape[0], wd), F32) for g in groups for wd in out_widths],
        compiler_params=_params(("arbitrary",)),
        name=name,
    )(*[a for g in groups for a in g], *weights)
    return [tuple(outs[k * n_out:(k + 1) * n_out]) for k in range(len(groups))]


def _dense_in_body(x_ref, w_in_ref, w_out_ref, g_ref, b_ref, w_mix_ref, x1_ref, z_ref):
    tiles = _sub_tiles(x_ref.shape[0])
    xs = [x_ref[t, :] for t in tiles]
    ffn = [_swiglu(x.astype(BF16), w_in_ref, w_out_ref) for x in xs]
    ys = [_layer_norm(ALPHA * x + 0.5 * f, g_ref[...], b_ref[...]) for x, f in zip(xs, ffn)]
    for t, y in zip(tiles, ys):
        x1_ref[t, :] = y
        z_ref[t, :] = jnp.dot(y.astype(BF16), w_mix_ref[...], preferred_element_type=F32)


def _dense_in(xs, layer, w_in, w_out, g, b, w_mix):
    return [_grouped_row_call(_dense_in_body, "dense_in", [(x,)], (D_MODEL,), (D_MODEL, IN_PROJ),
                              (w_in, w_out, g, b, w_mix), layer)[0] for x in xs]


def _dense_out_body(x_ref, orw_ref, ohg_ref, ocm_ref, wmix_ref, g2_ref, b2_ref,
                    w_in_ref, w_out_ref, g3_ref, b3_ref, y_ref):
    hg0, cm0 = RW_WIDTH, RW_WIDTH + HG_WIDTH
    tiles = _sub_tiles(x_ref.shape[0])
    mix = [jnp.dot(orw_ref[t, :], wmix_ref[:hg0, :], preferred_element_type=F32)
           + jnp.dot(ohg_ref[t, :], wmix_ref[hg0:cm0, :], preferred_element_type=F32)
           + jnp.dot(ocm_ref[t, :], wmix_ref[cm0:, :], preferred_element_type=F32) for t in tiles]
    x2 = [_layer_norm(ALPHA * x_ref[t, :] + m, g2_ref[...], b2_ref[...]) for t, m in zip(tiles, mix)]
    ffn = [_swiglu(x.astype(BF16), w_in_ref, w_out_ref) for x in x2]
    for t, x, f in zip(tiles, x2, ffn):
        y_ref[t, :] = _layer_norm(ALPHA * x + 0.5 * f, g3_ref[...], b3_ref[...])


def _dense_out(groups, layer, w_mix, g2, b2, w_in, w_out, g3, b3):
    ys = _grouped_row_call(_dense_out_body, "dense_out", groups,
                           (D_MODEL, RW_WIDTH, HG_WIDTH, CM_WIDTH), (D_MODEL,),
                           (w_mix, g2, b2, w_in, w_out, g3, b3), layer)
    return [y for (y,) in ys]


def _head_stack(x, head_masks):
    return jnp.concatenate([x * m for m in head_masks], axis=0)


def _rwkv_kernel(*refs, chunk, t_valid, n_seq, seq_rows, has_state, multi_step, layer):
    earlier_ref, refs = (refs[0], refs[1:]) if layer else (None, refs)
    if multi_step:
        *refs, s_scr, prev_scr = refs
    if has_state:
        (zr_ref, zk_ref, zv_ref, zx_ref, shift_ref, s0_ref, mu_ref, w0_ref, ww2_ref, a0_ref, aw2_ref,
         gw2_ref, kk_ref, ka_ref, rk_ref, gng_ref, gnb_ref, o_ref, s_out_ref) = refs
    else:
        (zr_ref, zk_ref, zv_ref, zx_ref, mu_ref, w0_ref, ww2_ref, a0_ref, aw2_ref,
         gw2_ref, kk_ref, ka_ref, rk_ref, gng_ref, gnb_ref, o_ref, s_out_ref) = refs
    C, N, H, W, G = chunk, RW_DIM, RW_HEADS, RW_WIDTH, RW_GROUP
    L = seq_rows
    R = n_seq * L

    def initial_state(b):
        if has_state:
            return jnp.concatenate([s0_ref[b, h] for h in range(H)], axis=1)
        return jnp.zeros((N, W), F32)

    def initial_prev(b):
        return shift_ref[b] if has_state else jnp.zeros((1, RW_PROJ), F32)

    if multi_step:
        step = pl.program_id(1)

        @pl.when(step == 0)
        def _():
            for b in range(n_seq):
                s_scr[b] = initial_state(b)
                prev_scr[b:b + 1, :] = initial_prev(b)

        carried_state = lambda b: s_scr[b]
        carried_prev = lambda b, cols: prev_scr[b:b + 1, cols]
    else:
        carried_state = initial_state
        carried_prev = lambda b, cols: initial_prev(b)[:, cols]

    RB = min(RW_BATCH_GROUPS * G, R)
    n_batches = R // RB
    assert (n_seq == 1 and t_valid == L) or n_batches == 1
    assert C == G or C == L
    brow = lax.broadcasted_iota(jnp.int32, (RB, 1), 0)
    lane = lax.broadcasted_iota(jnp.int32, (1, W), 1)
    head_masks = [((lane >= h * N) & (lane < (h + 1) * N)).astype(F32) for h in range(H)]
    hr = lax.broadcasted_iota(jnp.int32, (W, W), 0)
    hc = lax.broadcasted_iota(jnp.int32, (W, W), 1)
    head_ones = ((hr // N) == (hc // N)).astype(F32)
    elem = {}

    def prologue(bi):
        lo = bi * RB
        t_in = (brow + lo) % L
        valid = t_in < t_valid

        def load(ref, part):
            cols = slice(part * W, (part + 1) * W)
            if t_valid < L:
                z = _unpack_rows(ref[...], n_seq, t_valid, L)
            else:
                z = ref[lo:lo + RB, :]
            if n_seq == 1:
                first = carried_prev(0, cols) if bi == 0 else ref[lo - 1:lo, :]
                prev = jnp.where(brow == 0, first, pltpu.roll(z, 1, axis=0))
            else:
                first = jnp.concatenate([jnp.broadcast_to(carried_prev(b, cols), (L, W))
                                         for b in range(n_seq)], axis=0)
                prev = jnp.where(t_in == 0, first, pltpu.roll(z, 1, axis=0))
            if multi_step:
                for b in range(n_seq):
                    last = b * L + t_valid - 1 - lo
                    if 0 <= last < RB:
                        prev_scr[b:b + 1, cols] = z[last:last + 1, :]
            return z + (prev - z) * mu_ref[:, cols]

        r = load(zr_ref, 0)
        k = load(zk_ref, 1)
        v = load(zv_ref, 2)
        x4 = load(zx_ref, 3)

        w_pre = w0_ref[...] + _dot(jnp.tanh(x4), ww2_ref[...])
        yield
        nw = -w_pre
        softplus = jnp.maximum(nw, 0.0) + jnp.log(1.0 + jnp.exp(-jnp.abs(nw)))
        lw = -jnp.exp(-softplus - 0.5)
        a = jax.nn.sigmoid(a0_ref[...] + _dot(x4, aw2_ref[...]))
        yield
        gate = _dot(jax.nn.sigmoid(x4), gw2_ref[...])
        yield
        kk = k * kk_ref[...]
        k = k * (1.0 + (a - 1.0) * ka_ref[...])
        sums = _dot(jnp.concatenate([kk * kk, r * k * rk_ref[...]], axis=0), head_ones)
        yield
        kk = kk / jnp.maximum(jnp.sqrt(sums[:RB]), 1e-12)
        bonus = sums[RB:] * v
        if t_valid < L:
            lw = jnp.where(valid, lw, 0.0)
            kk = jnp.where(valid, kk, 0.0)
            k = jnp.where(valid, k, 0.0)
            v = jnp.where(valid, v, 0.0)
        cum = _chunk_cumsum(lw, C)
        yield
        g_in = jnp.exp(cum)
        g_inv = jnp.exp(-cum)
        elem[bi] = dict(a_hat=-kk * jnp.exp(cum - lw), b_chk=kk * a * g_inv, k_chk=k * g_inv,
                        r_hat=r * g_in, v=v, g_in=g_in, bonus=bonus, gate=gate)

    HG_ = H * G
    sr = lax.broadcasted_iota(jnp.int32, (HG_, HG_), 0)
    sc = lax.broadcasted_iota(jnp.int32, (HG_, HG_), 1)
    same_chunk = (sr // C) == (sc // C)
    strict = same_chunk & (sc < sr)
    incl = same_chunk & (sc <= sr)
    eye = (sr == sc).astype(F32)
    n_sub = G // C

    def gather(x, sub):
        if n_sub == 1:
            return x
        return jnp.concatenate([x[h * G + sub * C:h * G + (sub + 1) * C] for h in range(H)], axis=0)

    def scatter(pieces):
        if n_sub == 1:
            return pieces[0]
        return jnp.concatenate([pieces[sub][h * C:(h + 1) * C]
                                for h in range(H) for sub in range(n_sub)], axis=0)

    pre = {}
    gis = range(RB // G)

    def precompute(bi):
        e = elem[bi]
        a_st, b_st, k_st, r_st, v_st = {}, {}, {}, {}, {}
        for gi in gis:
            rows = slice(gi * G, (gi + 1) * G)
            a_st[gi] = _head_stack(e["a_hat"][rows], head_masks)
            b_st[gi] = _head_stack(e["b_chk"][rows], head_masks)
            k_st[gi] = _head_stack(e["k_chk"][rows], head_masks)
            r_st[gi] = _head_stack(e["r_hat"][rows], head_masks)
            v_st[gi] = jnp.concatenate([e["v"][rows, h * N:(h + 1) * N] for h in range(H)], axis=0)
        m_ab = {gi: jnp.where(strict, _dot(a_st[gi], b_st[gi], _NT), 0.0) for gi in gis}
        yield
        m_ak = {gi: jnp.where(strict, _dot(a_st[gi], k_st[gi], _NT), 0.0) for gi in gis}
        yield
        p_rb = {gi: jnp.where(incl, _dot(r_st[gi], b_st[gi], _NT), 0.0) for gi in gis}
        yield
        p_rk = {gi: jnp.where(incl, _dot(r_st[gi], k_st[gi], _NT), 0.0) for gi in gis}
        yield
        t_inv = {gi: eye + m_ab[gi] for gi in gis}
        power = dict(m_ab)
        span = 2
        while span < C:
            power = {gi: _dot(power[gi], power[gi]) for gi in gis}
            yield
            t_inv = {gi: t_inv[gi] + _dot(t_inv[gi], power[gi]) for gi in gis}
            yield
            span *= 2
        w_m = {gi: _dot(t_inv[gi], a_st[gi]) for gi in gis}
        yield
        mv = {gi: _dot(m_ak[gi], v_st[gi]) for gi in gis}
        yield
        u_m = {gi: _dot(t_inv[gi], mv[gi]) for gi in gis}
        yield
        vk = {gi: [_dot(gather(v_st[gi], sub), gather(k_st[gi], sub), _TN) for sub in range(n_sub)]
              for gi in gis}
        yield
        for gi in gis:
            pre[bi, gi] = dict(b_s=b_st[gi], r_s=r_st[gi], v_s=v_st[gi], p_rb=p_rb[gi], p_rk=p_rk[gi],
                               w_m=w_m[gi], u_m=u_m[gi], vk=vk[gi])

    states = [carried_state(b) for b in range(n_seq)]

    def chain(bi):
        e = elem[bi]
        outs = []
        for gi in gis:
            g = pre[bi, gi]
            subs = range(n_sub)
            seqs = [(bi * RB + gi * G + sub * C) // L for sub in subs]
            old = [states[b] for b in seqs]
            c_parts = [_dot(gather(g["w_m"], sub), old[sub], _NT) + gather(g["u_m"], sub)
                       for sub in subs]
            yield
            upd = [_dot(c_parts[sub], gather(g["b_s"], sub), _TN) for sub in subs]
            for sub in subs:
                last = gi * G + sub * C + C - 1
                states[seqs[sub]] = (old[sub] + upd[sub] + g["vk"][sub]) * e["g_in"][last:last + 1, :]
            yield
            rs_parts = [_dot(gather(g["r_s"], sub), old[sub], _NT) for sub in subs]
            yield
            o_s = scatter(rs_parts) + _dot(jnp.concatenate([g["p_rb"], g["p_rk"]], axis=1),
                                           jnp.concatenate([scatter(c_parts), g["v_s"]], axis=0))
            mu_o = jnp.mean(o_s, axis=-1, keepdims=True)
            oc = o_s - mu_o
            var_o = jnp.mean(oc * oc, axis=-1, keepdims=True)
            on = oc * lax.rsqrt(var_o + RW_GN_EPS)
            outs.append(jnp.concatenate([on[h * G:(h + 1) * G, :] for h in range(H)], axis=1))
            yield
        o = outs[0] if len(outs) == 1 else jnp.concatenate(outs, axis=0)
        o = ((o * gng_ref[...] + gnb_ref[...] + e["bonus"]) * e["gate"]).astype(o_ref.dtype)
        if t_valid < L:
            o_ref[...] = _pack_rows(o, n_seq, t_valid, L)
        else:
            o_ref[bi * RB:(bi + 1) * RB, :] = o

    def run_interleaved(*gens):
        live = list(gens)
        while live:
            for gen in list(live):
                if next(gen, StopIteration) is StopIteration:
                    live.remove(gen)

    run_interleaved(prologue(0))
    for bi in range(n_batches):
        run_interleaved(*([precompute(bi)]
                          + ([prologue(bi + 1)] if bi + 1 < n_batches else [])
                          + ([chain(bi - 1)] if bi > 0 else [])))
    run_interleaved(chain(n_batches - 1))

    def write_final_states():
        if layer:
            s_out_ref[:layer] = earlier_ref[...]
        for b in range(n_seq):
            for h in range(H):
                s_out_ref[layer, b, h] = states[b][:, h * N:(h + 1) * N]

    if multi_step:
        for b in range(n_seq):
            s_scr[b] = states[b]
        pl.when(step == pl.num_programs(1) - 1)(write_final_states)
    else:
        write_final_states()


def _seq_grid(n_rows, n_batch, n_seq, step_rows):
    rows_per_seq = n_rows // n_batch
    time_steps = rows_per_seq // step_rows
    assert n_seq == 1 or time_steps == 1
    return (n_batch // n_seq, time_steps), (lambda i, t: i * time_steps + t)


def _layer_state_call(kern, layer, earlier, state_shape, n_seq, **kwargs):
    tail = (0,) * (len(state_shape) - 1)
    block = lambda depth: pl.BlockSpec((depth, n_seq) + state_shape[1:], lambda i, t: (0, i) + tail)
    layer_block = pl.BlockSpec((None, n_seq) + state_shape[1:], lambda i, t: (layer, i) + tail)

    def call(args, specs, o_spec, o_shape):
        if layer:
            args = [earlier] + args
            specs = [block(layer)] + specs
        return pl.pallas_call(
            functools.partial(kern, layer=layer),
            in_specs=specs,
            out_specs=[o_spec, block(layer + 1)],
            out_shape=[o_shape, jax.ShapeDtypeStruct((layer + 1,) + state_shape, F32)],
            **kwargs,
        )(*args)

    return layer_block, call


def _rwkv(z, shift0, s0, states_out, layer, p, n_batch, t_valid, chunk, n_seq, seq_rows):
    W = RW_WIDTH
    has_state = s0 is not None
    grid, row_block = _seq_grid(z.shape[0], n_batch, n_seq, t_valid)
    R = n_seq * t_valid
    zcol = lambda j: pl.BlockSpec((R, W), lambda i, t, j=j: (row_block(i, t), j))
    weights = (p["mu"], p["w0"], p["w_w2"], p["a0"], p["a_w2"], p["g_w2"], p["k_k"], p["k_a"],
               p["r_k"], p["gn_g"], p["gn_b"])
    multi_step = grid[1] > 1
    kern = functools.partial(_rwkv_kernel, chunk=chunk, t_valid=t_valid, n_seq=n_seq,
                             seq_rows=seq_rows, has_state=has_state, multi_step=multi_step)
    state_spec, call = _layer_state_call(
        kern, layer, states_out, (n_batch, RW_HEADS, RW_DIM, RW_DIM), n_seq,
        grid=grid,
        scratch_shapes=[pltpu.VMEM((n_seq, RW_DIM, W), F32),
                        pltpu.VMEM((n_seq, RW_PROJ), F32)] if multi_step else [],
        compiler_params=_params(("parallel", "arbitrary")),
        name="rwkv7")
    args, specs = [z, z, z, z], [zcol(0), zcol(1), zcol(2), zcol(3)]
    if has_state:
        args += [shift0, s0]
        specs += [pl.BlockSpec((None, n_seq, 1, RW_PROJ), lambda i, t: (layer, i, 0, 0)), state_spec]
    args += list(weights)
    specs += [_layer_resident(w, layer) for w in weights]
    return call(args, specs, pl.BlockSpec((R, W), lambda i, t: (row_block(i, t), 0)),
                jax.ShapeDtypeStruct((z.shape[0], W), BF16))


def _hgrn_kernel(*refs, chunk, t_valid, n_seq, seq_rows, has_state, multi_step, layer):
    earlier_ref, refs = (refs[0], refs[1:]) if layer else (None, refs)
    if multi_step:
        *refs, s_scr = refs
    if has_state:
        zq_ref, zf_ref, zi_ref, zg_ref, s0_ref, lb_ref, ng_ref, o_ref, s_out_ref = refs
    else:
        zq_ref, zf_ref, zi_ref, zg_ref, lb_ref, ng_ref, o_ref, s_out_ref = refs
    C, N, H = chunk, HG_DIM, HG_HEADS
    L = seq_rows
    R = n_seq * L
    sub = min(HG_SUB, C)
    if t_valid < L:
        load = lambda ref: _unpack_rows(ref[...], n_seq, t_valid, L)
    else:
        load = lambda ref: ref[...]

    def initial_state(b, h):
        return s0_ref[b, h].T if has_state else jnp.zeros((N, N), F32)

    if multi_step:
        step = pl.program_id(1)

        @pl.when(step == 0)
        def _():
            for b in range(n_seq):
                for h in range(H):
                    s_scr[b, h] = initial_state(b, h)

        carried_state = lambda b, h: s_scr[b, h]
    else:
        carried_state = initial_state

    zq = load(zq_ref)
    q = zq * jax.nn.sigmoid(zq)
    lb = lb_ref[...]
    f = lb + (1.0 - lb) * jax.nn.sigmoid(load(zf_ref))
    log_f = jnp.log(jnp.maximum(f, F_MIN))
    k = 1.0 - f
    v = load(zi_ref)
    zg = load(zg_ref)
    out_gate = ng_ref[...] * (zg * jax.nn.sigmoid(zg))
    if t_valid < L:
        valid = (lax.broadcasted_iota(jnp.int32, (R, 1), 0) % L) < t_valid
        log_f = jnp.where(valid, log_f, 0.0)
        k = jnp.where(valid, k, 0.0)

    cum_all = _chunk_cumsum(log_f, C)
    tr = lax.broadcasted_iota(jnp.int32, (C, C), 0)
    tc = lax.broadcasted_iota(jnp.int32, (C, C), 1)
    causal = tc <= tr

    units = [(c, h) for c in range(R // C) for h in range(H)]
    part = lambda x, c, h: x[c * C:(c + 1) * C, h * N:(h + 1) * N]
    cums = {u: part(cum_all, *u) for u in units}
    totals = {u: cums[u][C - 1:C, :] for u in units}
    scores, intra, updates, inter = {}, {}, {}, {}
    n_blk = C // sub
    states = [[carried_state(b, h) for h in range(H)] for b in range(n_seq)]

    def score_stage(u):
        cum, q_h, k_h = cums[u], part(q, *u), part(k, *u)
        blks = [slice(i * sub, (i + 1) * sub) for i in range(n_blk)]
        refs = [cum[i * sub:i * sub + 1, :] for i in range(n_blk)]
        k_own = [k_h[blks[j]] * jnp.exp(jnp.minimum(refs[j] - cum[blks[j]], EXP_CLAMP))
                 for j in range(n_blk)]
        score_rows = []
        for i in range(n_blk):
            q_hat = q_h[blks[i]] * jnp.exp(cum[blks[i]] - refs[i])
            pieces = [k_own[j] * jnp.exp(refs[i] - refs[j]) for j in range(i)] + [k_own[i]]
            if i + 1 < n_blk:
                pieces.append(jnp.zeros(((n_blk - 1 - i) * sub, N), F32))
            k_hat = pieces[0] if len(pieces) == 1 else jnp.concatenate(pieces, axis=0)
            score_rows.append(_dot(q_hat, k_hat, _NT))
        rows_ = score_rows[0] if len(score_rows) == 1 else jnp.concatenate(score_rows, axis=0)
        scores[u] = jnp.where(causal, rows_, 0.0)

    def intra_stage(u):
        intra[u] = _dot(scores[u], part(v, *u))
        updates[u] = _dot(part(v, *u), part(k, *u) * jnp.exp(totals[u] - cums[u]), _TN)

    def state_stage(u):
        c, h = u
        b = (c * C) // L
        inter[u] = _dot(part(q, *u) * jnp.exp(cums[u]), states[b][h], _NT)
        states[b][h] = states[b][h] * jnp.exp(totals[u]) + updates[u]

    stages = (score_stage, intra_stage, state_stage)
    for idx in range(len(units) + len(stages) - 1):
        for lag, stage in enumerate(stages):
            if 0 <= idx - lag < len(units):
                stage(units[idx - lag])
    out_rows = []
    for c in range(R // C):
        outs = []
        for h in range(H):
            o_h = intra[(c, h)] + inter[(c, h)]
            outs.append(o_h * lax.rsqrt(jnp.mean(o_h * o_h, axis=-1, keepdims=True) + RMS_EPS))
        out_rows.append(jnp.concatenate(outs, axis=1))
    o = out_rows[0] if len(out_rows) == 1 else jnp.concatenate(out_rows, axis=0)
    o = (o * out_gate).astype(o_ref.dtype)
    o_ref[...] = _pack_rows(o, n_seq, t_valid, L) if t_valid < L else o

    def write_final_states():
        if layer:
            s_out_ref[:layer] = earlier_ref[...]
        for b in range(n_seq):
            for h in range(H):
                s_out_ref[layer, b, h] = states[b][h].T

    if multi_step:
        for b in range(n_seq):
            for h in range(H):
                s_scr[b, h] = states[b][h]
        pl.when(step == pl.num_programs(1) - 1)(write_final_states)
    else:
        write_final_states()


def _hgrn(z, s0, states_out, layer, lb, norm_g, n_batch, t_valid, chunk, n_seq, seq_rows):
    W = HG_WIDTH
    first = RW_PROJ // W
    has_state = s0 is not None
    grid, row_block = _seq_grid(z.shape[0], n_batch, n_seq, t_valid)
    R = n_seq * t_valid
    zcol = lambda j: pl.BlockSpec((R, W), lambda i, t, j=j: (row_block(i, t), first + j))
    multi_step = grid[1] > 1
    kern = functools.partial(_hgrn_kernel, chunk=chunk, t_valid=t_valid, n_seq=n_seq,
                             seq_rows=seq_rows, has_state=has_state, multi_step=multi_step)
    state_spec, call = _layer_state_call(
        kern, layer, states_out, (n_batch, HG_HEADS, HG_DIM, HG_DIM), n_seq,
        grid=grid,
        scratch_shapes=[pltpu.VMEM((n_seq, HG_HEADS, HG_DIM, HG_DIM), F32)] if multi_step else [],
        compiler_params=_params(("parallel", "arbitrary")),
        name="hgrn2")
    args, specs = [z, z, z, z], [zcol(0), zcol(1), zcol(2), zcol(3)]
    if has_state:
        args.append(s0)
        specs.append(state_spec)
    args += [lb, norm_g]
    specs += [_layer_resident(lb, layer), _layer_resident(norm_g, layer)]
    return call(args, specs, pl.BlockSpec((R, W), lambda i, t: (row_block(i, t), 0)),
                jax.ShapeDtypeStruct((z.shape[0], W), BF16))


def _gelu(x):
    return 0.5 * x * (1.0 + lax.erf(x * (2.0 ** -0.5)))


def _cm_kernel(zu_ref, zv_ref, ws_ref, bs_ref, g_ref, b_ref, o_ref, *v_out):
    H, N, W, C = CM_HEADS, CM_DIM, CM_WIDTH, CM_CHUNK
    u = _gelu(zu_ref[...])
    v = _gelu(zv_ref[...])
    hr = lax.broadcasted_iota(jnp.int32, (W, W), 0)
    hc = lax.broadcasted_iota(jnp.int32, (W, W), 1)
    head_mean = jnp.where((hr // N) == (hc // N), 1.0 / N, 0.0).astype(F32)
    vc = v - _dot(v, head_mean)
    var = _dot(vc * vc, head_mean)
    vn = vc * lax.rsqrt(var + LN_EPS) * g_ref[...] + b_ref[...]
    if v_out:
        v_out[0][...] = vn
    tr = lax.broadcasted_iota(jnp.int32, (C, C), 0)
    tc = lax.broadcasted_iota(jnp.int32, (C, C), 1)
    lane = lax.broadcasted_iota(jnp.int32, (1, W), 1)
    w_causal = [jnp.where(tc <= tr, ws_ref[h], 0.0).astype(BF16) for h in range(H)]
    head_cols = [(lane >= h * N) & (lane < (h + 1) * N) for h in range(H)]
    vb = vn.astype(BF16)
    mixed = []
    for c in range(zu_ref.shape[0] // C):
        v_c = vb[c * C:(c + 1) * C]
        acc = bs_ref[...]
        for h in range(H):
            acc = acc + jnp.dot(w_causal[h], jnp.where(head_cols[h], v_c, jnp.zeros_like(v_c)),
                                preferred_element_type=F32)
        mixed.append(acc)
    mixed = mixed[0] if len(mixed) == 1 else jnp.concatenate(mixed, axis=0)
    o_ref[...] = (u * mixed).astype(o_ref.dtype)


def _chunk_mlp(z, layer, ws, bs_wide, ln_g, ln_b, want_v):
    n, W = z.shape[0], CM_WIDTH
    rows = min(CM_STEP_ROWS, n)
    first = (RW_PROJ + HG_PROJ) // W
    zcol = lambda j: pl.BlockSpec((rows, W), lambda i, j=j: (i, first + j))
    out_block = pl.BlockSpec((rows, W), lambda i: (i, 0))
    out_specs, out_shape = [out_block], [jax.ShapeDtypeStruct((n, W), BF16)]
    if want_v:
        out_specs.append(out_block)
        out_shape.append(jax.ShapeDtypeStruct((n, W), F32))
    return pl.pallas_call(
        _cm_kernel,
        grid=(n // rows,),
        in_specs=[zcol(0), zcol(1)] + [_layer_resident(w, layer) for w in (ws, bs_wide, ln_g, ln_b)],
        out_specs=out_specs,
        out_shape=out_shape,
        compiler_params=_params(("parallel",)),
        name="chunk_gmlp",
    )(z, z, ws, bs_wide, ln_g, ln_b)


def _dense_weights(p):
    vec = lambda a: a[:, None, :]
    out = {k: p[k].astype(BF16) for k in ("ffn1_w_in", "ffn1_w_out", "mix_w_in", "mix_w_out",
                                           "ffn2_w_in", "ffn2_w_out")}
    out.update({k: vec(p[k]) for k in ("ln1_g", "ln1_b", "ln2_g", "ln2_b", "ln3_g", "ln3_b")})
    return out


def _mixer_weights(p, lb):
    vec = lambda a: a.reshape(DEPTH, 1, -1)
    lora_pad = lambda w, start: jnp.pad(w, ((0, 0), (start, RW_WIDTH - start - w.shape[1]), (0, 0)))
    return dict(
        rw=dict(mu=vec(p["rw_mu"]), w0=vec(p["rw_w0"]),
                w_w2=lora_pad(p["rw_w_w2"], 0), a0=vec(p["rw_a0"]),
                a_w2=lora_pad(p["rw_a_w2"], RW_DECAY_LORA),
                g_w2=lora_pad(p["rw_g_w2"], RW_DECAY_LORA + RW_AAA_LORA),
                k_k=vec(p["rw_k_k"]), k_a=vec(p["rw_k_a"]), r_k=vec(p["rw_r_k"]),
                gn_g=vec(p["rw_gn_g"]), gn_b=vec(p["rw_gn_b"])),
        hg_lb=vec(lb), hg_norm_g=vec(p["hg_norm_g"]),
        cm_ws=p["cm_ws"],
        cm_bs=jnp.repeat(jnp.swapaxes(p["cm_bs"], 1, 2), CM_DIM, axis=2),
        cm_ln_g=vec(p["cm_ln_g"]), cm_ln_b=vec(p["cm_ln_b"]),
    )


def _short_chunk_mixing(ws, bs_wide, t_pad):
    reps = CM_CHUNK // t_pad
    pos = jnp.arange(CM_CHUNK)
    sel = (pos[None, :] == (pos % t_pad)[:, None]).astype(ws.dtype)
    tiled = jnp.einsum("rk,lhkm,cm->lhrc", sel, ws, sel, precision=lax.Precision.HIGHEST)
    same_seq = (pos[:, None] // t_pad) == (pos[None, :] // t_pad)
    return jnp.where(same_seq, tiled, 0.0), jnp.tile(bs_wide[:, :t_pad], (1, reps, 1))


class _TokenGroup:
    def __init__(self, x, rw_s0, rw_shift0, hg_s0, w):
        self.B, self.T, _ = x.shape
        self.x = x.reshape(self.B * self.T, D_MODEL)
        self.has_state = rw_s0 is not None
        self.rw_s0, self.hg_s0 = rw_s0, hg_s0
        self.rw_shift0 = rw_shift0[:, :, None, :] if self.has_state else None
        T = self.T
        self.short = T < RW_CHUNK
        if self.short:
            slot = -(-T // SUBLANES) * SUBLANES
            self.rw_chunk = self.hg_chunk = self.rw_rows = self.hg_rows = slot
            self.n_seq = SAMPLE_STEP_SEQS
            self.cm_ws, self.cm_bs = _short_chunk_mixing(w["cm_ws"], w["cm_bs"], T)
        else:
            self.rw_chunk, self.hg_chunk, self.n_seq = RW_CHUNK, HG_CHUNK, 1
            self.rw_rows, self.hg_rows = min(T, RW_STEP_ROWS), min(T, HG_STEP_ROWS)
            self.cm_ws, self.cm_bs = w["cm_ws"], w["cm_bs"]
        self.rw_states, self.hg_states, self.rw_shifts, self.cm_vs = None, None, [], []

    def mix(self, z, layer, w):
        B, T = self.B, self.T
        self.rw_shifts.append(lax.slice(z, (T - 1, 0), (B * T, RW_PROJ), (T, 1)))
        o_rw, self.rw_states = _rwkv(z, self.rw_shift0, self.rw_s0, self.rw_states, layer, w["rw"], B,
                                     min(T, self.rw_rows), self.rw_chunk, self.n_seq, self.rw_rows)
        o_hg, self.hg_states = _hgrn(z, self.hg_s0, self.hg_states, layer, w["hg_lb"], w["hg_norm_g"],
                                     B, min(T, self.hg_rows), self.hg_chunk, self.n_seq, self.hg_rows)
        cm_out = _chunk_mlp(z, layer, self.cm_ws, self.cm_bs, w["cm_ln_g"], w["cm_ln_b"],
                            want_v=self.has_state)
        if self.has_state:
            self.cm_vs.append(cm_out[1].reshape(B, T, CM_WIDTH))
        return o_rw, o_hg, cm_out[0]


def _run_trunk(groups, dw, w):
    for l in range(DEPTH):
        projected = _dense_in([g.x for g in groups], l, dw["ffn1_w_in"], dw["ffn1_w_out"],
                              dw["ln1_g"], dw["ln1_b"], dw["mix_w_in"])
        mixed = [(x1,) + g.mix(z, l, w) for g, (x1, z) in zip(groups, projected)]
        ys = _dense_out(mixed, l, dw["mix_w_out"], dw["ln2_g"], dw["ln2_b"], dw["ffn2_w_in"],
                        dw["ffn2_w_out"], dw["ln3_g"], dw["ln3_b"])
        for g, y in zip(groups, ys):
            g.x = y


def kernel(x_prompt, x_sample, state_rwkv, state_rwkv_shift, state_hgrn, ffn1_w_in, ffn1_w_out, ln1_g, ln1_b, mix_w_in, mix_w_out, ln2_g, ln2_b, rw_mu, rw_w0, rw_w_w2, rw_a0, rw_a_w2, rw_g_w2, rw_k_k, rw_k_a, rw_r_k, rw_gn_g, rw_gn_b, hg_lb_logits, hg_norm_g, cm_ws, cm_bs, cm_ln_g, cm_ln_b, ffn2_w_in, ffn2_w_out, ln3_g, ln3_b):
    p = dict(ffn1_w_in=ffn1_w_in, ffn1_w_out=ffn1_w_out, ln1_g=ln1_g, ln1_b=ln1_b,
             mix_w_in=mix_w_in, mix_w_out=mix_w_out, ln2_g=ln2_g, ln2_b=ln2_b,
             rw_mu=rw_mu, rw_w0=rw_w0, rw_w_w2=rw_w_w2, rw_a0=rw_a0, rw_a_w2=rw_a_w2,
             rw_g_w2=rw_g_w2, rw_k_k=rw_k_k, rw_k_a=rw_k_a, rw_r_k=rw_r_k,
             rw_gn_g=rw_gn_g, rw_gn_b=rw_gn_b, hg_norm_g=hg_norm_g,
             cm_ws=cm_ws, cm_bs=cm_bs, cm_ln_g=cm_ln_g, cm_ln_b=cm_ln_b,
             ffn2_w_in=ffn2_w_in, ffn2_w_out=ffn2_w_out, ln3_g=ln3_g, ln3_b=ln3_b)
    s = jax.nn.softmax(hg_lb_logits.astype(F32), axis=0)
    lb = jnp.cumsum(s, axis=0) - s[0]
    weights = _mixer_weights(p, lb)
    dw = _dense_weights(p)
    prompt = _TokenGroup(x_prompt, None, None, None, weights)
    sample = _TokenGroup(x_sample, state_rwkv, state_rwkv_shift, state_hgrn, weights)
    _run_trunk([prompt, sample], dw, weights)
    return (prompt.x.reshape(x_prompt.shape), sample.x.reshape(x_sample.shape),
            prompt.rw_states, jnp.stack(prompt.rw_shifts), prompt.hg_states,
            sample.rw_states, jnp.stack(sample.rw_shifts), sample.hg_states,
            jnp.stack(sample.cm_vs))
```

```python
import functools

import jax
import jax.numpy as jnp
from jax import lax
from jax.experimental import pallas as pl
from jax.experimental.pallas import tpu as pltpu

F32 = jnp.float32
BF16 = jnp.bfloat16

D_MODEL = 1024
DEPTH = 2
RW_HEADS, RW_DIM = 4, 64
RW_WIDTH = RW_HEADS * RW_DIM
RW_DECAY_LORA, RW_AAA_LORA, RW_GATE_LORA = 64, 64, 128
RW_PROJ = 3 * RW_WIDTH + RW_DECAY_LORA + RW_AAA_LORA + RW_GATE_LORA
RW_GN_EPS = 64e-5
HG_HEADS, HG_DIM = 4, 128
HG_WIDTH = HG_HEADS * HG_DIM
HG_PROJ = 4 * HG_WIDTH
RMS_EPS = 1e-6
F_MIN = 1e-30
CM_HEADS, CM_DIM = 4, 64
CM_WIDTH = CM_HEADS * CM_DIM
CM_CHUNK = 128
CM_PROJ = 2 * CM_WIDTH
MIX_WIDTH = RW_WIDTH + HG_WIDTH + CM_WIDTH
IN_PROJ = RW_PROJ + HG_PROJ + CM_PROJ
D_FF = 2816
LN_EPS = 1e-5
ALPHA = (2.0 * DEPTH) ** 0.25

VMEM_LIMIT_BYTES = 56 * 1024 * 1024
SUBLANES = 8
MXU_DIM = 256
DENSE_ROWS = 512
DENSE_SUB_ROWS = 256
FF_CHUNK = 1408
RW_CHUNK = 64
RW_GROUP = 64
HG_CHUNK = 64
HG_SUB = 16
EXP_CLAMP = 80.0
RW_BATCH_GROUPS = 4
RW_STEP_ROWS = 1024
HG_STEP_ROWS = 512
SAMPLE_STEP_SEQS = 16
CM_STEP_ROWS = 1024


def _params(semantics):
    return pltpu.CompilerParams(dimension_semantics=semantics,
                                vmem_limit_bytes=VMEM_LIMIT_BYTES)


_NN = ((1,), (0,))
_NT = ((1,), (1,))
_TN = ((0,), (0,))


def _dot(a, b, dims=_NN):
    return lax.dot_general(a.astype(BF16), b.astype(BF16), (dims, ((), ())),
                           preferred_element_type=F32)


def _mask_dot(mask, x):
    m = mask.astype(BF16)
    hi = x.astype(BF16)
    r1 = x - hi.astype(F32)
    mid = r1.astype(BF16)
    lo = (r1 - mid.astype(F32)).astype(BF16)
    dot = lambda t: jnp.dot(m, t, preferred_element_type=F32)
    return dot(hi) + (dot(mid) + dot(lo))


def _chunk_cumsum(x, chunk):
    n = min(x.shape[0], MXU_DIM)
    r = lax.broadcasted_iota(jnp.int32, (n, n), 0)
    c = lax.broadcasted_iota(jnp.int32, (n, n), 1)
    mask = ((r // chunk) == (c // chunk)) & (c <= r)
    pieces = [_mask_dot(mask, x[lo:lo + n]) for lo in range(0, x.shape[0], n)]
    return pieces[0] if len(pieces) == 1 else jnp.concatenate(pieces, axis=0)


def _slot_matrix(n_seq, t_valid, slot, transpose=False):
    shape = (n_seq * slot, n_seq * t_valid)
    padded = lax.broadcasted_iota(jnp.int32, shape[::-1] if transpose else shape, 1 if transpose else 0)
    packed = lax.broadcasted_iota(jnp.int32, shape[::-1] if transpose else shape, 0 if transpose else 1)
    return ((padded // slot) == (packed // t_valid)) & ((padded % slot) == (packed % t_valid))


def _unpack_rows(x, n_seq, t_valid, slot):
    return _mask_dot(_slot_matrix(n_seq, t_valid, slot), x)


def _pack_rows(x, n_seq, t_valid, slot):
    sel = _slot_matrix(n_seq, t_valid, slot, transpose=True).astype(BF16)
    return jnp.dot(sel, x, preferred_element_type=F32).astype(x.dtype)


def _layer_norm(x, g, b):
    mu = jnp.mean(x, axis=-1, keepdims=True)
    xc = x - mu
    var = jnp.mean(xc * xc, axis=-1, keepdims=True)
    return xc * lax.rsqrt(var + LN_EPS) * g + b


def _swiglu(xb, w_in_ref, w_out_ref):
    acc = None
    for lo in range(0, D_FF, FF_CHUNK):
        gate = jnp.dot(xb, w_in_ref[:, lo:lo + FF_CHUNK], preferred_element_type=F32)
        up = jnp.dot(xb, w_in_ref[:, D_FF + lo:D_FF + lo + FF_CHUNK], preferred_element_type=F32)
        h = (gate * jax.nn.sigmoid(gate) * up).astype(BF16)
        part = jnp.dot(h, w_out_ref[lo:lo + FF_CHUNK, :], preferred_element_type=F32)
        acc = part if acc is None else acc + part
    return acc


def _sub_tiles(n_rows):
    sub = min(DENSE_SUB_ROWS, n_rows)
    return [slice(lo, lo + sub) for lo in range(0, n_rows, sub)]


def _layer_resident(stacked, layer):
    tail = (0,) * (stacked.ndim - 1)
    return pl.BlockSpec((None,) + stacked.shape[1:], lambda *_: (layer,) + tail,
                        pipeline_mode=pl.Buffered(1))


def _grouped_row_call(body, name, groups, in_widths, out_widths, weights, layer):
    rows = min(DENSE_ROWS, min(g[0].shape[0] for g in groups))
    steps = [g[0].shape[0] // rows for g in groups]
    starts = [sum(steps[:k]) for k in range(len(groups))]
    n_in, n_out, n_w = len(in_widths), len(out_widths), len(weights)

    def kern(*refs):
        ins, w_refs, outs = refs[:len(groups) * n_in], refs[len(groups) * n_in:][:n_w], refs[-len(groups) * n_out:]
        step = pl.program_id(0)
        for k in range(len(groups)):
            run = functools.partial(body, *ins[k * n_in:(k + 1) * n_in], *w_refs,
                                    *outs[k * n_out:(k + 1) * n_out])
            if len(groups) == 1:
                run()
            else:
                pl.when((step >= starts[k]) & (step < starts[k] + steps[k]))(run)

    def block(width, k):
        return pl.BlockSpec((rows, width), lambda i: (jnp.clip(i - starts[k], 0, steps[k] - 1), 0))

    outs = pl.pallas_call(
        kern,
        grid=(sum(steps),),
        in_specs=[block(wd, k) for k in range(len(groups)) for wd in in_widths]
                 + [_layer_resident(w, layer) for w in weights],
        out_specs=[block(wd, k) for k in range(len(groups)) for wd in out_widths],
        out_shape=[jax.ShapeDtypeStruct((g[0].shape[0], wd), F32) for g in groups for wd in out_widths],
        compiler_params=_params(("arbitrary",)),
        name=name,
    )(*[a for g in groups for a in g], *weights)
    return [tuple(outs[k * n_out:(k + 1) * n_out]) for k in range(len(groups))]


def _dense_in_body(x_ref, w_in_ref, w_out_ref, g_ref, b_ref, w_mix_ref, x1_ref, z_ref):
    tiles = _sub_tiles(x_ref.shape[0])
    xs = [x_ref[t, :] for t in tiles]
    ffn = [_swiglu(x.astype(BF16), w_in_ref, w_out_ref) for x in xs]
    ys = [_layer_norm(ALPHA * x + 0.5 * f, g_ref[...], b_ref[...]) for x, f in zip(xs, ffn)]
    for t, y in zip(tiles, ys):
        x1_ref[t, :] = y
        z_ref[t, :] = jnp.dot(y.astype(BF16), w_mix_ref[...], preferred_element_type=F32)


def _dense_in(xs, layer, w_in, w_out, g, b, w_mix):
    return [_grouped_row_call(_dense_in_body, "dense_in", [(x,)], (D_MODEL,), (D_MODEL, IN_PROJ),
                              (w_in, w_out, g, b, w_mix), layer)[0] for x in xs]


def _dense_out_body(x_ref, orw_ref, ohg_ref, ocm_ref, wmix_ref, g2_ref, b2_ref,
                    w_in_ref, w_out_ref, g3_ref, b3_ref, y_ref):
    hg0, cm0 = RW_WIDTH, RW_WIDTH + HG_WIDTH
    tiles = _sub_tiles(x_ref.shape[0])
    mix = [jnp.dot(orw_ref[t, :], wmix_ref[:hg0, :], preferred_element_type=F32)
           + jnp.dot(ohg_ref[t, :], wmix_ref[hg0:cm0, :], preferred_element_type=F32)
           + jnp.dot(ocm_ref[t, :], wmix_ref[cm0:, :], preferred_element_type=F32) for t in tiles]
    x2 = [_layer_norm(ALPHA * x_ref[t, :] + m, g2_ref[...], b2_ref[...]) for t, m in zip(tiles, mix)]
    ffn = [_swiglu(x.astype(BF16), w_in_ref, w_out_ref) for x in x2]
    for t, x, f in zip(tiles, x2, ffn):
        y_ref[t, :] = _layer_norm(ALPHA * x + 0.5 * f, g3_ref[...], b3_ref[...])


def _dense_out(groups, layer, w_mix, g2, b2, w_in, w_out, g3, b3):
    ys = _grouped_row_call(_dense_out_body, "dense_out", groups,
                           (D_MODEL, RW_WIDTH, HG_WIDTH, CM_WIDTH), (D_MODEL,),
                           (w_mix, g2, b2, w_in, w_out, g3, b3), layer)
    return [y for (y,) in ys]


def _head_stack(x, head_masks):
    return jnp.concatenate([x * m for m in head_masks], axis=0)


def _rwkv_kernel(*refs, chunk, t_valid, n_seq, seq_rows, has_state, multi_step, layer):
    earlier_ref, refs = (refs[0], refs[1:]) if layer else (None, refs)
    if multi_step:
        *refs, s_scr, prev_scr = refs
    if has_state:
        (zr_ref, zk_ref, zv_ref, zx_ref, shift_ref, s0_ref, mu_ref, w0_ref, ww2_ref, a0_ref, aw2_ref,
         gw2_ref, kk_ref, ka_ref, rk_ref, gng_ref, gnb_ref, o_ref, last_ref, s_out_ref) = refs
    else:
        (zr_ref, zk_ref, zv_ref, zx_ref, mu_ref, w0_ref, ww2_ref, a0_ref, aw2_ref,
         gw2_ref, kk_ref, ka_ref, rk_ref, gng_ref, gnb_ref, o_ref, last_ref, s_out_ref) = refs
    C, N, H, W, G = chunk, RW_DIM, RW_HEADS, RW_WIDTH, RW_GROUP
    L = seq_rows
    R = n_seq * L

    def initial_state(b):
        if has_state:
            return jnp.concatenate([s0_ref[b, h] for h in range(H)], axis=1)
        return jnp.zeros((N, W), F32)

    def initial_prev(b):
        return shift_ref[b] if has_state else jnp.zeros((1, RW_PROJ), F32)

    if multi_step:
        step = pl.program_id(1)

        @pl.when(step == 0)
        def _():
            for b in range(n_seq):
                s_scr[b] = initial_state(b)
                prev_scr[b:b + 1, :] = initial_prev(b)

        carried_state = lambda b: s_scr[b]
        carried_prev = lambda b, cols: prev_scr[b:b + 1, cols]
    else:
        carried_state = initial_state
        carried_prev = lambda b, cols: initial_prev(b)[:, cols]

    RB = min(RW_BATCH_GROUPS * G, R)
    n_batches = R // RB
    assert (n_seq == 1 and t_valid == L) or n_batches == 1
    assert C == G or C == L
    brow = lax.broadcasted_iota(jnp.int32, (RB, 1), 0)
    lane = lax.broadcasted_iota(jnp.int32, (1, W), 1)
    head_masks = [((lane >= h * N) & (lane < (h + 1) * N)).astype(F32) for h in range(H)]
    hr = lax.broadcasted_iota(jnp.int32, (W, W), 0)
    hc = lax.broadcasted_iota(jnp.int32, (W, W), 1)
    head_ones = ((hr // N) == (hc // N)).astype(F32)
    elem = {}

    def prologue(bi):
        lo = bi * RB
        t_in = (brow + lo) % L
        valid = t_in < t_valid

        def load(ref, part):
            cols = slice(part * W, (part + 1) * W)
            if t_valid < L:
                z = _unpack_rows(ref[...], n_seq, t_valid, L)
            else:
                z = ref[lo:lo + RB, :]
            if n_seq == 1:
                first = carried_prev(0, cols) if bi == 0 else ref[lo - 1:lo, :]
                prev = jnp.where(brow == 0, first, pltpu.roll(z, 1, axis=0))
            else:
                first = jnp.concatenate([jnp.broadcast_to(carried_prev(b, cols), (L, W))
                                         for b in range(n_seq)], axis=0)
                prev = jnp.where(t_in == 0, first, pltpu.roll(z, 1, axis=0))
            if multi_step:
                for b in range(n_seq):
                    last = b * L + t_valid - 1 - lo
                    if 0 <= last < RB:
                        prev_scr[b:b + 1, cols] = z[last:last + 1, :]
            return z + (prev - z) * mu_ref[:, cols]

        r = load(zr_ref, 0)
        k = load(zk_ref, 1)
        v = load(zv_ref, 2)
        x4 = load(zx_ref, 3)

        w_pre = w0_ref[...] + _dot(jnp.tanh(x4), ww2_ref[...])
        yield
        nw = -w_pre
        softplus = jnp.maximum(nw, 0.0) + jnp.log(1.0 + jnp.exp(-jnp.abs(nw)))
        lw = -jnp.exp(-softplus - 0.5)
        a = jax.nn.sigmoid(a0_ref[...] + _dot(x4, aw2_ref[...]))
        yield
        gate = _dot(jax.nn.sigmoid(x4), gw2_ref[...])
        yield
        kk = k * kk_ref[...]
        k = k * (1.0 + (a - 1.0) * ka_ref[...])
        sums = _dot(jnp.concatenate([kk * kk, r * k * rk_ref[...]], axis=0), head_ones)
        yield
        kk = kk / jnp.maximum(jnp.sqrt(sums[:RB]), 1e-12)
        bonus = sums[RB:] * v
        if t_valid < L:
            lw = jnp.where(valid, lw, 0.0)
            kk = jnp.where(valid, kk, 0.0)
            k = jnp.where(valid, k, 0.0)
            v = jnp.where(valid, v, 0.0)
        cum = _chunk_cumsum(lw, C)
        yield
        g_in = jnp.exp(cum)
        g_inv = jnp.exp(-cum)
        elem[bi] = dict(a_hat=-kk * jnp.exp(cum - lw), b_chk=kk * a * g_inv, k_chk=k * g_inv,
                        r_hat=r * g_in, v=v, g_in=g_in, bonus=bonus, gate=gate)

    HG_ = H * G
    sr = lax.broadcasted_iota(jnp.int32, (HG_, HG_), 0)
    sc = lax.broadcasted_iota(jnp.int32, (HG_, HG_), 1)
    same_chunk = (sr // C) == (sc // C)
    strict = same_chunk & (sc < sr)
    incl = same_chunk & (sc <= sr)
    eye = (sr == sc).astype(F32)
    n_sub = G // C

    def gather(x, sub):
        if n_sub == 1:
            return x
        return jnp.concatenate([x[h * G + sub * C:h * G + (sub + 1) * C] for h in range(H)], axis=0)

    def scatter(pieces):
        if n_sub == 1:
            return pieces[0]
        return jnp.concatenate([pieces[sub][h * C:(h + 1) * C]
                                for h in range(H) for sub in range(n_sub)], axis=0)

    pre = {}
    gis = range(RB // G)

    def precompute(bi):
        e = elem[bi]
        a_st, b_st, k_st, r_st, v_st = {}, {}, {}, {}, {}
        for gi in gis:
            rows = slice(gi * G, (gi + 1) * G)
            a_st[gi] = _head_stack(e["a_hat"][rows], head_masks)
            b_st[gi] = _head_stack(e["b_chk"][rows], head_masks)
            k_st[gi] = _head_stack(e["k_chk"][rows], head_masks)
            r_st[gi] = _head_stack(e["r_hat"][rows], head_masks)
            v_st[gi] = jnp.concatenate([e["v"][rows, h * N:(h + 1) * N] for h in range(H)], axis=0)
        m_ab = {gi: jnp.where(strict, _dot(a_st[gi], b_st[gi], _NT), 0.0) for gi in gis}
        yield
        m_ak = {gi: jnp.where(strict, _dot(a_st[gi], k_st[gi], _NT), 0.0) for gi in gis}
        yield
        p_rb = {gi: jnp.where(incl, _dot(r_st[gi], b_st[gi], _NT), 0.0) for gi in gis}
        yield
        p_rk = {gi: jnp.where(incl, _dot(r_st[gi], k_st[gi], _NT), 0.0) for gi in gis}
        yield
        t_inv = {gi: eye + m_ab[gi] for gi in gis}
        power = dict(m_ab)
        span = 2
        while span < C:
            power = {gi: _dot(power[gi], power[gi]) for gi in gis}
            yield
            t_inv = {gi: t_inv[gi] + _dot(t_inv[gi], power[gi]) for gi in gis}
            yield
            span *= 2
        w_m = {gi: _dot(t_inv[gi], a_st[gi]) for gi in gis}
        yield
        mv = {gi: _dot(m_ak[gi], v_st[gi]) for gi in gis}
        yield
        u_m = {gi: _dot(t_inv[gi], mv[gi]) for gi in gis}
        yield
        vk = {gi: [_dot(gather(v_st[gi], sub), gather(k_st[gi], sub), _TN) for sub in range(n_sub)]
              for gi in gis}
        yield
        for gi in gis:
            pre[bi, gi] = dict(b_s=b_st[gi], r_s=r_st[gi], v_s=v_st[gi], p_rb=p_rb[gi], p_rk=p_rk[gi],
                               w_m=w_m[gi], u_m=u_m[gi], vk=vk[gi])

    states = [carried_state(b) for b in range(n_seq)]

    def chain(bi):
        e = elem[bi]
        outs = []
        for gi in gis:
            g = pre[bi, gi]
            subs = range(n_sub)
            seqs = [(bi * RB + gi * G + sub * C) // L for sub in subs]
            old = [states[b] for b in seqs]
            c_parts = [_dot(gather(g["w_m"], sub), old[sub], _NT) + gather(g["u_m"], sub)
                       for sub in subs]
            yield
            upd = [_dot(c_parts[sub], gather(g["b_s"], sub), _TN) for sub in subs]
            for sub in subs:
                last = gi * G + sub * C + C - 1
                states[seqs[sub]] = (old[sub] + upd[sub] + g["vk"][sub]) * e["g_in"][last:last + 1, :]
            yield
            rs_parts = [_dot(gather(g["r_s"], sub), old[sub], _NT) for sub in subs]
            yield
            o_s = scatter(rs_parts) + _dot(jnp.concatenate([g["p_rb"], g["p_rk"]], axis=1),
                                           jnp.concatenate([scatter(c_parts), g["v_s"]], axis=0))
            mu_o = jnp.mean(o_s, axis=-1, keepdims=True)
            oc = o_s - mu_o
            var_o = jnp.mean(oc * oc, axis=-1, keepdims=True)
            on = oc * lax.rsqrt(var_o + RW_GN_EPS)
            outs.append(jnp.concatenate([on[h * G:(h + 1) * G, :] for h in range(H)], axis=1))
            yield
        o = outs[0] if len(outs) == 1 else jnp.concatenate(outs, axis=0)
        o = ((o * gng_ref[...] + gnb_ref[...] + e["bonus"]) * e["gate"]).astype(o_ref.dtype)
        if t_valid < L:
            o_ref[...] = _pack_rows(o, n_seq, t_valid, L)
        else:
            o_ref[bi * RB:(bi + 1) * RB, :] = o

    def run_interleaved(*gens):
        live = list(gens)
        while live:
            for gen in list(live):
                if next(gen, StopIteration) is StopIteration:
                    live.remove(gen)

    run_interleaved(prologue(0))
    for bi in range(n_batches):
        run_interleaved(*([precompute(bi)]
                          + ([prologue(bi + 1)] if bi + 1 < n_batches else [])
                          + ([chain(bi - 1)] if bi > 0 else [])))
    run_interleaved(chain(n_batches - 1))

    def write_final_states():
        if layer:
            s_out_ref[:layer] = earlier_ref[...]
        for b in range(n_seq):
            for h in range(H):
                s_out_ref[layer, b, h] = states[b][:, h * N:(h + 1) * N]
            row = (b + 1) * t_valid - 1
            for part, ref in enumerate((zr_ref, zk_ref, zv_ref, zx_ref)):
                last_ref[b, :, part * W:(part + 1) * W] = ref[row:row + 1, :]

    if multi_step:
        for b in range(n_seq):
            s_scr[b] = states[b]
        pl.when(step == pl.num_programs(1) - 1)(write_final_states)
    else:
        write_final_states()


def _seq_grid(n_rows, n_batch, n_seq, step_rows):
    rows_per_seq = n_rows // n_batch
    time_steps = rows_per_seq // step_rows
    assert n_seq == 1 or time_steps == 1
    return (n_batch // n_seq, time_steps), (lambda i, t: i * time_steps + t)


def _layer_state_call(kern, layer, earlier, state_shape, n_seq, **kwargs):
    tail = (0,) * (len(state_shape) - 1)
    block = lambda depth: pl.BlockSpec((depth, n_seq) + state_shape[1:], lambda i, t: (0, i) + tail)
    layer_block = pl.BlockSpec((None, n_seq) + state_shape[1:], lambda i, t: (layer, i) + tail)

    def call(args, specs, o_specs, o_shapes):
        if layer:
            args = [earlier] + args
            specs = [block(layer)] + specs
        return pl.pallas_call(
            functools.partial(kern, layer=layer),
            in_specs=specs,
            out_specs=list(o_specs) + [block(layer + 1)],
            out_shape=list(o_shapes) + [jax.ShapeDtypeStruct((layer + 1,) + state_shape, F32)],
            **kwargs,
        )(*args)

    return layer_block, call


def _rwkv(z, shift0, s0, states_out, layer, p, n_batch, t_valid, chunk, n_seq, seq_rows):
    W = RW_WIDTH
    has_state = s0 is not None
    grid, row_block = _seq_grid(z.shape[0], n_batch, n_seq, t_valid)
    R = n_seq * t_valid
    zcol = lambda j: pl.BlockSpec((R, W), lambda i, t, j=j: (row_block(i, t), j))
    weights = (p["mu"], p["w0"], p["w_w2"], p["a0"], p["a_w2"], p["g_w2"], p["k_k"], p["k_a"],
               p["r_k"], p["gn_g"], p["gn_b"])
    multi_step = grid[1] > 1
    kern = functools.partial(_rwkv_kernel, chunk=chunk, t_valid=t_valid, n_seq=n_seq,
                             seq_rows=seq_rows, has_state=has_state, multi_step=multi_step)
    state_spec, call = _layer_state_call(
        kern, layer, states_out, (n_batch, RW_HEADS, RW_DIM, RW_DIM), n_seq,
        grid=grid,
        scratch_shapes=[pltpu.VMEM((n_seq, RW_DIM, W), F32),
                        pltpu.VMEM((n_seq, RW_PROJ), F32)] if multi_step else [],
        compiler_params=_params(("parallel", "arbitrary")),
        name="rwkv7")
    args, specs = [z, z, z, z], [zcol(0), zcol(1), zcol(2), zcol(3)]
    if has_state:
        args += [shift0, s0]
        specs += [pl.BlockSpec((None, n_seq, 1, RW_PROJ), lambda i, t: (layer, i, 0, 0)), state_spec]
    args += list(weights)
    specs += [_layer_resident(w, layer) for w in weights]
    return call(args, specs,
                [pl.BlockSpec((R, W), lambda i, t: (row_block(i, t), 0)),
                 pl.BlockSpec((n_seq, 1, RW_PROJ), lambda i, t: (i, 0, 0))],
                [jax.ShapeDtypeStruct((z.shape[0], W), BF16),
                 jax.ShapeDtypeStruct((n_batch, 1, RW_PROJ), F32)])


def _hgrn_kernel(*refs, chunk, t_valid, n_seq, seq_rows, has_state, multi_step, layer):
    earlier_ref, refs = (refs[0], refs[1:]) if layer else (None, refs)
    if multi_step:
        *refs, s_scr = refs
    if has_state:
        zq_ref, zf_ref, zi_ref, zg_ref, s0_ref, lb_ref, ng_ref, o_ref, s_out_ref = refs
    else:
        zq_ref, zf_ref, zi_ref, zg_ref, lb_ref, ng_ref, o_ref, s_out_ref = refs
    C, N, H = chunk, HG_DIM, HG_HEADS
    L = seq_rows
    R = n_seq * L
    sub = min(HG_SUB, C)
    if t_valid < L:
        load = lambda ref: _unpack_rows(ref[...], n_seq, t_valid, L)
    else:
        load = lambda ref: ref[...]

    def initial_state(b, h):
        return s0_ref[b, h].T if has_state else jnp.zeros((N, N), F32)

    if multi_step:
        step = pl.program_id(1)

        @pl.when(step == 0)
        def _():
            for b in range(n_seq):
                for h in range(H):
                    s_scr[b, h] = initial_state(b, h)

        carried_state = lambda b, h: s_scr[b, h]
    else:
        carried_state = initial_state

    zq = load(zq_ref)
    q = zq * jax.nn.sigmoid(zq)
    lb = lb_ref[...]
    f = lb + (1.0 - lb) * jax.nn.sigmoid(load(zf_ref))
    log_f = jnp.log(jnp.maximum(f, F_MIN))
    k = 1.0 - f
    v = load(zi_ref)
    zg = load(zg_ref)
    out_gate = ng_ref[...] * (zg * jax.nn.sigmoid(zg))
    if t_valid < L:
        valid = (lax.broadcasted_iota(jnp.int32, (R, 1), 0) % L) < t_valid
        log_f = jnp.where(valid, log_f, 0.0)
        k = jnp.where(valid, k, 0.0)

    cum_all = _chunk_cumsum(log_f, C)
    tr = lax.broadcasted_iota(jnp.int32, (C, C), 0)
    tc = lax.broadcasted_iota(jnp.int32, (C, C), 1)
    causal = tc <= tr

    units = [(c, h) for c in range(R // C) for h in range(H)]
    part = lambda x, c, h: x[c * C:(c + 1) * C, h * N:(h + 1) * N]
    cums = {u: part(cum_all, *u) for u in units}
    totals = {u: cums[u][C - 1:C, :] for u in units}
    scores, intra, updates, inter = {}, {}, {}, {}
    n_blk = C // sub
    states = [[carried_state(b, h) for h in range(H)] for b in range(n_seq)]

    def score_stage(u):
        cum, q_h, k_h = cums[u], part(q, *u), part(k, *u)
        blks = [slice(i * sub, (i + 1) * sub) for i in range(n_blk)]
        refs = [cum[i * sub:i * sub + 1, :] for i in range(n_blk)]
        k_own = [k_h[blks[j]] * jnp.exp(jnp.minimum(refs[j] - cum[blks[j]], EXP_CLAMP))
                 for j in range(n_blk)]
        score_rows = []
        for i in range(n_blk):
            q_hat = q_h[blks[i]] * jnp.exp(cum[blks[i]] - refs[i])
            pieces = [k_own[j] * jnp.exp(refs[i] - refs[j]) for j in range(i)] + [k_own[i]]
            if i + 1 < n_blk:
                pieces.append(jnp.zeros(((n_blk - 1 - i) * sub, N), F32))
            k_hat = pieces[0] if len(pieces) == 1 else jnp.concatenate(pieces, axis=0)
            score_rows.append(_dot(q_hat, k_hat, _NT))
        rows_ = score_rows[0] if len(score_rows) == 1 else jnp.concatenate(score_rows, axis=0)
        scores[u] = jnp.where(causal, rows_, 0.0)

    def intra_stage(u):
        intra[u] = _dot(scores[u], part(v, *u))
        updates[u] = _dot(part(v, *u), part(k, *u) * jnp.exp(totals[u] - cums[u]), _TN)

    def state_stage(u):
        c, h = u
        b = (c * C) // L
        inter[u] = _dot(part(q, *u) * jnp.exp(cums[u]), states[b][h], _NT)
        states[b][h] = states[b][h] * jnp.exp(totals[u]) + updates[u]

    stages = (score_stage, intra_stage, state_stage)
    for idx in range(len(units) + len(stages) - 1):
        for lag, stage in enumerate(stages):
            if 0 <= idx - lag < len(units):
                stage(units[idx - lag])
    out_rows = []
    for c in range(R // C):
        outs = []
        for h in range(H):
            o_h = intra[(c, h)] + inter[(c, h)]
            outs.append(o_h * lax.rsqrt(jnp.mean(o_h * o_h, axis=-1, keepdims=True) + RMS_EPS))
        out_rows.append(jnp.concatenate(outs, axis=1))
    o = out_rows[0] if len(out_rows) == 1 else jnp.concatenate(out_rows, axis=0)
    o = (o * out_gate).astype(o_ref.dtype)
    o_ref[...] = _pack_rows(o, n_seq, t_valid, L) if t_valid < L else o

    def write_final_states():
        if layer:
            s_out_ref[:layer] = earlier_ref[...]
        for b in range(n_seq):
            for h in range(H):
                s_out_ref[layer, b, h] = states[b][h].T

    if multi_step:
        for b in range(n_seq):
            for h in range(H):
                s_scr[b, h] = states[b][h]
        pl.when(step == pl.num_programs(1) - 1)(write_final_states)
    else:
        write_final_states()


def _hgrn(z, s0, states_out, layer, lb, norm_g, n_batch, t_valid, chunk, n_seq, seq_rows):
    W = HG_WIDTH
    first = RW_PROJ // W
    has_state = s0 is not None
    grid, row_block = _seq_grid(z.shape[0], n_batch, n_seq, t_valid)
    R = n_seq * t_valid
    zcol = lambda j: pl.BlockSpec((R, W), lambda i, t, j=j: (row_block(i, t), first + j))
    multi_step = grid[1] > 1
    kern = functools.partial(_hgrn_kernel, chunk=chunk, t_valid=t_valid, n_seq=n_seq,
                             seq_rows=seq_rows, has_state=has_state, multi_step=multi_step)
    state_spec, call = _layer_state_call(
        kern, layer, states_out, (n_batch, HG_HEADS, HG_DIM, HG_DIM), n_seq,
        grid=grid,
        scratch_shapes=[pltpu.VMEM((n_seq, HG_HEADS, HG_DIM, HG_DIM), F32)] if multi_step else [],
        compiler_params=_params(("parallel", "arbitrary")),
        name="hgrn2")
    args, specs = [z, z, z, z], [zcol(0), zcol(1), zcol(2), zcol(3)]
    if has_state:
        args.append(s0)
        specs.append(state_spec)
    args += [lb, norm_g]
    specs += [_layer_resident(lb, layer), _layer_resident(norm_g, layer)]
    return call(args, specs, [pl.BlockSpec((R, W), lambda i, t: (row_block(i, t), 0))],
                [jax.ShapeDtypeStruct((z.shape[0], W), BF16)])


def _gelu(x):
    return 0.5 * x * (1.0 + lax.erf(x * (2.0 ** -0.5)))


def _cm_kernel(zu_ref, zv_ref, ws_ref, bs_ref, g_ref, b_ref, o_ref, *v_out):
    H, N, W, C = CM_HEADS, CM_DIM, CM_WIDTH, CM_CHUNK
    u = _gelu(zu_ref[...])
    v = _gelu(zv_ref[...])
    hr = lax.broadcasted_iota(jnp.int32, (W, W), 0)
    hc = lax.broadcasted_iota(jnp.int32, (W, W), 1)
    head_mean = jnp.where((hr // N) == (hc // N), 1.0 / N, 0.0).astype(F32)
    vc = v - _dot(v, head_mean)
    var = _dot(vc * vc, head_mean)
    vn = vc * lax.rsqrt(var + LN_EPS) * g_ref[...] + b_ref[...]
    if v_out:
        v_out[0][...] = vn
    tr = lax.broadcasted_iota(jnp.int32, (C, C), 0)
    tc = lax.broadcasted_iota(jnp.int32, (C, C), 1)
    lane = lax.broadcasted_iota(jnp.int32, (1, W), 1)
    w_causal = [jnp.where(tc <= tr, ws_ref[h], 0.0).astype(BF16) for h in range(H)]
    head_cols = [(lane >= h * N) & (lane < (h + 1) * N) for h in range(H)]
    vb = vn.astype(BF16)
    mixed = []
    for c in range(zu_ref.shape[0] // C):
        v_c = vb[c * C:(c + 1) * C]
        acc = bs_ref[...]
        for h in range(H):
            acc = acc + jnp.dot(w_causal[h], jnp.where(head_cols[h], v_c, jnp.zeros_like(v_c)),
                                preferred_element_type=F32)
        mixed.append(acc)
    mixed = mixed[0] if len(mixed) == 1 else jnp.concatenate(mixed, axis=0)
    o_ref[...] = (u * mixed).astype(o_ref.dtype)


def _chunk_mlp(z, layer, ws, bs_wide, ln_g, ln_b, want_v):
    n, W = z.shape[0], CM_WIDTH
    rows = min(CM_STEP_ROWS, n)
    first = (RW_PROJ + HG_PROJ) // W
    zcol = lambda j: pl.BlockSpec((rows, W), lambda i, j=j: (i, first + j))
    out_block = pl.BlockSpec((rows, W), lambda i: (i, 0))
    out_specs, out_shape = [out_block], [jax.ShapeDtypeStruct((n, W), BF16)]
    if want_v:
        out_specs.append(out_block)
        out_shape.append(jax.ShapeDtypeStruct((n, W), F32))
    return pl.pallas_call(
        _cm_kernel,
        grid=(n // rows,),
        in_specs=[zcol(0), zcol(1)] + [_layer_resident(w, layer) for w in (ws, bs_wide, ln_g, ln_b)],
        out_specs=out_specs,
        out_shape=out_shape,
        compiler_params=_params(("parallel",)),
        name="chunk_gmlp",
    )(z, z, ws, bs_wide, ln_g, ln_b)


def _dense_weights(p):
    vec = lambda a: a[:, None, :]
    out = {k: p[k].astype(BF16) for k in ("ffn1_w_in", "ffn1_w_out", "mix_w_in", "mix_w_out",
                                           "ffn2_w_in", "ffn2_w_out")}
    out.update({k: vec(p[k]) for k in ("ln1_g", "ln1_b", "ln2_g", "ln2_b", "ln3_g", "ln3_b")})
    return out


def _mixer_weights(p, lb):
    vec = lambda a: a.reshape(DEPTH, 1, -1)
    lora_pad = lambda w, start: jnp.pad(w, ((0, 0), (start, RW_WIDTH - start - w.shape[1]), (0, 0)))
    return dict(
        rw=dict(mu=vec(p["rw_mu"]), w0=vec(p["rw_w0"]),
                w_w2=lora_pad(p["rw_w_w2"], 0), a0=vec(p["rw_a0"]),
                a_w2=lora_pad(p["rw_a_w2"], RW_DECAY_LORA),
                g_w2=lora_pad(p["rw_g_w2"], RW_DECAY_LORA + RW_AAA_LORA),
                k_k=vec(p["rw_k_k"]), k_a=vec(p["rw_k_a"]), r_k=vec(p["rw_r_k"]),
                gn_g=vec(p["rw_gn_g"]), gn_b=vec(p["rw_gn_b"])),
        hg_lb=vec(lb), hg_norm_g=vec(p["hg_norm_g"]),
        cm_ws=p["cm_ws"],
        cm_bs=jnp.repeat(jnp.swapaxes(p["cm_bs"], 1, 2), CM_DIM, axis=2),
        cm_ln_g=vec(p["cm_ln_g"]), cm_ln_b=vec(p["cm_ln_b"]),
    )


def _short_chunk_mixing(ws, bs_wide, t_pad):
    reps = CM_CHUNK // t_pad
    pos = jnp.arange(CM_CHUNK)
    sel = (pos[None, :] == (pos % t_pad)[:, None]).astype(ws.dtype)
    tiled = jnp.einsum("rk,lhkm,cm->lhrc", sel, ws, sel, precision=lax.Precision.HIGHEST)
    same_seq = (pos[:, None] // t_pad) == (pos[None, :] // t_pad)
    return jnp.where(same_seq, tiled, 0.0), jnp.tile(bs_wide[:, :t_pad], (1, reps, 1))


class _TokenGroup:
    def __init__(self, x, rw_s0, rw_shift0, hg_s0, w):
        self.B, self.T, _ = x.shape
        self.x = x.reshape(self.B * self.T, D_MODEL)
        self.has_state = rw_s0 is not None
        self.rw_s0, self.hg_s0 = rw_s0, hg_s0
        self.rw_shift0 = rw_shift0[:, :, None, :] if self.has_state else None
        T = self.T
        self.short = T < RW_CHUNK
        if self.short:
            slot = -(-T // SUBLANES) * SUBLANES
            self.rw_chunk = self.hg_chunk = self.rw_rows = self.hg_rows = slot
            self.n_seq = SAMPLE_STEP_SEQS
            self.cm_ws, self.cm_bs = _short_chunk_mixing(w["cm_ws"], w["cm_bs"], T)
        else:
            self.rw_chunk, self.hg_chunk, self.n_seq = RW_CHUNK, HG_CHUNK, 1
            self.rw_rows, self.hg_rows = min(T, RW_STEP_ROWS), min(T, HG_STEP_ROWS)
            self.cm_ws, self.cm_bs = w["cm_ws"], w["cm_bs"]
        self.rw_states, self.hg_states, self.rw_shifts, self.cm_vs = None, None, [], []

    def mix(self, z, layer, w):
        B, T = self.B, self.T
        o_rw, last, self.rw_states = _rwkv(z, self.rw_shift0, self.rw_s0, self.rw_states, layer, w["rw"],
                                           B, min(T, self.rw_rows), self.rw_chunk, self.n_seq, self.rw_rows)
        self.rw_shifts.append(last.reshape(B, RW_PROJ))
        o_hg, self.hg_states = _hgrn(z, self.hg_s0, self.hg_states, layer, w["hg_lb"], w["hg_norm_g"],
                                     B, min(T, self.hg_rows), self.hg_chunk, self.n_seq, self.hg_rows)
        cm_out = _chunk_mlp(z, layer, self.cm_ws, self.cm_bs, w["cm_ln_g"], w["cm_ln_b"],
                            want_v=self.has_state)
        if self.has_state:
            self.cm_vs.append(cm_out[1].reshape(B, T, CM_WIDTH))
        return o_rw, o_hg, cm_out[0]


def _run_trunk(groups, dw, w):
    for l in range(DEPTH):
        projected = _dense_in([g.x for g in groups], l, dw["ffn1_w_in"], dw["ffn1_w_out"],
                              dw["ln1_g"], dw["ln1_b"], dw["mix_w_in"])
        mixed = [(x1,) + g.mix(z, l, w) for g, (x1, z) in zip(groups, projected)]
        ys = _dense_out(mixed, l, dw["mix_w_out"], dw["ln2_g"], dw["ln2_b"], dw["ffn2_w_in"],
                        dw["ffn2_w_out"], dw["ln3_g"], dw["ln3_b"])
        for g, y in zip(groups, ys):
            g.x = y


def kernel(x_prompt, x_sample, state_rwkv, state_rwkv_shift, state_hgrn, ffn1_w_in, ffn1_w_out, ln1_g, ln1_b, mix_w_in, mix_w_out, ln2_g, ln2_b, rw_mu, rw_w0, rw_w_w2, rw_a0, rw_a_w2, rw_g_w2, rw_k_k, rw_k_a, rw_r_k, rw_gn_g, rw_gn_b, hg_lb_logits, hg_norm_g, cm_ws, cm_bs, cm_ln_g, cm_ln_b, ffn2_w_in, ffn2_w_out, ln3_g, ln3_b):
    p = dict(ffn1_w_in=ffn1_w_in, ffn1_w_out=ffn1_w_out, ln1_g=ln1_g, ln1_b=ln1_b,
             mix_w_in=mix_w_in, mix_w_out=mix_w_out, ln2_g=ln2_g, ln2_b=ln2_b,
             rw_mu=rw_mu, rw_w0=rw_w0, rw_w_w2=rw_w_w2, rw_a0=rw_a0, rw_a_w2=rw_a_w2,
             rw_g_w2=rw_g_w2, rw_k_k=rw_k_k, rw_k_a=rw_k_a, rw_r_k=rw_r_k,
             rw_gn_g=rw_gn_g, rw_gn_b=rw_gn_b, hg_norm_g=hg_norm_g,
             cm_ws=cm_ws, cm_bs=cm_bs, cm_ln_g=cm_ln_g, cm_ln_b=cm_ln_b,
             ffn2_w_in=ffn2_w_in, ffn2_w_out=ffn2_w_out, ln3_g=ln3_g, ln3_b=ln3_b)
    s = jax.nn.softmax(hg_lb_logits.astype(F32), axis=0)
    lb = jnp.cumsum(s, axis=0) - s[0]
    weights = _mixer_weights(p, lb)
    dw = _dense_weights(p)
    prompt = _TokenGroup(x_prompt, None, None, None, weights)
    sample = _TokenGroup(x_sample, state_rwkv, state_rwkv_shift, state_hgrn, weights)
    _run_trunk([prompt, sample], dw, weights)
    return (prompt.x.reshape(x_prompt.shape), sample.x.reshape(x_sample.shape),
            prompt.rw_states, jnp.stack(prompt.rw_shifts), prompt.hg_states,
            sample.rw_states, jnp.stack(sample.rw_shifts), sample.hg_states,
            jnp.stack(sample.cm_vs))
```

```python
import functools

import jax
import jax.numpy as jnp
from jax import lax
from jax.experimental import pallas as pl
from jax.experimental.pallas import tpu as pltpu

F32 = jnp.float32
BF16 = jnp.bfloat16

D_MODEL = 1024
DEPTH = 2
RW_HEADS, RW_DIM = 4, 64
RW_WIDTH = RW_HEADS * RW_DIM
RW_DECAY_LORA, RW_AAA_LORA, RW_GATE_LORA = 64, 64, 128
RW_PROJ = 3 * RW_WIDTH + RW_DECAY_LORA + RW_AAA_LORA + RW_GATE_LORA
RW_GN_EPS = 64e-5
HG_HEADS, HG_DIM = 4, 128
HG_WIDTH = HG_HEADS * HG_DIM
HG_PROJ = 4 * HG_WIDTH
RMS_EPS = 1e-6
F_MIN = 1e-30
CM_HEADS, CM_DIM = 4, 64
CM_WIDTH = CM_HEADS * CM_DIM
CM_CHUNK = 128
CM_PROJ = 2 * CM_WIDTH
MIX_WIDTH = RW_WIDTH + HG_WIDTH + CM_WIDTH
IN_PROJ = RW_PROJ + HG_PROJ + CM_PROJ
D_FF = 2816
LN_EPS = 1e-5
ALPHA = (2.0 * DEPTH) ** 0.25

VMEM_LIMIT_BYTES = 56 * 1024 * 1024
SUBLANES = 8
MXU_DIM = 256
DENSE_ROWS = 512
DENSE_SUB_ROWS = 256
FF_CHUNK = 1408
RW_CHUNK = 64
RW_GROUP = 64
HG_CHUNK = 64
HG_SUB = 16
EXP_CLAMP = 80.0
RW_BATCH_GROUPS = 4
RW_STEP_ROWS = 1024
HG_STEP_ROWS = 1024
SAMPLE_STEP_SEQS = 16
CM_STEP_ROWS = 1024


def _params(semantics):
    return pltpu.CompilerParams(dimension_semantics=semantics,
                                vmem_limit_bytes=VMEM_LIMIT_BYTES)


_NN = ((1,), (0,))
_NT = ((1,), (1,))
_TN = ((0,), (0,))


def _dot(a, b, dims=_NN):
    return lax.dot_general(a.astype(BF16), b.astype(BF16), (dims, ((), ())),
                           preferred_element_type=F32)


def _mask_dot(mask, x):
    m = mask.astype(BF16)
    hi = x.astype(BF16)
    r1 = x - hi.astype(F32)
    mid = r1.astype(BF16)
    lo = (r1 - mid.astype(F32)).astype(BF16)
    dot = lambda t: jnp.dot(m, t, preferred_element_type=F32)
    return dot(hi) + (dot(mid) + dot(lo))


def _chunk_cumsum(x, chunk):
    n = min(x.shape[0], MXU_DIM)
    r = lax.broadcasted_iota(jnp.int32, (n, n), 0)
    c = lax.broadcasted_iota(jnp.int32, (n, n), 1)
    mask = ((r // chunk) == (c // chunk)) & (c <= r)
    pieces = [_mask_dot(mask, x[lo:lo + n]) for lo in range(0, x.shape[0], n)]
    return pieces[0] if len(pieces) == 1 else jnp.concatenate(pieces, axis=0)


def _slot_matrix(n_seq, t_valid, slot, transpose=False):
    shape = (n_seq * slot, n_seq * t_valid)
    padded = lax.broadcasted_iota(jnp.int32, shape[::-1] if transpose else shape, 1 if transpose else 0)
    packed = lax.broadcasted_iota(jnp.int32, shape[::-1] if transpose else shape, 0 if transpose else 1)
    return ((padded // slot) == (packed // t_valid)) & ((padded % slot) == (packed % t_valid))


def _unpack_rows(x, n_seq, t_valid, slot):
    return _mask_dot(_slot_matrix(n_seq, t_valid, slot), x)


def _pack_rows(x, n_seq, t_valid, slot):
    sel = _slot_matrix(n_seq, t_valid, slot, transpose=True).astype(BF16)
    return jnp.dot(sel, x, preferred_element_type=F32).astype(x.dtype)


def _layer_norm(x, g, b):
    mu = jnp.mean(x, axis=-1, keepdims=True)
    xc = x - mu
    var = jnp.mean(xc * xc, axis=-1, keepdims=True)
    return xc * lax.rsqrt(var + LN_EPS) * g + b


def _swiglu(xb, w_in_ref, w_out_ref):
    acc = None
    for lo in range(0, D_FF, FF_CHUNK):
        gate = jnp.dot(xb, w_in_ref[:, lo:lo + FF_CHUNK], preferred_element_type=F32)
        up = jnp.dot(xb, w_in_ref[:, D_FF + lo:D_FF + lo + FF_CHUNK], preferred_element_type=F32)
        h = (gate * jax.nn.sigmoid(gate) * up).astype(BF16)
        part = jnp.dot(h, w_out_ref[lo:lo + FF_CHUNK, :], preferred_element_type=F32)
        acc = part if acc is None else acc + part
    return acc


def _sub_tiles(n_rows):
    sub = min(DENSE_SUB_ROWS, n_rows)
    return [slice(lo, lo + sub) for lo in range(0, n_rows, sub)]


def _layer_resident(stacked, layer):
    tail = (0,) * (stacked.ndim - 1)
    return pl.BlockSpec((None,) + stacked.shape[1:], lambda *_: (layer,) + tail,
                        pipeline_mode=pl.Buffered(1))


def _grouped_row_call(body, name, groups, in_widths, out_widths, weights, layer):
    rows = min(DENSE_ROWS, min(g[0].shape[0] for g in groups))
    steps = [g[0].shape[0] // rows for g in groups]
    starts = [sum(steps[:k]) for k in range(len(groups))]
    n_in, n_out, n_w = len(in_widths), len(out_widths), len(weights)

    def kern(*refs):
        ins, w_refs, outs = refs[:len(groups) * n_in], refs[len(groups) * n_in:][:n_w], refs[-len(groups) * n_out:]
        step = pl.program_id(0)
        for k in range(len(groups)):
            run = functools.partial(body, *ins[k * n_in:(k + 1) * n_in], *w_refs,
                                    *outs[k * n_out:(k + 1) * n_out])
            if len(groups) == 1:
                run()
            else:
                pl.when((step >= starts[k]) & (step < starts[k] + steps[k]))(run)

    def block(width, k):
        return pl.BlockSpec((rows, width), lambda i: (jnp.clip(i - starts[k], 0, steps[k] - 1), 0))

    outs = pl.pallas_call(
        kern,
        grid=(sum(steps),),
        in_specs=[block(wd, k) for k in range(len(groups)) for wd in in_widths]
                 + [_layer_resident(w, layer) for w in weights],
        out_specs=[block(wd, k) for k in range(len(groups)) for wd in out_widths],
        out_shape=[jax.ShapeDtypeStruct((g[0].shape[0], wd), F32) for g in groups for wd in out_widths],
        compiler_params=_params(("arbitrary",)),
        name=name,
    )(*[a for g in groups for a in g], *weights)
    return [tuple(outs[k * n_out:(k + 1) * n_out]) for k in range(len(groups))]


def _dense_in_body(x_ref, w_in_ref, w_out_ref, g_ref, b_ref, w_mix_ref, x1_ref, z_ref):
    tiles = _sub_tiles(x_ref.shape[0])
    xs = [x_ref[t, :] for t in tiles]
    ffn = [_swiglu(x.astype(BF16), w_in_ref, w_out_ref) for x in xs]
    ys = [_layer_norm(ALPHA * x + 0.5 * f, g_ref[...], b_ref[...]) for x, f in zip(xs, ffn)]
    for t, y in zip(tiles, ys):
        x1_ref[t, :] = y
        z_ref[t, :] = jnp.dot(y.astype(BF16), w_mix_ref[...], preferred_element_type=F32)


def _dense_in(xs, layer, w_in, w_out, g, b, w_mix):
    return [_grouped_row_call(_dense_in_body, "dense_in", [(x,)], (D_MODEL,), (D_MODEL, IN_PROJ),
                              (w_in, w_out, g, b, w_mix), layer)[0] for x in xs]


def _dense_out_body(x_ref, orw_ref, ohg_ref, ocm_ref, wmix_ref, g2_ref, b2_ref,
                    w_in_ref, w_out_ref, g3_ref, b3_ref, y_ref):
    hg0, cm0 = RW_WIDTH, RW_WIDTH + HG_WIDTH
    tiles = _sub_tiles(x_ref.shape[0])
    mix = [jnp.dot(orw_ref[t, :], wmix_ref[:hg0, :], preferred_element_type=F32)
           + jnp.dot(ohg_ref[t, :], wmix_ref[hg0:cm0, :], preferred_element_type=F32)
           + jnp.dot(ocm_ref[t, :], wmix_ref[cm0:, :], preferred_element_type=F32) for t in tiles]
    x2 = [_layer_norm(ALPHA * x_ref[t, :] + m, g2_ref[...], b2_ref[...]) for t, m in zip(tiles, mix)]
    ffn = [_swiglu(x.astype(BF16), w_in_ref, w_out_ref) for x in x2]
    for t, x, f in zip(tiles, x2, ffn):
        y_ref[t, :] = _layer_norm(ALPHA * x + 0.5 * f, g3_ref[...], b3_ref[...])


def _dense_out(groups, layer, w_mix, g2, b2, w_in, w_out, g3, b3):
    ys = _grouped_row_call(_dense_out_body, "dense_out", groups,
                           (D_MODEL, RW_WIDTH, HG_WIDTH, CM_WIDTH), (D_MODEL,),
                           (w_mix, g2, b2, w_in, w_out, g3, b3), layer)
    return [y for (y,) in ys]


def _head_stack(x, head_masks):
    return jnp.concatenate([x * m for m in head_masks], axis=0)


def _rwkv_kernel(*refs, chunk, t_valid, n_seq, seq_rows, has_state, multi_step, layer):
    earlier_ref, refs = (refs[0], refs[1:]) if layer else (None, refs)
    if multi_step:
        *refs, s_scr, prev_scr = refs
    if has_state:
        (zr_ref, zk_ref, zv_ref, zx_ref, shift_ref, s0_ref, mu_ref, w0_ref, ww2_ref, a0_ref, aw2_ref,
         gw2_ref, kk_ref, ka_ref, rk_ref, gng_ref, gnb_ref, o_ref, last_ref, s_out_ref) = refs
    else:
        (zr_ref, zk_ref, zv_ref, zx_ref, mu_ref, w0_ref, ww2_ref, a0_ref, aw2_ref,
         gw2_ref, kk_ref, ka_ref, rk_ref, gng_ref, gnb_ref, o_ref, last_ref, s_out_ref) = refs
    C, N, H, W, G = chunk, RW_DIM, RW_HEADS, RW_WIDTH, RW_GROUP
    L = seq_rows
    R = n_seq * L

    def initial_state(b):
        if has_state:
            return jnp.concatenate([s0_ref[b, h] for h in range(H)], axis=1)
        return jnp.zeros((N, W), F32)

    def initial_prev(b):
        return shift_ref[b] if has_state else jnp.zeros((1, RW_PROJ), F32)

    if multi_step:
        step = pl.program_id(1)

        @pl.when(step == 0)
        def _():
            for b in range(n_seq):
                s_scr[b] = initial_state(b)
                prev_scr[b:b + 1, :] = initial_prev(b)

        carried_state = lambda b: s_scr[b]
        carried_prev = lambda b, cols: prev_scr[b:b + 1, cols]
    else:
        carried_state = initial_state
        carried_prev = lambda b, cols: initial_prev(b)[:, cols]

    RB = min(RW_BATCH_GROUPS * G, R)
    n_batches = R // RB
    assert (n_seq == 1 and t_valid == L) or n_batches == 1
    assert C == G or C == L
    brow = lax.broadcasted_iota(jnp.int32, (RB, 1), 0)
    lane = lax.broadcasted_iota(jnp.int32, (1, W), 1)
    head_masks = [((lane >= h * N) & (lane < (h + 1) * N)).astype(F32) for h in range(H)]
    hr = lax.broadcasted_iota(jnp.int32, (W, W), 0)
    hc = lax.broadcasted_iota(jnp.int32, (W, W), 1)
    head_ones = ((hr // N) == (hc // N)).astype(F32)
    elem = {}

    def prologue(bi):
        lo = bi * RB
        t_in = (brow + lo) % L
        valid = t_in < t_valid

        def load(ref, part):
            cols = slice(part * W, (part + 1) * W)
            if t_valid < L:
                z = _unpack_rows(ref[...], n_seq, t_valid, L)
            else:
                z = ref[lo:lo + RB, :]
            if n_seq == 1:
                first = carried_prev(0, cols) if bi == 0 else ref[lo - 1:lo, :]
                prev = jnp.where(brow == 0, first, pltpu.roll(z, 1, axis=0))
            else:
                first = jnp.concatenate([jnp.broadcast_to(carried_prev(b, cols), (L, W))
                                         for b in range(n_seq)], axis=0)
                prev = jnp.where(t_in == 0, first, pltpu.roll(z, 1, axis=0))
            if multi_step:
                for b in range(n_seq):
                    last = b * L + t_valid - 1 - lo
                    if 0 <= last < RB:
                        prev_scr[b:b + 1, cols] = z[last:last + 1, :]
            return z + (prev - z) * mu_ref[:, cols]

        r = load(zr_ref, 0)
        k = load(zk_ref, 1)
        v = load(zv_ref, 2)
        x4 = load(zx_ref, 3)

        w_pre = w0_ref[...] + _dot(jnp.tanh(x4), ww2_ref[...])
        yield
        nw = -w_pre
        softplus = jnp.maximum(nw, 0.0) + jnp.log(1.0 + jnp.exp(-jnp.abs(nw)))
        lw = -jnp.exp(-softplus - 0.5)
        a = jax.nn.sigmoid(a0_ref[...] + _dot(x4, aw2_ref[...]))
        yield
        gate = _dot(jax.nn.sigmoid(x4), gw2_ref[...])
        yield
        kk = k * kk_ref[...]
        k = k * (1.0 + (a - 1.0) * ka_ref[...])
        sums = _dot(jnp.concatenate([kk * kk, r * k * rk_ref[...]], axis=0), head_ones)
        yield
        kk = kk / jnp.maximum(jnp.sqrt(sums[:RB]), 1e-12)
        bonus = sums[RB:] * v
        if t_valid < L:
            lw = jnp.where(valid, lw, 0.0)
            kk = jnp.where(valid, kk, 0.0)
            k = jnp.where(valid, k, 0.0)
            v = jnp.where(valid, v, 0.0)
        cum = _chunk_cumsum(lw, C)
        yield
        g_in = jnp.exp(cum)
        g_inv = jnp.exp(-cum)
        elem[bi] = dict(a_hat=-kk * jnp.exp(cum - lw), b_chk=kk * a * g_inv, k_chk=k * g_inv,
                        r_hat=r * g_in, v=v, g_in=g_in, bonus=bonus, gate=gate)

    HG_ = H * G
    sr = lax.broadcasted_iota(jnp.int32, (HG_, HG_), 0)
    sc = lax.broadcasted_iota(jnp.int32, (HG_, HG_), 1)
    same_chunk = (sr // C) == (sc // C)
    strict = same_chunk & (sc < sr)
    incl = same_chunk & (sc <= sr)
    eye = (sr == sc).astype(F32)
    n_sub = G // C

    def gather(x, sub):
        if n_sub == 1:
            return x
        return jnp.concatenate([x[h * G + sub * C:h * G + (sub + 1) * C] for h in range(H)], axis=0)

    def scatter(pieces):
        if n_sub == 1:
            return pieces[0]
        return jnp.concatenate([pieces[sub][h * C:(h + 1) * C]
                                for h in range(H) for sub in range(n_sub)], axis=0)

    pre = {}
    gis = range(RB // G)

    def precompute(bi):
        e = elem[bi]
        a_st, b_st, k_st, r_st, v_st = {}, {}, {}, {}, {}
        for gi in gis:
            rows = slice(gi * G, (gi + 1) * G)
            a_st[gi] = _head_stack(e["a_hat"][rows], head_masks)
            b_st[gi] = _head_stack(e["b_chk"][rows], head_masks)
            k_st[gi] = _head_stack(e["k_chk"][rows], head_masks)
            r_st[gi] = _head_stack(e["r_hat"][rows], head_masks)
            v_st[gi] = jnp.concatenate([e["v"][rows, h * N:(h + 1) * N] for h in range(H)], axis=0)
        m_ab = {gi: jnp.where(strict, _dot(a_st[gi], b_st[gi], _NT), 0.0) for gi in gis}
        yield
        m_ak = {gi: jnp.where(strict, _dot(a_st[gi], k_st[gi], _NT), 0.0) for gi in gis}
        yield
        p_rb = {gi: jnp.where(incl, _dot(r_st[gi], b_st[gi], _NT), 0.0) for gi in gis}
        yield
        p_rk = {gi: jnp.where(incl, _dot(r_st[gi], k_st[gi], _NT), 0.0) for gi in gis}
        yield
        t_inv = {gi: eye + m_ab[gi] for gi in gis}
        power = dict(m_ab)
        span = 2
        while span < C:
            power = {gi: _dot(power[gi], power[gi]) for gi in gis}
            yield
            t_inv = {gi: t_inv[gi] + _dot(t_inv[gi], power[gi]) for gi in gis}
            yield
            span *= 2
        w_m = {gi: _dot(t_inv[gi], a_st[gi]) for gi in gis}
        yield
        mv = {gi: _dot(m_ak[gi], v_st[gi]) for gi in gis}
        yield
        u_m = {gi: _dot(t_inv[gi], mv[gi]) for gi in gis}
        yield
        vk = {gi: [_dot(gather(v_st[gi], sub), gather(k_st[gi], sub), _TN) for sub in range(n_sub)]
              for gi in gis}
        yield
        for gi in gis:
            pre[bi, gi] = dict(b_s=b_st[gi], r_s=r_st[gi], v_s=v_st[gi], p_rb=p_rb[gi], p_rk=p_rk[gi],
                               w_m=w_m[gi], u_m=u_m[gi], vk=vk[gi])

    states = [carried_state(b) for b in range(n_seq)]

    def chain(bi):
        e = elem[bi]
        outs = []
        for gi in gis:
            g = pre[bi, gi]
            subs = range(n_sub)
            seqs = [(bi * RB + gi * G + sub * C) // L for sub in subs]
            old = [states[b] for b in seqs]
            c_parts = [_dot(gather(g["w_m"], sub), old[sub], _NT) + gather(g["u_m"], sub)
                       for sub in subs]
            yield
            upd = [_dot(c_parts[sub], gather(g["b_s"], sub), _TN) for sub in subs]
            for sub in subs:
                last = gi * G + sub * C + C - 1
                states[seqs[sub]] = (old[sub] + upd[sub] + g["vk"][sub]) * e["g_in"][last:last + 1, :]
            yield
            rs_parts = [_dot(gather(g["r_s"], sub), old[sub], _NT) for sub in subs]
            yield
            o_s = scatter(rs_parts) + _dot(jnp.concatenate([g["p_rb"], g["p_rk"]], axis=1),
                                           jnp.concatenate([scatter(c_parts), g["v_s"]], axis=0))
            mu_o = jnp.mean(o_s, axis=-1, keepdims=True)
            oc = o_s - mu_o
            var_o = jnp.mean(oc * oc, axis=-1, keepdims=True)
            on = oc * lax.rsqrt(var_o + RW_GN_EPS)
            outs.append(jnp.concatenate([on[h * G:(h + 1) * G, :] for h in range(H)], axis=1))
            yield
        o = outs[0] if len(outs) == 1 else jnp.concatenate(outs, axis=0)
        o = ((o * gng_ref[...] + gnb_ref[...] + e["bonus"]) * e["gate"]).astype(o_ref.dtype)
        if t_valid < L:
            o_ref[...] = _pack_rows(o, n_seq, t_valid, L)
        else:
            o_ref[bi * RB:(bi + 1) * RB, :] = o

    def run_interleaved(*gens):
        live = list(gens)
        while live:
            for gen in list(live):
                if next(gen, StopIteration) is StopIteration:
                    live.remove(gen)

    run_interleaved(prologue(0))
    for bi in range(n_batches):
        run_interleaved(*([precompute(bi)]
                          + ([prologue(bi + 1)] if bi + 1 < n_batches else [])
                          + ([chain(bi - 1)] if bi > 0 else [])))
    run_interleaved(chain(n_batches - 1))

    def write_final_states():
        if layer:
            s_out_ref[:layer] = earlier_ref[...]
        for b in range(n_seq):
            for h in range(H):
                s_out_ref[layer, b, h] = states[b][:, h * N:(h + 1) * N]
            row = (b + 1) * t_valid - 1
            for part, ref in enumerate((zr_ref, zk_ref, zv_ref, zx_ref)):
                last_ref[b, :, part * W:(part + 1) * W] = ref[row:row + 1, :]

    if multi_step:
        for b in range(n_seq):
            s_scr[b] = states[b]
        pl.when(step == pl.num_programs(1) - 1)(write_final_states)
    else:
        write_final_states()


def _seq_grid(n_rows, n_batch, n_seq, step_rows):
    rows_per_seq = n_rows // n_batch
    time_steps = rows_per_seq // step_rows
    assert n_seq == 1 or time_steps == 1
    return (n_batch // n_seq, time_steps), (lambda i, t: i * time_steps + t)


def _layer_state_call(kern, layer, earlier, state_shape, n_seq, **kwargs):
    tail = (0,) * (len(state_shape) - 1)
    block = lambda depth: pl.BlockSpec((depth, n_seq) + state_shape[1:], lambda i, t: (0, i) + tail)
    layer_block = pl.BlockSpec((None, n_seq) + state_shape[1:], lambda i, t: (layer, i) + tail)

    def call(args, specs, o_specs, o_shapes):
        if layer:
            args = [earlier] + args
            specs = [block(layer)] + specs
        return pl.pallas_call(
            functools.partial(kern, layer=layer),
            in_specs=specs,
            out_specs=list(o_specs) + [block(layer + 1)],
            out_shape=list(o_shapes) + [jax.ShapeDtypeStruct((layer + 1,) + state_shape, F32)],
            **kwargs,
        )(*args)

    return layer_block, call


def _rwkv(z, shift0, s0, states_out, layer, p, n_batch, t_valid, chunk, n_seq, seq_rows):
    W = RW_WIDTH
    has_state = s0 is not None
    grid, row_block = _seq_grid(z.shape[0], n_batch, n_seq, t_valid)
    R = n_seq * t_valid
    zcol = lambda j: pl.BlockSpec((R, W), lambda i, t, j=j: (row_block(i, t), j))
    weights = (p["mu"], p["w0"], p["w_w2"], p["a0"], p["a_w2"], p["g_w2"], p["k_k"], p["k_a"],
               p["r_k"], p["gn_g"], p["gn_b"])
    multi_step = grid[1] > 1
    kern = functools.partial(_rwkv_kernel, chunk=chunk, t_valid=t_valid, n_seq=n_seq,
                             seq_rows=seq_rows, has_state=has_state, multi_step=multi_step)
    state_spec, call = _layer_state_call(
        kern, layer, states_out, (n_batch, RW_HEADS, RW_DIM, RW_DIM), n_seq,
        grid=grid,
        scratch_shapes=[pltpu.VMEM((n_seq, RW_DIM, W), F32),
                        pltpu.VMEM((n_seq, RW_PROJ), F32)] if multi_step else [],
        compiler_params=_params(("parallel", "arbitrary")),
        name="rwkv7")
    args, specs = [z, z, z, z], [zcol(0), zcol(1), zcol(2), zcol(3)]
    if has_state:
        args += [shift0, s0]
        specs += [pl.BlockSpec((None, n_seq, 1, RW_PROJ), lambda i, t: (layer, i, 0, 0)), state_spec]
    args += list(weights)
    specs += [_layer_resident(w, layer) for w in weights]
    return call(args, specs,
                [pl.BlockSpec((R, W), lambda i, t: (row_block(i, t), 0)),
                 pl.BlockSpec((n_seq, 1, RW_PROJ), lambda i, t: (i, 0, 0))],
                [jax.ShapeDtypeStruct((z.shape[0], W), BF16),
                 jax.ShapeDtypeStruct((n_batch, 1, RW_PROJ), F32)])


def _hgrn_kernel(*refs, chunk, t_valid, n_seq, seq_rows, has_state, multi_step, layer):
    earlier_ref, refs = (refs[0], refs[1:]) if layer else (None, refs)
    if multi_step:
        *refs, s_scr = refs
    if has_state:
        zq_ref, zf_ref, zi_ref, zg_ref, s0_ref, lb_ref, ng_ref, o_ref, s_out_ref = refs
    else:
        zq_ref, zf_ref, zi_ref, zg_ref, lb_ref, ng_ref, o_ref, s_out_ref = refs
    C, N, H = chunk, HG_DIM, HG_HEADS
    L = seq_rows
    R = n_seq * L
    sub = min(HG_SUB, C)
    if t_valid < L:
        load = lambda ref: _unpack_rows(ref[...], n_seq, t_valid, L)
    else:
        load = lambda ref: ref[...]

    def initial_state(b, h):
        return s0_ref[b, h].T if has_state else jnp.zeros((N, N), F32)

    if multi_step:
        step = pl.program_id(1)

        @pl.when(step == 0)
        def _():
            for b in range(n_seq):
                for h in range(H):
                    s_scr[b, h] = initial_state(b, h)

        carried_state = lambda b, h: s_scr[b, h]
    else:
        carried_state = initial_state

    zq = load(zq_ref)
    q = zq * jax.nn.sigmoid(zq)
    lb = lb_ref[...]
    f = lb + (1.0 - lb) * jax.nn.sigmoid(load(zf_ref))
    log_f = jnp.log(jnp.maximum(f, F_MIN))
    k = 1.0 - f
    v = load(zi_ref)
    zg = load(zg_ref)
    out_gate = ng_ref[...] * (zg * jax.nn.sigmoid(zg))
    if t_valid < L:
        valid = (lax.broadcasted_iota(jnp.int32, (R, 1), 0) % L) < t_valid
        log_f = jnp.where(valid, log_f, 0.0)
        k = jnp.where(valid, k, 0.0)

    cum_all = _chunk_cumsum(log_f, C)
    tr = lax.broadcasted_iota(jnp.int32, (C, C), 0)
    tc = lax.broadcasted_iota(jnp.int32, (C, C), 1)
    causal = tc <= tr

    units = [(c, h) for c in range(R // C) for h in range(H)]
    part = lambda x, c, h: x[c * C:(c + 1) * C, h * N:(h + 1) * N]
    cums = {u: part(cum_all, *u) for u in units}
    totals = {u: cums[u][C - 1:C, :] for u in units}
    scores, intra, updates, inter = {}, {}, {}, {}
    n_blk = C // sub
    states = [[carried_state(b, h) for h in range(H)] for b in range(n_seq)]

    def score_stage(u):
        cum, q_h, k_h = cums[u], part(q, *u), part(k, *u)
        blks = [slice(i * sub, (i + 1) * sub) for i in range(n_blk)]
        refs = [cum[i * sub:i * sub + 1, :] for i in range(n_blk)]
        k_own = [k_h[blks[j]] * jnp.exp(jnp.minimum(refs[j] - cum[blks[j]], EXP_CLAMP))
                 for j in range(n_blk)]
        score_rows = []
        for i in range(n_blk):
            q_hat = q_h[blks[i]] * jnp.exp(cum[blks[i]] - refs[i])
            pieces = [k_own[j] * jnp.exp(refs[i] - refs[j]) for j in range(i)] + [k_own[i]]
            if i + 1 < n_blk:
                pieces.append(jnp.zeros(((n_blk - 1 - i) * sub, N), F32))
            k_hat = pieces[0] if len(pieces) == 1 else jnp.concatenate(pieces, axis=0)
            score_rows.append(_dot(q_hat, k_hat, _NT))
        rows_ = score_rows[0] if len(score_rows) == 1 else jnp.concatenate(score_rows, axis=0)
        scores[u] = jnp.where(causal, rows_, 0.0)

    def intra_stage(u):
        intra[u] = _dot(scores[u], part(v, *u))
        updates[u] = _dot(part(v, *u), part(k, *u) * jnp.exp(totals[u] - cums[u]), _TN)

    def state_stage(u):
        c, h = u
        b = (c * C) // L
        inter[u] = _dot(part(q, *u) * jnp.exp(cums[u]), states[b][h], _NT)
        states[b][h] = states[b][h] * jnp.exp(totals[u]) + updates[u]

    stages = (score_stage, intra_stage, state_stage)
    for idx in range(len(units) + len(stages) - 1):
        for lag, stage in enumerate(stages):
            if 0 <= idx - lag < len(units):
                stage(units[idx - lag])
    out_rows = []
    for c in range(R // C):
        outs = []
        for h in range(H):
            o_h = intra[(c, h)] + inter[(c, h)]
            outs.append(o_h * lax.rsqrt(jnp.mean(o_h * o_h, axis=-1, keepdims=True) + RMS_EPS))
        out_rows.append(jnp.concatenate(outs, axis=1))
    o = out_rows[0] if len(out_rows) == 1 else jnp.concatenate(out_rows, axis=0)
    o = (o * out_gate).astype(o_ref.dtype)
    o_ref[...] = _pack_rows(o, n_seq, t_valid, L) if t_valid < L else o

    def write_final_states():
        if layer:
            s_out_ref[:layer] = earlier_ref[...]
        for b in range(n_seq):
            for h in range(H):
                s_out_ref[layer, b, h] = states[b][h].T

    if multi_step:
        for b in range(n_seq):
            for h in range(H):
                s_scr[b, h] = states[b][h]
        pl.when(step == pl.num_programs(1) - 1)(write_final_states)
    else:
        write_final_states()


def _hgrn(z, s0, states_out, layer, lb, norm_g, n_batch, t_valid, chunk, n_seq, seq_rows):
    W = HG_WIDTH
    first = RW_PROJ // W
    has_state = s0 is not None
    grid, row_block = _seq_grid(z.shape[0], n_batch, n_seq, t_valid)
    R = n_seq * t_valid
    zcol = lambda j: pl.BlockSpec((R, W), lambda i, t, j=j: (row_block(i, t), first + j))
    multi_step = grid[1] > 1
    kern = functools.partial(_hgrn_kernel, chunk=chunk, t_valid=t_valid, n_seq=n_seq,
                             seq_rows=seq_rows, has_state=has_state, multi_step=multi_step)
    state_spec, call = _layer_state_call(
        kern, layer, states_out, (n_batch, HG_HEADS, HG_DIM, HG_DIM), n_seq,
        grid=grid,
        scratch_shapes=[pltpu.VMEM((n_seq, HG_HEADS, HG_DIM, HG_DIM), F32)] if multi_step else [],
        compiler_params=_params(("parallel", "arbitrary")),
        name="hgrn2")
    args, specs = [z, z, z, z], [zcol(0), zcol(1), zcol(2), zcol(3)]
    if has_state:
        args.append(s0)
        specs.append(state_spec)
    args += [lb, norm_g]
    specs += [_layer_resident(lb, layer), _layer_resident(norm_g, layer)]
    return call(args, specs, [pl.BlockSpec((R, W), lambda i, t: (row_block(i, t), 0))],
                [jax.ShapeDtypeStruct((z.shape[0], W), BF16)])


def _gelu(x):
    return 0.5 * x * (1.0 + lax.erf(x * (2.0 ** -0.5)))


def _cm_kernel(zu_ref, zv_ref, ws_ref, bs_ref, g_ref, b_ref, o_ref, *v_out):
    H, N, W, C = CM_HEADS, CM_DIM, CM_WIDTH, CM_CHUNK
    u = _gelu(zu_ref[...])
    v = _gelu(zv_ref[...])
    hr = lax.broadcasted_iota(jnp.int32, (W, W), 0)
    hc = lax.broadcasted_iota(jnp.int32, (W, W), 1)
    head_mean = jnp.where((hr // N) == (hc // N), 1.0 / N, 0.0).astype(F32)
    vc = v - _dot(v, head_mean)
    var = _dot(vc * vc, head_mean)
    vn = vc * lax.rsqrt(var + LN_EPS) * g_ref[...] + b_ref[...]
    if v_out:
        v_out[0][...] = vn
    tr = lax.broadcasted_iota(jnp.int32, (C, C), 0)
    tc = lax.broadcasted_iota(jnp.int32, (C, C), 1)
    lane = lax.broadcasted_iota(jnp.int32, (1, W), 1)
    w_causal = [jnp.where(tc <= tr, ws_ref[h], 0.0).astype(BF16) for h in range(H)]
    head_cols = [(lane >= h * N) & (lane < (h + 1) * N) for h in range(H)]
    vb = vn.astype(BF16)
    mixed = []
    for c in range(zu_ref.shape[0] // C):
        v_c = vb[c * C:(c + 1) * C]
        acc = bs_ref[...]
        for h in range(H):
            acc = acc + jnp.dot(w_causal[h], jnp.where(head_cols[h], v_c, jnp.zeros_like(v_c)),
                                preferred_element_type=F32)
        mixed.append(acc)
    mixed = mixed[0] if len(mixed) == 1 else jnp.concatenate(mixed, axis=0)
    o_ref[...] = (u * mixed).astype(o_ref.dtype)


def _chunk_mlp(z, layer, ws, bs_wide, ln_g, ln_b, want_v):
    n, W = z.shape[0], CM_WIDTH
    rows = min(CM_STEP_ROWS, n)
    first = (RW_PROJ + HG_PROJ) // W
    zcol = lambda j: pl.BlockSpec((rows, W), lambda i, j=j: (i, first + j))
    out_block = pl.BlockSpec((rows, W), lambda i: (i, 0))
    out_specs, out_shape = [out_block], [jax.ShapeDtypeStruct((n, W), BF16)]
    if want_v:
        out_specs.append(out_block)
        out_shape.append(jax.ShapeDtypeStruct((n, W), F32))
    return pl.pallas_call(
        _cm_kernel,
        grid=(n // rows,),
        in_specs=[zcol(0), zcol(1)] + [_layer_resident(w, layer) for w in (ws, bs_wide, ln_g, ln_b)],
        out_specs=out_specs,
        out_shape=out_shape,
        compiler_params=_params(("parallel",)),
        name="chunk_gmlp",
    )(z, z, ws, bs_wide, ln_g, ln_b)


def _dense_weights(p):
    vec = lambda a: a[:, None, :]
    out = {k: p[k].astype(BF16) for k in ("ffn1_w_in", "ffn1_w_out", "mix_w_in", "mix_w_out",
                                           "ffn2_w_in", "ffn2_w_out")}
    out.update({k: vec(p[k]) for k in ("ln1_g", "ln1_b", "ln2_g", "ln2_b", "ln3_g", "ln3_b")})
    return out


def _mixer_weights(p, lb):
    vec = lambda a: a.reshape(DEPTH, 1, -1)
    lora_pad = lambda w, start: jnp.pad(w, ((0, 0), (start, RW_WIDTH - start - w.shape[1]), (0, 0)))
    return dict(
        rw=dict(mu=vec(p["rw_mu"]), w0=vec(p["rw_w0"]),
                w_w2=lora_pad(p["rw_w_w2"], 0), a0=vec(p["rw_a0"]),
                a_w2=lora_pad(p["rw_a_w2"], RW_DECAY_LORA),
                g_w2=lora_pad(p["rw_g_w2"], RW_DECAY_LORA + RW_AAA_LORA),
                k_k=vec(p["rw_k_k"]), k_a=vec(p["rw_k_a"]), r_k=vec(p["rw_r_k"]),
                gn_g=vec(p["rw_gn_g"]), gn_b=vec(p["rw_gn_b"])),
        hg_lb=vec(lb), hg_norm_g=vec(p["hg_norm_g"]),
        cm_ws=p["cm_ws"],
        cm_bs=jnp.repeat(jnp.swapaxes(p["cm_bs"], 1, 2), CM_DIM, axis=2),
        cm_ln_g=vec(p["cm_ln_g"]), cm_ln_b=vec(p["cm_ln_b"]),
    )


def _short_chunk_mixing(ws, bs_wide, t_pad):
    reps = CM_CHUNK // t_pad
    pos = jnp.arange(CM_CHUNK)
    sel = (pos[None, :] == (pos % t_pad)[:, None]).astype(ws.dtype)
    tiled = jnp.einsum("rk,lhkm,cm->lhrc", sel, ws, sel, precision=lax.Precision.HIGHEST)
    same_seq = (pos[:, None] // t_pad) == (pos[None, :] // t_pad)
    return jnp.where(same_seq, tiled, 0.0), jnp.tile(bs_wide[:, :t_pad], (1, reps, 1))


class _TokenGroup:
    def __init__(self, x, rw_s0, rw_shift0, hg_s0, w):
        self.B, self.T, _ = x.shape
        self.x = x.reshape(self.B * self.T, D_MODEL)
        self.has_state = rw_s0 is not None
        self.rw_s0, self.hg_s0 = rw_s0, hg_s0
        self.rw_shift0 = rw_shift0[:, :, None, :] if self.has_state else None
        T = self.T
        self.short = T < RW_CHUNK
        if self.short:
            slot = -(-T // SUBLANES) * SUBLANES
            self.rw_chunk = self.hg_chunk = self.rw_rows = self.hg_rows = slot
            self.n_seq = SAMPLE_STEP_SEQS
            self.cm_ws, self.cm_bs = _short_chunk_mixing(w["cm_ws"], w["cm_bs"], T)
        else:
            self.rw_chunk, self.hg_chunk, self.n_seq = RW_CHUNK, HG_CHUNK, 1
            self.rw_rows, self.hg_rows = min(T, RW_STEP_ROWS), min(T, HG_STEP_ROWS)
            self.cm_ws, self.cm_bs = w["cm_ws"], w["cm_bs"]
        self.rw_states, self.hg_states, self.rw_shifts, self.cm_vs = None, None, [], []

    def mix(self, z, layer, w):
        B, T = self.B, self.T
        o_rw, last, self.rw_states = _rwkv(z, self.rw_shift0, self.rw_s0, self.rw_states, layer, w["rw"],
                                           B, min(T, self.rw_rows), self.rw_chunk, self.n_seq, self.rw_rows)
        self.rw_shifts.append(last.reshape(B, RW_PROJ))
        o_hg, self.hg_states = _hgrn(z, self.hg_s0, self.hg_states, layer, w["hg_lb"], w["hg_norm_g"],
                                     B, min(T, self.hg_rows), self.hg_chunk, self.n_seq, self.hg_rows)
        cm_out = _chunk_mlp(z, layer, self.cm_ws, self.cm_bs, w["cm_ln_g"], w["cm_ln_b"],
                            want_v=self.has_state)
        if self.has_state:
            self.cm_vs.append(cm_out[1].reshape(B, T, CM_WIDTH))
        return o_rw, o_hg, cm_out[0]


def _run_trunk(groups, dw, w):
    for l in range(DEPTH):
        projected = _dense_in([g.x for g in groups], l, dw["ffn1_w_in"], dw["ffn1_w_out"],
                              dw["ln1_g"], dw["ln1_b"], dw["mix_w_in"])
        mixed = [(x1,) + g.mix(z, l, w) for g, (x1, z) in zip(groups, projected)]
        ys = _dense_out(mixed, l, dw["mix_w_out"], dw["ln2_g"], dw["ln2_b"], dw["ffn2_w_in"],
                        dw["ffn2_w_out"], dw["ln3_g"], dw["ln3_b"])
        for g, y in zip(groups, ys):
            g.x = y


def kernel(x_prompt, x_sample, state_rwkv, state_rwkv_shift, state_hgrn, ffn1_w_in, ffn1_w_out, ln1_g, ln1_b, mix_w_in, mix_w_out, ln2_g, ln2_b, rw_mu, rw_w0, rw_w_w2, rw_a0, rw_a_w2, rw_g_w2, rw_k_k, rw_k_a, rw_r_k, rw_gn_g, rw_gn_b, hg_lb_logits, hg_norm_g, cm_ws, cm_bs, cm_ln_g, cm_ln_b, ffn2_w_in, ffn2_w_out, ln3_g, ln3_b):
    p = dict(ffn1_w_in=ffn1_w_in, ffn1_w_out=ffn1_w_out, ln1_g=ln1_g, ln1_b=ln1_b,
             mix_w_in=mix_w_in, mix_w_out=mix_w_out, ln2_g=ln2_g, ln2_b=ln2_b,
             rw_mu=rw_mu, rw_w0=rw_w0, rw_w_w2=rw_w_w2, rw_a0=rw_a0, rw_a_w2=rw_a_w2,
             rw_g_w2=rw_g_w2, rw_k_k=rw_k_k, rw_k_a=rw_k_a, rw_r_k=rw_r_k,
             rw_gn_g=rw_gn_g, rw_gn_b=rw_gn_b, hg_norm_g=hg_norm_g,
             cm_ws=cm_ws, cm_bs=cm_bs, cm_ln_g=cm_ln_g, cm_ln_b=cm_ln_b,
             ffn2_w_in=ffn2_w_in, ffn2_w_out=ffn2_w_out, ln3_g=ln3_g, ln3_b=ln3_b)
    s = jax.nn.softmax(hg_lb_logits.astype(F32), axis=0)
    lb = jnp.cumsum(s, axis=0) - s[0]
    weights = _mixer_weights(p, lb)
    dw = _dense_weights(p)
    prompt = _TokenGroup(x_prompt, None, None, None, weights)
    sample = _TokenGroup(x_sample, state_rwkv, state_rwkv_shift, state_hgrn, weights)
    _run_trunk([prompt, sample], dw, weights)
    return (prompt.x.reshape(x_prompt.shape), sample.x.reshape(x_sample.shape),
            prompt.rw_states, jnp.stack(prompt.rw_shifts), prompt.hg_states,
            sample.rw_states, jnp.stack(sample.rw_shifts), sample.hg_states,
            jnp.stack(sample.cm_vs))
```

```python
import functools

import jax
import jax.numpy as jnp
from jax import lax
from jax.experimental import pallas as pl
from jax.experimental.pallas import tpu as pltpu

F32 = jnp.float32
BF16 = jnp.bfloat16

D_MODEL = 1024
DEPTH = 2
RW_HEADS, RW_DIM = 4, 64
RW_WIDTH = RW_HEADS * RW_DIM
RW_DECAY_LORA, RW_AAA_LORA, RW_GATE_LORA = 64, 64, 128
RW_PROJ = 3 * RW_WIDTH + RW_DECAY_LORA + RW_AAA_LORA + RW_GATE_LORA
RW_GN_EPS = 64e-5
HG_HEADS, HG_DIM = 4, 128
HG_WIDTH = HG_HEADS * HG_DIM
HG_PROJ = 4 * HG_WIDTH
RMS_EPS = 1e-6
F_MIN = 1e-30
CM_HEADS, CM_DIM = 4, 64
CM_WIDTH = CM_HEADS * CM_DIM
CM_CHUNK = 128
CM_PROJ = 2 * CM_WIDTH
MIX_WIDTH = RW_WIDTH + HG_WIDTH + CM_WIDTH
IN_PROJ = RW_PROJ + HG_PROJ + CM_PROJ
D_FF = 2816
LN_EPS = 1e-5
ALPHA = (2.0 * DEPTH) ** 0.25

VMEM_LIMIT_BYTES = 56 * 1024 * 1024
SUBLANES = 8
MXU_DIM = 256
DENSE_ROWS = 512
DENSE_SUB_ROWS = 256
FF_CHUNK = 2816
RW_CHUNK = 64
RW_GROUP = 64
HG_CHUNK = 64
HG_SUB = 16
EXP_CLAMP = 80.0
RW_BATCH_GROUPS = 4
RW_STEP_ROWS = 1024
HG_STEP_ROWS = 1024
SAMPLE_STEP_SEQS = 16
CM_STEP_ROWS = 1024


def _params(semantics):
    return pltpu.CompilerParams(dimension_semantics=semantics,
                                vmem_limit_bytes=VMEM_LIMIT_BYTES)


_NN = ((1,), (0,))
_NT = ((1,), (1,))
_TN = ((0,), (0,))


def _dot(a, b, dims=_NN):
    return lax.dot_general(a.astype(BF16), b.astype(BF16), (dims, ((), ())),
                           preferred_element_type=F32)


def _mask_dot(mask, x):
    m = mask.astype(BF16)
    hi = x.astype(BF16)
    r1 = x - hi.astype(F32)
    mid = r1.astype(BF16)
    lo = (r1 - mid.astype(F32)).astype(BF16)
    dot = lambda t: jnp.dot(m, t, preferred_element_type=F32)
    return dot(hi) + (dot(mid) + dot(lo))


def _chunk_cumsum(x, chunk):
    n = min(x.shape[0], MXU_DIM)
    r = lax.broadcasted_iota(jnp.int32, (n, n), 0)
    c = lax.broadcasted_iota(jnp.int32, (n, n), 1)
    mask = ((r // chunk) == (c // chunk)) & (c <= r)
    pieces = [_mask_dot(mask, x[lo:lo + n]) for lo in range(0, x.shape[0], n)]
    return pieces[0] if len(pieces) == 1 else jnp.concatenate(pieces, axis=0)


def _slot_matrix(n_seq, t_valid, slot, transpose=False):
    shape = (n_seq * slot, n_seq * t_valid)
    padded = lax.broadcasted_iota(jnp.int32, shape[::-1] if transpose else shape, 1 if transpose else 0)
    packed = lax.broadcasted_iota(jnp.int32, shape[::-1] if transpose else shape, 0 if transpose else 1)
    return ((padded // slot) == (packed // t_valid)) & ((padded % slot) == (packed % t_valid))


def _unpack_rows(x, n_seq, t_valid, slot):
    return _mask_dot(_slot_matrix(n_seq, t_valid, slot), x)


def _pack_rows(x, n_seq, t_valid, slot):
    sel = _slot_matrix(n_seq, t_valid, slot, transpose=True).astype(BF16)
    return jnp.dot(sel, x, preferred_element_type=F32).astype(x.dtype)


def _layer_norm(x, g, b):
    mu = jnp.mean(x, axis=-1, keepdims=True)
    xc = x - mu
    var = jnp.mean(xc * xc, axis=-1, keepdims=True)
    return xc * lax.rsqrt(var + LN_EPS) * g + b


def _swiglu(xb, w_in_ref, w_out_ref):
    acc = None
    for lo in range(0, D_FF, FF_CHUNK):
        gate = jnp.dot(xb, w_in_ref[:, lo:lo + FF_CHUNK], preferred_element_type=F32)
        up = jnp.dot(xb, w_in_ref[:, D_FF + lo:D_FF + lo + FF_CHUNK], preferred_element_type=F32)
        h = (gate * jax.nn.sigmoid(gate) * up).astype(BF16)
        part = jnp.dot(h, w_out_ref[lo:lo + FF_CHUNK, :], preferred_element_type=F32)
        acc = part if acc is None else acc + part
    return acc


def _sub_tiles(n_rows):
    sub = min(DENSE_SUB_ROWS, n_rows)
    return [slice(lo, lo + sub) for lo in range(0, n_rows, sub)]


def _layer_resident(stacked, layer):
    tail = (0,) * (stacked.ndim - 1)
    return pl.BlockSpec((None,) + stacked.shape[1:], lambda *_: (layer,) + tail,
                        pipeline_mode=pl.Buffered(1))


def _grouped_row_call(body, name, groups, in_widths, out_widths, weights, layer):
    rows = min(DENSE_ROWS, min(g[0].shape[0] for g in groups))
    steps = [g[0].shape[0] // rows for g in groups]
    starts = [sum(steps[:k]) for k in range(len(groups))]
    n_in, n_out, n_w = len(in_widths), len(out_widths), len(weights)

    def kern(*refs):
        ins, w_refs, outs = refs[:len(groups) * n_in], refs[len(groups) * n_in:][:n_w], refs[-len(groups) * n_out:]
        step = pl.program_id(0)
        for k in range(len(groups)):
            run = functools.partial(body, *ins[k * n_in:(k + 1) * n_in], *w_refs,
                                    *outs[k * n_out:(k + 1) * n_out])
            if len(groups) == 1:
                run()
            else:
                pl.when((step >= starts[k]) & (step < starts[k] + steps[k]))(run)

    def block(width, k):
        return pl.BlockSpec((rows, width), lambda i: (jnp.clip(i - starts[k], 0, steps[k] - 1), 0))

    outs = pl.pallas_call(
        kern,
        grid=(sum(steps),),
        in_specs=[block(wd, k) for k in range(len(groups)) for wd in in_widths]
                 + [_layer_resident(w, layer) for w in weights],
        out_specs=[block(wd, k) for k in range(len(groups)) for wd in out_widths],
        out_shape=[jax.ShapeDtypeStruct((g[0].shape[0], wd), F32) for g in groups for wd in out_widths],
        compiler_params=_params(("arbitrary",)),
        name=name,
    )(*[a for g in groups for a in g], *weights)
    return [tuple(outs[k * n_out:(k + 1) * n_out]) for k in range(len(groups))]


def _dense_in_body(x_ref, w_in_ref, w_out_ref, g_ref, b_ref, w_mix_ref, x1_ref, z_ref):
    tiles = _sub_tiles(x_ref.shape[0])
    xs = [x_ref[t, :] for t in tiles]
    ffn = [_swiglu(x.astype(BF16), w_in_ref, w_out_ref) for x in xs]
    ys = [_layer_norm(ALPHA * x + 0.5 * f, g_ref[...], b_ref[...]) for x, f in zip(xs, ffn)]
    for t, y in zip(tiles, ys):
        x1_ref[t, :] = y
        z_ref[t, :] = jnp.dot(y.astype(BF16), w_mix_ref[...], preferred_element_type=F32)


def _dense_in(xs, layer, w_in, w_out, g, b, w_mix):
    return [_grouped_row_call(_dense_in_body, "dense_in", [(x,)], (D_MODEL,), (D_MODEL, IN_PROJ),
                              (w_in, w_out, g, b, w_mix), layer)[0] for x in xs]


def _dense_out_body(x_ref, orw_ref, ohg_ref, ocm_ref, wmix_ref, g2_ref, b2_ref,
                    w_in_ref, w_out_ref, g3_ref, b3_ref, y_ref):
    hg0, cm0 = RW_WIDTH, RW_WIDTH + HG_WIDTH
    tiles = _sub_tiles(x_ref.shape[0])
    mix = [jnp.dot(orw_ref[t, :], wmix_ref[:hg0, :], preferred_element_type=F32)
           + jnp.dot(ohg_ref[t, :], wmix_ref[hg0:cm0, :], preferred_element_type=F32)
           + jnp.dot(ocm_ref[t, :], wmix_ref[cm0:, :], preferred_element_type=F32) for t in tiles]
    x2 = [_layer_norm(ALPHA * x_ref[t, :] + m, g2_ref[...], b2_ref[...]) for t, m in zip(tiles, mix)]
    ffn = [_swiglu(x.astype(BF16), w_in_ref, w_out_ref) for x in x2]
    for t, x, f in zip(tiles, x2, ffn):
        y_ref[t, :] = _layer_norm(ALPHA * x + 0.5 * f, g3_ref[...], b3_ref[...])


def _dense_out(groups, layer, w_mix, g2, b2, w_in, w_out, g3, b3):
    ys = _grouped_row_call(_dense_out_body, "dense_out", groups,
                           (D_MODEL, RW_WIDTH, HG_WIDTH, CM_WIDTH), (D_MODEL,),
                           (w_mix, g2, b2, w_in, w_out, g3, b3), layer)
    return [y for (y,) in ys]


def _head_stack(x, head_masks):
    return jnp.concatenate([x * m for m in head_masks], axis=0)


def _rwkv_kernel(*refs, chunk, t_valid, n_seq, seq_rows, has_state, multi_step, layer):
    earlier_ref, refs = (refs[0], refs[1:]) if layer else (None, refs)
    if multi_step:
        *refs, s_scr, prev_scr = refs
    if has_state:
        (zr_ref, zk_ref, zv_ref, zx_ref, shift_ref, s0_ref, mu_ref, w0_ref, ww2_ref, a0_ref, aw2_ref,
         gw2_ref, kk_ref, ka_ref, rk_ref, gng_ref, gnb_ref, o_ref, last_ref, s_out_ref) = refs
    else:
        (zr_ref, zk_ref, zv_ref, zx_ref, mu_ref, w0_ref, ww2_ref, a0_ref, aw2_ref,
         gw2_ref, kk_ref, ka_ref, rk_ref, gng_ref, gnb_ref, o_ref, last_ref, s_out_ref) = refs
    C, N, H, W, G = chunk, RW_DIM, RW_HEADS, RW_WIDTH, RW_GROUP
    L = seq_rows
    R = n_seq * L

    def initial_state(b):
        if has_state:
            return jnp.concatenate([s0_ref[b, h] for h in range(H)], axis=1)
        return jnp.zeros((N, W), F32)

    def initial_prev(b):
        return shift_ref[b] if has_state else jnp.zeros((1, RW_PROJ), F32)

    if multi_step:
        step = pl.program_id(1)

        @pl.when(step == 0)
        def _():
            for b in range(n_seq):
                s_scr[b] = initial_state(b)
                prev_scr[b:b + 1, :] = initial_prev(b)

        carried_state = lambda b: s_scr[b]
        carried_prev = lambda b, cols: prev_scr[b:b + 1, cols]
    else:
        carried_state = initial_state
        carried_prev = lambda b, cols: initial_prev(b)[:, cols]

    RB = min(RW_BATCH_GROUPS * G, R)
    n_batches = R // RB
    assert (n_seq == 1 and t_valid == L) or n_batches == 1
    assert C == G or C == L
    brow = lax.broadcasted_iota(jnp.int32, (RB, 1), 0)
    lane = lax.broadcasted_iota(jnp.int32, (1, W), 1)
    head_masks = [((lane >= h * N) & (lane < (h + 1) * N)).astype(F32) for h in range(H)]
    hr = lax.broadcasted_iota(jnp.int32, (W, W), 0)
    hc = lax.broadcasted_iota(jnp.int32, (W, W), 1)
    head_ones = ((hr // N) == (hc // N)).astype(F32)
    elem = {}

    def prologue(bi):
        lo = bi * RB
        t_in = (brow + lo) % L
        valid = t_in < t_valid

        def load(ref, part):
            cols = slice(part * W, (part + 1) * W)
            if t_valid < L:
                z = _unpack_rows(ref[...], n_seq, t_valid, L)
            else:
                z = ref[lo:lo + RB, :]
            if n_seq == 1:
                first = carried_prev(0, cols) if bi == 0 else ref[lo - 1:lo, :]
                prev = jnp.where(brow == 0, first, pltpu.roll(z, 1, axis=0))
            else:
                first = jnp.concatenate([jnp.broadcast_to(carried_prev(b, cols), (L, W))
                                         for b in range(n_seq)], axis=0)
                prev = jnp.where(t_in == 0, first, pltpu.roll(z, 1, axis=0))
            if multi_step:
                for b in range(n_seq):
                    last = b * L + t_valid - 1 - lo
                    if 0 <= last < RB:
                        prev_scr[b:b + 1, cols] = z[last:last + 1, :]
            return z + (prev - z) * mu_ref[:, cols]

        r = load(zr_ref, 0)
        k = load(zk_ref, 1)
        v = load(zv_ref, 2)
        x4 = load(zx_ref, 3)

        w_pre = w0_ref[...] + _dot(jnp.tanh(x4), ww2_ref[...])
        yield
        nw = -w_pre
        softplus = jnp.maximum(nw, 0.0) + jnp.log(1.0 + jnp.exp(-jnp.abs(nw)))
        lw = -jnp.exp(-softplus - 0.5)
        a = jax.nn.sigmoid(a0_ref[...] + _dot(x4, aw2_ref[...]))
        yield
        gate = _dot(jax.nn.sigmoid(x4), gw2_ref[...])
        yield
        kk = k * kk_ref[...]
        k = k * (1.0 + (a - 1.0) * ka_ref[...])
        sums = _dot(jnp.concatenate([kk * kk, r * k * rk_ref[...]], axis=0), head_ones)
        yield
        kk = kk / jnp.maximum(jnp.sqrt(sums[:RB]), 1e-12)
        bonus = sums[RB:] * v
        if t_valid < L:
            lw = jnp.where(valid, lw, 0.0)
            kk = jnp.where(valid, kk, 0.0)
            k = jnp.where(valid, k, 0.0)
            v = jnp.where(valid, v, 0.0)
        cum = _chunk_cumsum(lw, C)
        yield
        g_in = jnp.exp(cum)
        g_inv = jnp.exp(-cum)
        elem[bi] = dict(a_hat=-kk * jnp.exp(cum - lw), b_chk=kk * a * g_inv, k_chk=k * g_inv,
                        r_hat=r * g_in, v=v, g_in=g_in, bonus=bonus, gate=gate)

    HG_ = H * G
    sr = lax.broadcasted_iota(jnp.int32, (HG_, HG_), 0)
    sc = lax.broadcasted_iota(jnp.int32, (HG_, HG_), 1)
    same_chunk = (sr // C) == (sc // C)
    strict = same_chunk & (sc < sr)
    incl = same_chunk & (sc <= sr)
    eye = (sr == sc).astype(F32)
    n_sub = G // C

    def gather(x, sub):
        if n_sub == 1:
            return x
        return jnp.concatenate([x[h * G + sub * C:h * G + (sub + 1) * C] for h in range(H)], axis=0)

    def scatter(pieces):
        if n_sub == 1:
            return pieces[0]
        return jnp.concatenate([pieces[sub][h * C:(h + 1) * C]
                                for h in range(H) for sub in range(n_sub)], axis=0)

    pre = {}
    gis = range(RB // G)

    def precompute(bi):
        e = elem[bi]
        a_st, b_st, k_st, r_st, v_st = {}, {}, {}, {}, {}
        for gi in gis:
            rows = slice(gi * G, (gi + 1) * G)
            a_st[gi] = _head_stack(e["a_hat"][rows], head_masks)
            b_st[gi] = _head_stack(e["b_chk"][rows], head_masks)
            k_st[gi] = _head_stack(e["k_chk"][rows], head_masks)
            r_st[gi] = _head_stack(e["r_hat"][rows], head_masks)
            v_st[gi] = jnp.concatenate([e["v"][rows, h * N:(h + 1) * N] for h in range(H)], axis=0)
        m_ab = {gi: jnp.where(strict, _dot(a_st[gi], b_st[gi], _NT), 0.0) for gi in gis}
        yield
        m_ak = {gi: jnp.where(strict, _dot(a_st[gi], k_st[gi], _NT), 0.0) for gi in gis}
        yield
        p_rb = {gi: jnp.where(incl, _dot(r_st[gi], b_st[gi], _NT), 0.0) for gi in gis}
        yield
        p_rk = {gi: jnp.where(incl, _dot(r_st[gi], k_st[gi], _NT), 0.0) for gi in gis}
        yield
        t_inv = {gi: eye + m_ab[gi] for gi in gis}
        power = dict(m_ab)
        span = 2
        while span < C:
            power = {gi: _dot(power[gi], power[gi]) for gi in gis}
            yield
            t_inv = {gi: t_inv[gi] + _dot(t_inv[gi], power[gi]) for gi in gis}
            yield
            span *= 2
        w_m = {gi: _dot(t_inv[gi], a_st[gi]) for gi in gis}
        yield
        mv = {gi: _dot(m_ak[gi], v_st[gi]) for gi in gis}
        yield
        u_m = {gi: _dot(t_inv[gi], mv[gi]) for gi in gis}
        yield
        vk = {gi: [_dot(gather(v_st[gi], sub), gather(k_st[gi], sub), _TN) for sub in range(n_sub)]
              for gi in gis}
        yield
        for gi in gis:
            pre[bi, gi] = dict(b_s=b_st[gi], r_s=r_st[gi], v_s=v_st[gi], p_rb=p_rb[gi], p_rk=p_rk[gi],
                               w_m=w_m[gi], u_m=u_m[gi], vk=vk[gi])

    states = [carried_state(b) for b in range(n_seq)]

    def chain(bi):
        e = elem[bi]
        outs = []
        for gi in gis:
            g = pre[bi, gi]
            subs = range(n_sub)
            seqs = [(bi * RB + gi * G + sub * C) // L for sub in subs]
            old = [states[b] for b in seqs]
            c_parts = [_dot(gather(g["w_m"], sub), old[sub], _NT) + gather(g["u_m"], sub)
                       for sub in subs]
            yield
            upd = [_dot(c_parts[sub], gather(g["b_s"], sub), _TN) for sub in subs]
            for sub in subs:
                last = gi * G + sub * C + C - 1
                states[seqs[sub]] = (old[sub] + upd[sub] + g["vk"][sub]) * e["g_in"][last:last + 1, :]
            yield
            rs_parts = [_dot(gather(g["r_s"], sub), old[sub], _NT) for sub in subs]
            yield
            o_s = scatter(rs_parts) + _dot(jnp.concatenate([g["p_rb"], g["p_rk"]], axis=1),
                                           jnp.concatenate([scatter(c_parts), g["v_s"]], axis=0))
            mu_o = jnp.mean(o_s, axis=-1, keepdims=True)
            oc = o_s - mu_o
            var_o = jnp.mean(oc * oc, axis=-1, keepdims=True)
            on = oc * lax.rsqrt(var_o + RW_GN_EPS)
            outs.append(jnp.concatenate([on[h * G:(h + 1) * G, :] for h in range(H)], axis=1))
            yield
        o = outs[0] if len(outs) == 1 else jnp.concatenate(outs, axis=0)
        o = ((o * gng_ref[...] + gnb_ref[...] + e["bonus"]) * e["gate"]).astype(o_ref.dtype)
        if t_valid < L:
            o_ref[...] = _pack_rows(o, n_seq, t_valid, L)
        else:
            o_ref[bi * RB:(bi + 1) * RB, :] = o

    def run_interleaved(*gens):
        live = list(gens)
        while live:
            for gen in list(live):
                if next(gen, StopIteration) is StopIteration:
                    live.remove(gen)

    run_interleaved(prologue(0))
    for bi in range(n_batches):
        run_interleaved(*([precompute(bi)]
                          + ([prologue(bi + 1)] if bi + 1 < n_batches else [])
                          + ([chain(bi - 1)] if bi > 0 else [])))
    run_interleaved(chain(n_batches - 1))

    def write_final_states():
        if layer:
            s_out_ref[:layer] = earlier_ref[...]
        for b in range(n_seq):
            for h in range(H):
                s_out_ref[layer, b, h] = states[b][:, h * N:(h + 1) * N]
            row = (b + 1) * t_valid - 1
            for part, ref in enumerate((zr_ref, zk_ref, zv_ref, zx_ref)):
                last_ref[b, :, part * W:(part + 1) * W] = ref[row:row + 1, :]

    if multi_step:
        for b in range(n_seq):
            s_scr[b] = states[b]
        pl.when(step == pl.num_programs(1) - 1)(write_final_states)
    else:
        write_final_states()


def _seq_grid(n_rows, n_batch, n_seq, step_rows):
    rows_per_seq = n_rows // n_batch
    time_steps = rows_per_seq // step_rows
    assert n_seq == 1 or time_steps == 1
    return (n_batch // n_seq, time_steps), (lambda i, t: i * time_steps + t)


def _layer_state_call(kern, layer, earlier, state_shape, n_seq, **kwargs):
    tail = (0,) * (len(state_shape) - 1)
    block = lambda depth: pl.BlockSpec((depth, n_seq) + state_shape[1:], lambda i, t: (0, i) + tail)
    layer_block = pl.BlockSpec((None, n_seq) + state_shape[1:], lambda i, t: (layer, i) + tail)

    def call(args, specs, o_specs, o_shapes):
        if layer:
            args = [earlier] + args
            specs = [block(layer)] + specs
        return pl.pallas_call(
            functools.partial(kern, layer=layer),
            in_specs=specs,
            out_specs=list(o_specs) + [block(layer + 1)],
            out_shape=list(o_shapes) + [jax.ShapeDtypeStruct((layer + 1,) + state_shape, F32)],
            **kwargs,
        )(*args)

    return layer_block, call


def _rwkv(z, shift0, s0, states_out, layer, p, n_batch, t_valid, chunk, n_seq, seq_rows):
    W = RW_WIDTH
    has_state = s0 is not None
    grid, row_block = _seq_grid(z.shape[0], n_batch, n_seq, t_valid)
    R = n_seq * t_valid
    zcol = lambda j: pl.BlockSpec((R, W), lambda i, t, j=j: (row_block(i, t), j))
    weights = (p["mu"], p["w0"], p["w_w2"], p["a0"], p["a_w2"], p["g_w2"], p["k_k"], p["k_a"],
               p["r_k"], p["gn_g"], p["gn_b"])
    multi_step = grid[1] > 1
    kern = functools.partial(_rwkv_kernel, chunk=chunk, t_valid=t_valid, n_seq=n_seq,
                             seq_rows=seq_rows, has_state=has_state, multi_step=multi_step)
    state_spec, call = _layer_state_call(
        kern, layer, states_out, (n_batch, RW_HEADS, RW_DIM, RW_DIM), n_seq,
        grid=grid,
        scratch_shapes=[pltpu.VMEM((n_seq, RW_DIM, W), F32),
                        pltpu.VMEM((n_seq, RW_PROJ), F32)] if multi_step else [],
        compiler_params=_params(("parallel", "arbitrary")),
        name="rwkv7")
    args, specs = [z, z, z, z], [zcol(0), zcol(1), zcol(2), zcol(3)]
    if has_state:
        args += [shift0, s0]
        specs += [pl.BlockSpec((None, n_seq, 1, RW_PROJ), lambda i, t: (layer, i, 0, 0)), state_spec]
    args += list(weights)
    specs += [_layer_resident(w, layer) for w in weights]
    return call(args, specs,
                [pl.BlockSpec((R, W), lambda i, t: (row_block(i, t), 0)),
                 pl.BlockSpec((n_seq, 1, RW_PROJ), lambda i, t: (i, 0, 0))],
                [jax.ShapeDtypeStruct((z.shape[0], W), BF16),
                 jax.ShapeDtypeStruct((n_batch, 1, RW_PROJ), F32)])


def _hgrn_kernel(*refs, chunk, t_valid, n_seq, seq_rows, has_state, multi_step, layer):
    earlier_ref, refs = (refs[0], refs[1:]) if layer else (None, refs)
    if multi_step:
        *refs, s_scr = refs
    if has_state:
        zq_ref, zf_ref, zi_ref, zg_ref, s0_ref, lb_ref, ng_ref, o_ref, s_out_ref = refs
    else:
        zq_ref, zf_ref, zi_ref, zg_ref, lb_ref, ng_ref, o_ref, s_out_ref = refs
    C, N, H = chunk, HG_DIM, HG_HEADS
    L = seq_rows
    R = n_seq * L
    sub = min(HG_SUB, C)
    if t_valid < L:
        load = lambda ref: _unpack_rows(ref[...], n_seq, t_valid, L)
    else:
        load = lambda ref: ref[...]

    def initial_state(b, h):
        return s0_ref[b, h].T if has_state else jnp.zeros((N, N), F32)

    if multi_step:
        step = pl.program_id(1)

        @pl.when(step == 0)
        def _():
            for b in range(n_seq):
                for h in range(H):
                    s_scr[b, h] = initial_state(b, h)

        carried_state = lambda b, h: s_scr[b, h]
    else:
        carried_state = initial_state

    zq = load(zq_ref)
    q = zq * jax.nn.sigmoid(zq)
    lb = lb_ref[...]
    f = lb + (1.0 - lb) * jax.nn.sigmoid(load(zf_ref))
    log_f = jnp.log(jnp.maximum(f, F_MIN))
    k = 1.0 - f
    v = load(zi_ref)
    zg = load(zg_ref)
    out_gate = ng_ref[...] * (zg * jax.nn.sigmoid(zg))
    if t_valid < L:
        valid = (lax.broadcasted_iota(jnp.int32, (R, 1), 0) % L) < t_valid
        log_f = jnp.where(valid, log_f, 0.0)
        k = jnp.where(valid, k, 0.0)

    cum_all = _chunk_cumsum(log_f, C)
    tr = lax.broadcasted_iota(jnp.int32, (C, C), 0)
    tc = lax.broadcasted_iota(jnp.int32, (C, C), 1)
    causal = tc <= tr

    units = [(c, h) for c in range(R // C) for h in range(H)]
    part = lambda x, c, h: x[c * C:(c + 1) * C, h * N:(h + 1) * N]
    cums = {u: part(cum_all, *u) for u in units}
    totals = {u: cums[u][C - 1:C, :] for u in units}
    scores, intra, updates, inter = {}, {}, {}, {}
    n_blk = C // sub
    states = [[carried_state(b, h) for h in range(H)] for b in range(n_seq)]

    def score_stage(u):
        cum, q_h, k_h = cums[u], part(q, *u), part(k, *u)
        blks = [slice(i * sub, (i + 1) * sub) for i in range(n_blk)]
        refs = [cum[i * sub:i * sub + 1, :] for i in range(n_blk)]
        k_own = [k_h[blks[j]] * jnp.exp(jnp.minimum(refs[j] - cum[blks[j]], EXP_CLAMP))
                 for j in range(n_blk)]
        score_rows = []
        for i in range(n_blk):
            q_hat = q_h[blks[i]] * jnp.exp(cum[blks[i]] - refs[i])
            pieces = [k_own[j] * jnp.exp(refs[i] - refs[j]) for j in range(i)] + [k_own[i]]
            if i + 1 < n_blk:
                pieces.append(jnp.zeros(((n_blk - 1 - i) * sub, N), F32))
            k_hat = pieces[0] if len(pieces) == 1 else jnp.concatenate(pieces, axis=0)
            score_rows.append(_dot(q_hat, k_hat, _NT))
        rows_ = score_rows[0] if len(score_rows) == 1 else jnp.concatenate(score_rows, axis=0)
        scores[u] = jnp.where(causal, rows_, 0.0)

    def intra_stage(u):
        intra[u] = _dot(scores[u], part(v, *u))
        updates[u] = _dot(part(v, *u), part(k, *u) * jnp.exp(totals[u] - cums[u]), _TN)

    def state_stage(u):
        c, h = u
        b = (c * C) // L
        inter[u] = _dot(part(q, *u) * jnp.exp(cums[u]), states[b][h], _NT)
        states[b][h] = states[b][h] * jnp.exp(totals[u]) + updates[u]

    stages = (score_stage, intra_stage, state_stage)
    for idx in range(len(units) + len(stages) - 1):
        for lag, stage in enumerate(stages):
            if 0 <= idx - lag < len(units):
                stage(units[idx - lag])
    out_rows = []
    for c in range(R // C):
        outs = []
        for h in range(H):
            o_h = intra[(c, h)] + inter[(c, h)]
            outs.append(o_h * lax.rsqrt(jnp.mean(o_h * o_h, axis=-1, keepdims=True) + RMS_EPS))
        out_rows.append(jnp.concatenate(outs, axis=1))
    o = out_rows[0] if len(out_rows) == 1 else jnp.concatenate(out_rows, axis=0)
    o = (o * out_gate).astype(o_ref.dtype)
    o_ref[...] = _pack_rows(o, n_seq, t_valid, L) if t_valid < L else o

    def write_final_states():
        if layer:
            s_out_ref[:layer] = earlier_ref[...]
        for b in range(n_seq):
            for h in range(H):
                s_out_ref[layer, b, h] = states[b][h].T

    if multi_step:
        for b in range(n_seq):
            for h in range(H):
                s_scr[b, h] = states[b][h]
        pl.when(step == pl.num_programs(1) - 1)(write_final_states)
    else:
        write_final_states()


def _hgrn(z, s0, states_out, layer, lb, norm_g, n_batch, t_valid, chunk, n_seq, seq_rows):
    W = HG_WIDTH
    first = RW_PROJ // W
    has_state = s0 is not None
    grid, row_block = _seq_grid(z.shape[0], n_batch, n_seq, t_valid)
    R = n_seq * t_valid
    zcol = lambda j: pl.BlockSpec((R, W), lambda i, t, j=j: (row_block(i, t), first + j))
    multi_step = grid[1] > 1
    kern = functools.partial(_hgrn_kernel, chunk=chunk, t_valid=t_valid, n_seq=n_seq,
                             seq_rows=seq_rows, has_state=has_state, multi_step=multi_step)
    state_spec, call = _layer_state_call(
        kern, layer, states_out, (n_batch, HG_HEADS, HG_DIM, HG_DIM), n_seq,
        grid=grid,
        scratch_shapes=[pltpu.VMEM((n_seq, HG_HEADS, HG_DIM, HG_DIM), F32)] if multi_step else [],
        compiler_params=_params(("parallel", "arbitrary")),
        name="hgrn2")
    args, specs = [z, z, z, z], [zcol(0), zcol(1), zcol(2), zcol(3)]
    if has_state:
        args.append(s0)
        specs.append(state_spec)
    args += [lb, norm_g]
    specs += [_layer_resident(lb, layer), _layer_resident(norm_g, layer)]
    return call(args, specs, [pl.BlockSpec((R, W), lambda i, t: (row_block(i, t), 0))],
                [jax.ShapeDtypeStruct((z.shape[0], W), BF16)])


def _gelu(x):
    return 0.5 * x * (1.0 + lax.erf(x * (2.0 ** -0.5)))


def _cm_kernel(zu_ref, zv_ref, ws_ref, bs_ref, g_ref, b_ref, o_ref, *v_out):
    H, N, W, C = CM_HEADS, CM_DIM, CM_WIDTH, CM_CHUNK
    u = _gelu(zu_ref[...])
    v = _gelu(zv_ref[...])
    hr = lax.broadcasted_iota(jnp.int32, (W, W), 0)
    hc = lax.broadcasted_iota(jnp.int32, (W, W), 1)
    head_mean = jnp.where((hr // N) == (hc // N), 1.0 / N, 0.0).astype(F32)
    vc = v - _dot(v, head_mean)
    var = _dot(vc * vc, head_mean)
    vn = vc * lax.rsqrt(var + LN_EPS) * g_ref[...] + b_ref[...]
    if v_out:
        v_out[0][...] = vn
    tr = lax.broadcasted_iota(jnp.int32, (C, C), 0)
    tc = lax.broadcasted_iota(jnp.int32, (C, C), 1)
    lane = lax.broadcasted_iota(jnp.int32, (1, W), 1)
    w_causal = [jnp.where(tc <= tr, ws_ref[h], 0.0).astype(BF16) for h in range(H)]
    head_cols = [(lane >= h * N) & (lane < (h + 1) * N) for h in range(H)]
    vb = vn.astype(BF16)
    mixed = []
    for c in range(zu_ref.shape[0] // C):
        v_c = vb[c * C:(c + 1) * C]
        acc = bs_ref[...]
        for h in range(H):
            acc = acc + jnp.dot(w_causal[h], jnp.where(head_cols[h], v_c, jnp.zeros_like(v_c)),
                                preferred_element_type=F32)
        mixed.append(acc)
    mixed = mixed[0] if len(mixed) == 1 else jnp.concatenate(mixed, axis=0)
    o_ref[...] = (u * mixed).astype(o_ref.dtype)


def _chunk_mlp(z, layer, ws, bs_wide, ln_g, ln_b, want_v):
    n, W = z.shape[0], CM_WIDTH
    rows = min(CM_STEP_ROWS, n)
    first = (RW_PROJ + HG_PROJ) // W
    zcol = lambda j: pl.BlockSpec((rows, W), lambda i, j=j: (i, first + j))
    out_block = pl.BlockSpec((rows, W), lambda i: (i, 0))
    out_specs, out_shape = [out_block], [jax.ShapeDtypeStruct((n, W), BF16)]
    if want_v:
        out_specs.append(out_block)
        out_shape.append(jax.ShapeDtypeStruct((n, W), F32))
    return pl.pallas_call(
        _cm_kernel,
        grid=(n // rows,),
        in_specs=[zcol(0), zcol(1)] + [_layer_resident(w, layer) for w in (ws, bs_wide, ln_g, ln_b)],
        out_specs=out_specs,
        out_shape=out_shape,
        compiler_params=_params(("parallel",)),
        name="chunk_gmlp",
    )(z, z, ws, bs_wide, ln_g, ln_b)


def _dense_weights(p):
    vec = lambda a: a[:, None, :]
    out = {k: p[k].astype(BF16) for k in ("ffn1_w_in", "ffn1_w_out", "mix_w_in", "mix_w_out",
                                           "ffn2_w_in", "ffn2_w_out")}
    out.update({k: vec(p[k]) for k in ("ln1_g", "ln1_b", "ln2_g", "ln2_b", "ln3_g", "ln3_b")})
    return out


def _mixer_weights(p, lb):
    vec = lambda a: a.reshape(DEPTH, 1, -1)
    lora_pad = lambda w, start: jnp.pad(w, ((0, 0), (start, RW_WIDTH - start - w.shape[1]), (0, 0)))
    return dict(
        rw=dict(mu=vec(p["rw_mu"]), w0=vec(p["rw_w0"]),
                w_w2=lora_pad(p["rw_w_w2"], 0), a0=vec(p["rw_a0"]),
                a_w2=lora_pad(p["rw_a_w2"], RW_DECAY_LORA),
                g_w2=lora_pad(p["rw_g_w2"], RW_DECAY_LORA + RW_AAA_LORA),
                k_k=vec(p["rw_k_k"]), k_a=vec(p["rw_k_a"]), r_k=vec(p["rw_r_k"]),
                gn_g=vec(p["rw_gn_g"]), gn_b=vec(p["rw_gn_b"])),
        hg_lb=vec(lb), hg_norm_g=vec(p["hg_norm_g"]),
        cm_ws=p["cm_ws"],
        cm_bs=jnp.repeat(jnp.swapaxes(p["cm_bs"], 1, 2), CM_DIM, axis=2),
        cm_ln_g=vec(p["cm_ln_g"]), cm_ln_b=vec(p["cm_ln_b"]),
    )


def _short_chunk_mixing(ws, bs_wide, t_pad):
    reps = CM_CHUNK // t_pad
    pos = jnp.arange(CM_CHUNK)
    sel = (pos[None, :] == (pos % t_pad)[:, None]).astype(ws.dtype)
    tiled = jnp.einsum("rk,lhkm,cm->lhrc", sel, ws, sel, precision=lax.Precision.HIGHEST)
    same_seq = (pos[:, None] // t_pad) == (pos[None, :] // t_pad)
    return jnp.where(same_seq, tiled, 0.0), jnp.tile(bs_wide[:, :t_pad], (1, reps, 1))


class _TokenGroup:
    def __init__(self, x, rw_s0, rw_shift0, hg_s0, w):
        self.B, self.T, _ = x.shape
        self.x = x.reshape(self.B * self.T, D_MODEL)
        self.has_state = rw_s0 is not None
        self.rw_s0, self.hg_s0 = rw_s0, hg_s0
        self.rw_shift0 = rw_shift0[:, :, None, :] if self.has_state else None
        T = self.T
        self.short = T < RW_CHUNK
        if self.short:
            slot = -(-T // SUBLANES) * SUBLANES
            self.rw_chunk = self.hg_chunk = self.rw_rows = self.hg_rows = slot
            self.n_seq = SAMPLE_STEP_SEQS
            self.cm_ws, self.cm_bs = _short_chunk_mixing(w["cm_ws"], w["cm_bs"], T)
        else:
            self.rw_chunk, self.hg_chunk, self.n_seq = RW_CHUNK, HG_CHUNK, 1
            self.rw_rows, self.hg_rows = min(T, RW_STEP_ROWS), min(T, HG_STEP_ROWS)
            self.cm_ws, self.cm_bs = w["cm_ws"], w["cm_bs"]
        self.rw_states, self.hg_states, self.rw_shifts, self.cm_vs = None, None, [], []

    def mix(self, z, layer, w):
        B, T = self.B, self.T
        o_rw, last, self.rw_states = _rwkv(z, self.rw_shift0, self.rw_s0, self.rw_states, layer, w["rw"],
                                           B, min(T, self.rw_rows), self.rw_chunk, self.n_seq, self.rw_rows)
        self.rw_shifts.append(last.reshape(B, RW_PROJ))
        o_hg, self.hg_states = _hgrn(z, self.hg_s0, self.hg_states, layer, w["hg_lb"], w["hg_norm_g"],
                                     B, min(T, self.hg_rows), self.hg_chunk, self.n_seq, self.hg_rows)
        cm_out = _chunk_mlp(z, layer, self.cm_ws, self.cm_bs, w["cm_ln_g"], w["cm_ln_b"],
                            want_v=self.has_state)
        if self.has_state:
            self.cm_vs.append(cm_out[1].reshape(B, T, CM_WIDTH))
        return o_rw, o_hg, cm_out[0]


def _run_trunk(groups, dw, w):
    for l in range(DEPTH):
        projected = _dense_in([g.x for g in groups], l, dw["ffn1_w_in"], dw["ffn1_w_out"],
                              dw["ln1_g"], dw["ln1_b"], dw["mix_w_in"])
        mixed = [(x1,) + g.mix(z, l, w) for g, (x1, z) in zip(groups, projected)]
        ys = _dense_out(mixed, l, dw["mix_w_out"], dw["ln2_g"], dw["ln2_b"], dw["ffn2_w_in"],
                        dw["ffn2_w_out"], dw["ln3_g"], dw["ln3_b"])
        for g, y in zip(groups, ys):
            g.x = y


def kernel(x_prompt, x_sample, state_rwkv, state_rwkv_shift, state_hgrn, ffn1_w_in, ffn1_w_out, ln1_g, ln1_b, mix_w_in, mix_w_out, ln2_g, ln2_b, rw_mu, rw_w0, rw_w_w2, rw_a0, rw_a_w2, rw_g_w2, rw_k_k, rw_k_a, rw_r_k, rw_gn_g, rw_gn_b, hg_lb_logits, hg_norm_g, cm_ws, cm_bs, cm_ln_g, cm_ln_b, ffn2_w_in, ffn2_w_out, ln3_g, ln3_b):
    p = dict(ffn1_w_in=ffn1_w_in, ffn1_w_out=ffn1_w_out, ln1_g=ln1_g, ln1_b=ln1_b,
             mix_w_in=mix_w_in, mix_w_out=mix_w_out, ln2_g=ln2_g, ln2_b=ln2_b,
             rw_mu=rw_mu, rw_w0=rw_w0, rw_w_w2=rw_w_w2, rw_a0=rw_a0, rw_a_w2=rw_a_w2,
             rw_g_w2=rw_g_w2, rw_k_k=rw_k_k, rw_k_a=rw_k_a, rw_r_k=rw_r_k,
             rw_gn_g=rw_gn_g, rw_gn_b=rw_gn_b, hg_norm_g=hg_norm_g,
             cm_ws=cm_ws, cm_bs=cm_bs, cm_ln_g=cm_ln_g, cm_ln_b=cm_ln_b,
             ffn2_w_in=ffn2_w_in, ffn2_w_out=ffn2_w_out, ln3_g=ln3_g, ln3_b=ln3_b)
    s = jax.nn.softmax(hg_lb_logits.astype(F32), axis=0)
    lb = jnp.cumsum(s, axis=0) - s[0]
    weights = _mixer_weights(p, lb)
    dw = _dense_weights(p)
    prompt = _TokenGroup(x_prompt, None, None, None, weights)
    sample = _TokenGroup(x_sample, state_rwkv, state_rwkv_shift, state_hgrn, weights)
    _run_trunk([prompt, sample], dw, weights)
    return (prompt.x.reshape(x_prompt.shape), sample.x.reshape(x_sample.shape),
            prompt.rw_states, jnp.stack(prompt.rw_shifts), prompt.hg_states,
            sample.rw_states, jnp.stack(sample.rw_shifts), sample.hg_states,
            jnp.stack(sample.cm_vs))
```
